```python
import math
import jax, jax.numpy as jnp
from jax import lax
import numpy as np

D_MODEL = 1024
BATCH = 4
SEQ = 8192
DEPTH = 2

HEAD_DIM = 64
GRID_W = 64
NA_HEADS = 8
NA_WIN_ROWS = 8
NA_WIN_COLS = 16
DIFF_HEADS = 4
DENSE_Q_BLOCK = 128
SWA_Q_HEADS = 16
SWA_KV_HEADS = 4
SWA_WINDOW = 128
SWA_BLOCK = 128
ROPE_THETA = 10000.0
N_EXPERTS = 256
TOP_K = 8
N_GROUPS = 8
TOPK_GROUPS = 4
EXPERT_FF = 256
SHARED_FF = 256
ROUTED_SCALE = 2.5
MOE_BLOCK = 128
LN_EPS = 1e-5
DN_ALPHA = (2 * DEPTH) ** 0.25
DN_BETA = (8 * DEPTH) ** -0.25
NA_WIDTH = NA_HEADS * HEAD_DIM
DIFF_QK_WIDTH = DIFF_HEADS * 2 * HEAD_DIM
DIFF_V_WIDTH = DIFF_HEADS * 2 * HEAD_DIM
EVEN_IN = 3 * NA_WIDTH + 2 * DIFF_QK_WIDTH + DIFF_V_WIDTH
SWA_Q_WIDTH = SWA_Q_HEADS * HEAD_DIM
SWA_KV_WIDTH = SWA_KV_HEADS * HEAD_DIM
ODD_IN = SWA_Q_WIDTH + 2 * SWA_KV_WIDTH

kernel_name = "hybrid_natten_diffattn_swa_moe_encoder"


def layer_norm(x, g, b):
    xf = x.astype(jnp.float32)
    mu = xf.mean(-1, keepdims=True)
    var = jnp.square(xf - mu).mean(-1, keepdims=True)
    y = (xf - mu) * lax.rsqrt(var + LN_EPS) * g.astype(jnp.float32) + b.astype(jnp.float32)
    return y.astype(x.dtype)


def rope_tables(seq_len):
    inv = 1.0 / (ROPE_THETA ** (jnp.arange(0, HEAD_DIM, 2, dtype=jnp.float32) / HEAD_DIM))
    ang = jnp.arange(seq_len, dtype=jnp.float32)[:, None] * inv[None, :]
    ang = jnp.concatenate([ang, ang], axis=-1)
    return jnp.cos(ang), jnp.sin(ang)


def apply_rope(t, cos, sin):
    tf = t.astype(jnp.float32)
    t1, t2 = jnp.split(tf, 2, axis=-1)
    rot = jnp.concatenate([-t2, t1], axis=-1)
    return (tf * cos[:, None, :] + rot * sin[:, None, :]).astype(t.dtype)


def neighbourhood_attention(q, k, v, rpb):
    b, s, h, d = q.shape
    rows_n = s // GRID_W
    kh = min(NA_WIN_ROWS, rows_n)
    kw = NA_WIN_COLS

    def grid(t):
        return t.reshape(b, rows_n, GRID_W, h, d).transpose(0, 3, 1, 2, 4)

    qg, kg, vg = grid(q), grid(k), grid(v)
    rows = jnp.arange(rows_n)
    row_start = jnp.clip(rows - kh // 2, 0, rows_n - kh)
    key_rows = row_start[:, None] + jnp.arange(kh)[None, :]
    k_nb = kg[:, :, key_rows]
    v_nb = vg[:, :, key_rows]
    cols = jnp.arange(GRID_W)
    col_start = jnp.clip(cols - kw // 2, 0, GRID_W - kw)
    col_off = cols[None, :] - col_start[:, None]
    col_valid = (col_off >= 0) & (col_off < kw)
    dr = key_rows - rows[:, None] + (NA_WIN_ROWS - 1)
    dc = jnp.clip(cols[None, :] - cols[:, None], -(kw - 1), kw - 1) + (kw - 1)
    bias = rpb[:, dr[:, None, :, None], dc[None, :, None, :]]
    scores = jnp.einsum('bhrqd,bhrikd->bhrqik', qg, k_nb).astype(jnp.float32) * (d ** -0.5)
    scores = scores + bias.astype(jnp.float32)[None]
    scores = jnp.where(col_valid[:, None, :], scores, -jnp.inf)
    p = jax.nn.softmax(scores.reshape(b, h, rows_n, GRID_W, kh * GRID_W), axis=-1)
    p = p.reshape(scores.shape).astype(v.dtype)
    o = jnp.einsum('bhrqik,bhrikd->bhrqd', p, v_nb)
    return o.transpose(0, 2, 3, 1, 4).reshape(b, s, h * d)


def differential_attention(q, k, v, lq1, lk1, lq2, lk2, subln_g, lambda_init, cos, sin):
    b, s, h, _, d = q.shape
    q = apply_rope(q.reshape(b, s, 2 * h, d), cos, sin).reshape(b, s, h, 2, d)
    k = apply_rope(k.reshape(b, s, 2 * h, d), cos, sin).reshape(b, s, h, 2, d)
    q1, q2 = q[:, :, :, 0].transpose(0, 2, 1, 3), q[:, :, :, 1].transpose(0, 2, 1, 3)
    k1, k2 = k[:, :, :, 0].transpose(0, 2, 1, 3), k[:, :, :, 1].transpose(0, 2, 1, 3)
    vt = v.transpose(0, 2, 1, 3)
    f32 = jnp.float32
    lam = (jnp.exp(jnp.sum(lq1.astype(f32) * lk1.astype(f32)))
           - jnp.exp(jnp.sum(lq2.astype(f32) * lk2.astype(f32))) + lambda_init)
    nb = s // DENSE_Q_BLOCK
    scale = d ** -0.5

    def to_blocks(t):
        return t.reshape(b, h, nb, DENSE_Q_BLOCK, d).transpose(2, 0, 1, 3, 4)

    def attend(qs):
        q1b, q2b = qs
        p1 = jax.nn.softmax(jnp.einsum('bhqd,bhkd->bhqk', q1b, k1).astype(f32) * scale, axis=-1)
        p2 = jax.nn.softmax(jnp.einsum('bhqd,bhkd->bhqk', q2b, k2).astype(f32) * scale, axis=-1)
        return jnp.einsum('bhqk,bhke->bhqe', (p1 - lam * p2).astype(v.dtype), vt)

    o = lax.map(attend, (to_blocks(q1), to_blocks(q2)))
    o = o.transpose(1, 0, 3, 2, 4).reshape(b, s, h, 2 * d).astype(f32)
    o = o * lax.rsqrt(jnp.mean(jnp.square(o), axis=-1, keepdims=True) + LN_EPS)
    o = o * subln_g.astype(f32) * (1.0 - lambda_init)
    return o.reshape(b, s, h * 2 * d).astype(v.dtype)


def sliding_window_gqa(q, k, v, sinks, cos, sin):
    b, s, hq, d = q.shape
    hkv = k.shape[2]
    g = hq // hkv
    q = apply_rope(q, cos, sin)
    k = apply_rope(k, cos, sin)
    nb = s // SWA_BLOCK
    qb = q.reshape(b, nb, SWA_BLOCK, hkv, g, d)

    def band(t):
        tp = jnp.pad(t, ((0, 0), (SWA_BLOCK, SWA_BLOCK), (0, 0), (0, 0)))
        tp = tp.reshape(b, nb + 2, SWA_BLOCK, hkv, d)
        return jnp.concatenate([tp[:, :nb], tp[:, 1:nb + 1], tp[:, 2:nb + 2]], axis=2)

    kb, vb = band(k), band(v)
    scores = jnp.einsum('bnqkgd,bnjkd->bnkgqj', qb, kb).astype(jnp.float32) * (d ** -0.5)
    blocks = jnp.arange(nb)[:, None]
    qpos = blocks * SWA_BLOCK + jnp.arange(SWA_BLOCK)[None, :]
    kpos = (blocks - 1) * SWA_BLOCK + jnp.arange(3 * SWA_BLOCK)[None, :]
    valid = ((jnp.abs(kpos[:, None, :] - qpos[:, :, None]) <= SWA_WINDOW)
             & (kpos >= 0)[:, None, :] & (kpos < s)[:, None, :])
    scores = jnp.where(valid[None, :, None, None], scores, -jnp.inf)
    sink = sinks.astype(jnp.float32).reshape(hkv, g)[None, None, :, :, None]
    m = jnp.maximum(scores.max(axis=-1), sink)
    e = jnp.exp(scores - m[..., None])
    denom = e.sum(axis=-1) + jnp.exp(sink - m)
    p = (e / denom[..., None]).astype(v.dtype)
    o = jnp.einsum('bnkgqj,bnjkd->bnqkgd', p, vb)
    return o.reshape(b, s, hq * d)


def even_mixer(x, w_in, w_out, rpb, lq1, lk1, lq2, lk2, subln_g, lambda_init, cos, sin):
    b, s, _ = x.shape
    hcat = x @ w_in
    cuts = np.cumsum([NA_WIDTH, NA_WIDTH, NA_WIDTH, DIFF_QK_WIDTH, DIFF_QK_WIDTH]).tolist()
    qa, ka, va, qd, kd, vd = jnp.split(hcat, cuts, axis=-1)
    shp = (b, s, NA_HEADS, HEAD_DIM)
    oa = neighbourhood_attention(qa.reshape(shp), ka.reshape(shp), va.reshape(shp), rpb)
    od = differential_attention(
        qd.reshape(b, s, DIFF_HEADS, 2, HEAD_DIM), kd.reshape(b, s, DIFF_HEADS, 2, HEAD_DIM),
        vd.reshape(b, s, DIFF_HEADS, 2 * HEAD_DIM), lq1, lk1, lq2, lk2, subln_g, lambda_init, cos, sin)
    return jnp.concatenate([oa, od], axis=-1) @ w_out


def odd_mixer(x, w_in, w_out, sinks, cos, sin):
    b, s, _ = x.shape
    hcat = x @ w_in
    q, k, v = jnp.split(hcat, [SWA_Q_WIDTH, SWA_Q_WIDTH + SWA_KV_WIDTH], axis=-1)
    o = sliding_window_gqa(q.reshape(b, s, SWA_Q_HEADS, HEAD_DIM),
                           k.reshape(b, s, SWA_KV_HEADS, HEAD_DIM),
                           v.reshape(b, s, SWA_KV_HEADS, HEAD_DIM), sinks, cos, sin)
    return o @ w_out


def moe_ffn(x, router_w, router_bias, w_gate, w_up, w_down, sh_gate, sh_up, sh_down):
    b, s, d = x.shape
    t = b * s
    xf = x.reshape(t, d)
    scores = jax.nn.sigmoid((xf @ router_w).astype(jnp.float32))
    choice = scores + router_bias.astype(jnp.float32)
    grp = choice.reshape(t, N_GROUPS, N_EXPERTS // N_GROUPS)
    grp_score = lax.top_k(grp, 2)[0].sum(axis=-1)
    _, grp_idx = lax.top_k(grp_score, TOPK_GROUPS)
    grp_mask = jnp.any(grp_idx[:, :, None] == jnp.arange(N_GROUPS)[None, None, :], axis=1)
    exp_mask = jnp.repeat(grp_mask, N_EXPERTS // N_GROUPS, axis=1)
    _, top_idx = lax.top_k(jnp.where(exp_mask, choice, -jnp.inf), TOP_K)
    gates = jnp.take_along_axis(scores, top_idx, axis=1)
    gates = gates / gates.sum(axis=-1, keepdims=True) * ROUTED_SCALE
    n_assign = t * TOP_K
    flat_e = top_idx.reshape(n_assign)
    flat_tok = jnp.repeat(jnp.arange(t, dtype=jnp.int32), TOP_K)
    flat_g = gates.reshape(n_assign)
    order = jnp.argsort(flat_e)
    e_s, tok_s, g_s = flat_e[order], flat_tok[order], flat_g[order]
    counts = jnp.bincount(flat_e, length=N_EXPERTS)
    padded = (counts + MOE_BLOCK - 1) // MOE_BLOCK * MOE_BLOCK
    pad_end = jnp.cumsum(padded)
    pad_start = pad_end - padded
    sort_start = jnp.cumsum(counts) - counts
    slot = pad_start[e_s] + jnp.arange(n_assign) - sort_start[e_s]
    n_blocks = -(-n_assign // MOE_BLOCK) + N_EXPERTS
    n_slots = n_blocks * MOE_BLOCK
    tok_buf = jnp.zeros((n_slots,), jnp.int32).at[slot].set(tok_s)
    gate_buf = jnp.zeros((n_slots,), x.dtype).at[slot].set(g_s.astype(x.dtype))
    blk_expert = jnp.minimum(
        jnp.searchsorted(pad_end, jnp.arange(n_blocks) * MOE_BLOCK, side='right'), N_EXPERTS - 1)

    def expert_block(args):
        tok_b, g_b, e = args
        xb = xf[tok_b]
        hb = jax.nn.silu(xb @ w_gate[e]) * (xb @ w_up[e])
        return (hb @ w_down[e]) * g_b[:, None]

    ys = lax.map(expert_block, (tok_buf.reshape(n_blocks, MOE_BLOCK),
                                gate_buf.reshape(n_blocks, MOE_BLOCK), blk_expert))
    routed = jax.ops.segment_sum(ys.reshape(n_slots, d), tok_buf, num_segments=t)
    shared = (jax.nn.silu(xf @ sh_gate) * (xf @ sh_up)) @ sh_down
    return (routed + shared).reshape(b, s, d)


def setup_inputs(seed: int = 0) -> dict:
    key = jax.random.key(seed)
    ks = iter(jax.random.split(key, 64))

    def nrm(shape, scale):
        return jax.random.normal(next(ks), shape, jnp.float32) * scale

    def gain(n):
        return 1.0 + nrm((n,), 0.02)

    def moe_params(prefix):
        return {
            prefix + 'router_w': nrm((D_MODEL, N_EXPERTS), D_MODEL ** -0.5),
            prefix + 'router_bias': nrm((N_EXPERTS,), 0.01),
            prefix + 'expert_w_gate': nrm((N_EXPERTS, D_MODEL, EXPERT_FF), D_MODEL ** -0.5),
            prefix + 'expert_w_up': nrm((N_EXPERTS, D_MODEL, EXPERT_FF), D_MODEL ** -0.5),
            prefix + 'expert_w_down': nrm((N_EXPERTS, EXPERT_FF, D_MODEL), EXPERT_FF ** -0.5 * DN_BETA),
            prefix + 'shared_w_gate': nrm((D_MODEL, SHARED_FF), D_MODEL ** -0.5),
            prefix + 'shared_w_up': nrm((D_MODEL, SHARED_FF), D_MODEL ** -0.5),
            prefix + 'shared_w_down': nrm((SHARED_FF, D_MODEL), SHARED_FF ** -0.5 * DN_BETA),
        }

    x = nrm((BATCH, SEQ, D_MODEL), 1.0)
    even_col_scale = jnp.concatenate([
        jnp.ones((2 * NA_WIDTH,), jnp.float32), jnp.full((NA_WIDTH,), DN_BETA, jnp.float32),
        jnp.ones((2 * DIFF_QK_WIDTH,), jnp.float32), jnp.full((DIFF_V_WIDTH,), DN_BETA, jnp.float32)])
    odd_col_scale = jnp.concatenate([
        jnp.ones((SWA_Q_WIDTH + SWA_KV_WIDTH,), jnp.float32),
        jnp.full((SWA_KV_WIDTH,), DN_BETA, jnp.float32)])
    l0 = {
        'l0_w_in': nrm((D_MODEL, EVEN_IN), D_MODEL ** -0.5) * even_col_scale,
        'l0_w_out': nrm((NA_WIDTH + DIFF_V_WIDTH, D_MODEL), (NA_WIDTH + DIFF_V_WIDTH) ** -0.5 * DN_BETA),
        'l0_na_rpb': nrm((NA_HEADS, 2 * NA_WIN_ROWS - 1, 2 * NA_WIN_COLS - 1), 0.02),
        'l0_diff_lq1': nrm((HEAD_DIM,), 0.1),
        'l0_diff_lk1': nrm((HEAD_DIM,), 0.1),
        'l0_diff_lq2': nrm((HEAD_DIM,), 0.1),
        'l0_diff_lk2': nrm((HEAD_DIM,), 0.1),
        'l0_diff_subln_g': gain(2 * HEAD_DIM),
        'l0_ln1_g': gain(D_MODEL),
        'l0_ln1_b': nrm((D_MODEL,), 0.02),
    }
    l0.update(moe_params('l0_'))
    l0.update({'l0_ln2_g': gain(D_MODEL), 'l0_ln2_b': nrm((D_MODEL,), 0.02)})
    l1 = {
        'l1_w_in': nrm((D_MODEL, ODD_IN), D_MODEL ** -0.5) * odd_col_scale,
        'l1_w_out': nrm((SWA_Q_WIDTH, D_MODEL), SWA_Q_WIDTH ** -0.5 * DN_BETA),
        'l1_swa_sinks': nrm((SWA_Q_HEADS,), 0.5),
        'l1_ln1_g': gain(D_MODEL),
        'l1_ln1_b': nrm((D_MODEL,), 0.02),
    }
    l1.update(moe_params('l1_'))
    l1.update({'l1_ln2_g': gain(D_MODEL), 'l1_ln2_b': nrm((D_MODEL,), 0.02)})
    out = {'x': x}
    out.update(l0)
    out.update(l1)
    return out


def reference(x,
              l0_w_in, l0_w_out, l0_na_rpb, l0_diff_lq1, l0_diff_lk1, l0_diff_lq2, l0_diff_lk2,
              l0_diff_subln_g, l0_ln1_g, l0_ln1_b,
              l0_router_w, l0_router_bias, l0_expert_w_gate, l0_expert_w_up, l0_expert_w_down,
              l0_shared_w_gate, l0_shared_w_up, l0_shared_w_down, l0_ln2_g, l0_ln2_b,
              l1_w_in, l1_w_out, l1_swa_sinks, l1_ln1_g, l1_ln1_b,
              l1_router_w, l1_router_bias, l1_expert_w_gate, l1_expert_w_up, l1_expert_w_down,
              l1_shared_w_gate, l1_shared_w_up, l1_shared_w_down, l1_ln2_g, l1_ln2_b):
    seq_len = x.shape[1]
    cos, sin = rope_tables(seq_len)

    def mixer0(h, l):
        lambda_init = 0.8 - 0.6 * math.exp(-0.3 * l)
        return even_mixer(h, l0_w_in, l0_w_out, l0_na_rpb, l0_diff_lq1, l0_diff_lk1,
                          l0_diff_lq2, l0_diff_lk2, l0_diff_subln_g, lambda_init, cos, sin)

    def mixer1(h, l):
        return odd_mixer(h, l1_w_in, l1_w_out, l1_swa_sinks, cos, sin)

    mixers = (mixer0, mixer1)
    moes = ((l0_router_w, l0_router_bias, l0_expert_w_gate, l0_expert_w_up, l0_expert_w_down,
             l0_shared_w_gate, l0_shared_w_up, l0_shared_w_down),
            (l1_router_w, l1_router_bias, l1_expert_w_gate, l1_expert_w_up, l1_expert_w_down,
             l1_shared_w_gate, l1_shared_w_up, l1_shared_w_down))
    norms = ((l0_ln1_g, l0_ln1_b, l0_ln2_g, l0_ln2_b),
             (l1_ln1_g, l1_ln1_b, l1_ln2_g, l1_ln2_b))
    for l in range(DEPTH):
        g1, b1, g2, b2 = norms[l]
        x = layer_norm(DN_ALPHA * x + mixers[l](x, l), g1, b1)
        x = layer_norm(DN_ALPHA * x + moe_ffn(x, *moes[l]), g2, b2)
    return x
```

```python
import functools
import math

import numpy as np
import jax
import jax.numpy as jnp
from jax import lax
from jax.experimental import pallas as pl
from jax.experimental.pallas import tpu as pltpu

F32 = jnp.float32
BF16 = jnp.bfloat16
I32 = jnp.int32

HEAD_DIM = 64
GRID_W = 64
NA_HEADS = 8
NA_WIN_ROWS = 8
NA_WIN_COLS = 16
DIFF_HEADS = 4
SWA_Q_HEADS = 16
SWA_KV_HEADS = 4
SWA_WINDOW = 128
SWA_BLOCK = 128
ROPE_THETA = 10000.0
N_EXPERTS = 256
TOP_K = 8
N_GROUPS = 8
TOPK_GROUPS = 4
ROUTED_SCALE = 2.5
LN_EPS = 1e-5
DEPTH = 2
DN_ALPHA = (2 * DEPTH) ** 0.25

LOG2E = 1.4426950408889634
NEG_BIG = -1e30
LANES = 128
NA_KEY_ROWS = NA_WIN_ROWS + 1
EXPERT_BLOCK = 256
VMEM_LIMIT = 56 * 1024 * 1024


def _cparams(sem):
    return pltpu.CompilerParams(dimension_semantics=sem, vmem_limit_bytes=VMEM_LIMIT)


def _silu(x):
    return x / (1.0 + jnp.exp(-x))


def _layer_norm(y, g, b):
    mu = jnp.mean(y, axis=-1, keepdims=True)
    yc = y - mu
    var = jnp.mean(yc * yc, axis=-1, keepdims=True)
    return yc * lax.rsqrt(var + LN_EPS) * g + b


def _inproj_body(x_ref, w_ref, tab_ref, o_ref, *, chunk, rope_kind):
    tm = x_ref.shape[0]
    n_out = w_ref.shape[1]
    x = x_ref[...].astype(BF16)
    lane = lax.broadcasted_iota(I32, (tm, LANES), 1)
    first_half = (lane % HEAD_DIM) < (HEAD_DIM // 2)
    per = chunk // LANES
    for c in range(n_out // chunk):
        h = jnp.dot(x, w_ref[:, c * chunk:(c + 1) * chunk], preferred_element_type=F32)
        kinds = rope_kind[c * per:(c + 1) * per]
        if any(kinds):
            pieces = []
            for j, kind in enumerate(kinds):
                t = h[:, j * LANES:(j + 1) * LANES]
                if kind:
                    base = (kind - 1) * 2 * LANES
                    cos = tab_ref[:, base:base + LANES]
                    sin = tab_ref[:, base + LANES:base + 2 * LANES]
                    rot = jnp.where(first_half, pltpu.roll(t, LANES - HEAD_DIM // 2, 1),
                                    pltpu.roll(t, HEAD_DIM // 2, 1))
                    t = t * cos + rot * sin
                pieces.append(t)
            h = jnp.concatenate(pieces, axis=1)
        o_ref[:, c * chunk:(c + 1) * chunk] = h.astype(o_ref.dtype)


def _inproj(x2, w_bf, tab, rope_kind, seq, tm=512, chunk=256):
    t, d = x2.shape
    n_out = w_bf.shape[1]
    tm = min(tm, seq)
    sb = seq // tm
    return pl.pallas_call(
        functools.partial(_inproj_body, chunk=chunk, rope_kind=tuple(rope_kind)),
        grid=(t // tm,),
        in_specs=[pl.BlockSpec((tm, d), lambda i: (i, 0)),
                  pl.BlockSpec((d, n_out), lambda i: (0, 0)),
                  pl.BlockSpec((tm, tab.shape[1]), lambda i: (i % sb, 0))],
        out_specs=pl.BlockSpec((tm, n_out), lambda i: (i, 0)),
        out_shape=jax.ShapeDtypeStruct((t, n_out), BF16),
        compiler_params=_cparams(("parallel",)),
        name="inproj",
    )(x2, w_bf, tab)


def _rope_table(seq):
    half = HEAD_DIM // 2
    inv = 1.0 / (ROPE_THETA ** (jnp.arange(0, HEAD_DIM, 2, dtype=F32) / HEAD_DIM))
    ang = jnp.arange(seq, dtype=F32)[:, None] * inv[None, :]
    cos = jnp.cos(ang)
    sin = jnp.sin(ang)
    cos64 = jnp.concatenate([cos, cos], axis=-1)
    sin64 = jnp.concatenate([-sin, sin], axis=-1)
    one = jnp.ones((seq, HEAD_DIM), F32)
    zero = jnp.zeros((seq, HEAD_DIM), F32)
    del half
    return jnp.concatenate([cos64, cos64, sin64, sin64, cos64, one, sin64, zero], axis=-1)


def _na_body(pat_ref, q_ref, k_ref, v_ref, bias_ref, o_ref, *, rows_n):
    del pat_ref
    r = pl.program_id(2)
    nq = q_ref.shape[0]
    nk = NA_KEY_ROWS * GRID_W
    ks = jnp.clip(2 * r - NA_WIN_ROWS // 2, 0, rows_n - NA_KEY_ROWS)
    start = pl.multiple_of(ks * GRID_W, GRID_W)
    kwin = k_ref[pl.ds(start, nk), :]
    vwin = v_ref[pl.ds(start, nk), :]
    q = q_ref[...].astype(F32)
    lane = lax.broadcasted_iota(I32, (nq, LANES), 1)
    outs = []
    for hh in range(2):
        in_head = (lane < HEAD_DIM) if hh == 0 else (lane >= HEAD_DIM)
        qm = jnp.where(in_head, q, 0.0).astype(BF16)
        s = lax.dot_general(qm, kwin, (((1,), (1,)), ((), ())), preferred_element_type=F32)
        s = s + bias_ref[0, hh]
        m = jnp.max(s, axis=-1, keepdims=True)
        p = jnp.exp2(s - m)
        l = jnp.sum(p, axis=-1, keepdims=True)
        pv = jnp.dot(p.astype(BF16), vwin, preferred_element_type=F32)
        outs.append(pv / l)
    o_ref[...] = jnp.where(lane < HEAD_DIM, outs[0], outs[1]).astype(o_ref.dtype)


def _na_bias(rpb, rows_n):
    half = NA_WIN_ROWS // 2
    pats, pat_id = [], []
    for blk in range(rows_n // 2):
        r0 = 2 * blk
        ks = min(max(r0 - half, 0), rows_n - NA_KEY_ROWS)
        starts = tuple(min(max(r0 + i - half, 0), rows_n - NA_WIN_ROWS) - ks for i in range(2))
        key = (r0 - ks, starts)
        if key not in pats:
            pats.append(key)
        pat_id.append(pats.index(key))
    wq = np.arange(GRID_W)
    wk = np.arange(GRID_W)
    col_start = np.clip(wq - NA_WIN_COLS // 2, 0, GRID_W - NA_WIN_COLS)
    col_off = wk[None, :] - col_start[:, None]
    col_valid = (col_off >= 0) & (col_off < NA_WIN_COLS)
    dc = np.clip(wk[None, :] - wq[:, None], -(NA_WIN_COLS - 1), NA_WIN_COLS - 1) + (NA_WIN_COLS - 1)
    npat = len(pats)
    nq, nk = 2 * GRID_W, NA_KEY_ROWS * GRID_W
    dr_idx = np.zeros((npat, nq, nk), np.int32)
    dc_idx = np.zeros((npat, nq, nk), np.int32)
    valid = np.zeros((npat, nq, nk), bool)
    for p, (r0rel, starts) in enumerate(pats):
        for qi in range(2):
            for kr in range(NA_KEY_ROWS):
                row_ok = starts[qi] <= kr < starts[qi] + NA_WIN_ROWS
                dr = kr - (r0rel + qi) + (NA_WIN_ROWS - 1)
                qs = slice(qi * GRID_W, (qi + 1) * GRID_W)
                kslc = slice(kr * GRID_W, (kr + 1) * GRID_W)
                dr_idx[p, qs, kslc] = min(max(dr, 0), 2 * NA_WIN_ROWS - 2)
                dc_idx[p, qs, kslc] = dc
                valid[p, qs, kslc] = col_valid & row_ok
    bias = rpb.astype(F32)[:, dr_idx, dc_idx] * LOG2E
    bias = jnp.where(valid[None], bias, NEG_BIG)
    return jnp.transpose(bias, (1, 0, 2, 3)), np.asarray(pat_id, np.int32)


def _na_attention(hcat, rpb, batch, seq):
    t = hcat.shape[0]
    rows_n = seq // GRID_W
    nblk = rows_n // 2
    nq = 2 * GRID_W
    bias, pat_id = _na_bias(rpb, rows_n)
    nk = NA_KEY_ROWS * GRID_W
    hp = NA_HEADS // 2
    grid_spec = pltpu.PrefetchScalarGridSpec(
        num_scalar_prefetch=1,
        grid=(batch, hp, nblk),
        in_specs=[
            pl.BlockSpec((nq, LANES), lambda b, h, r, pat: (b * nblk + r, h)),
            pl.BlockSpec((seq, LANES), lambda b, h, r, pat: (b, hp + h)),
            pl.BlockSpec((seq, LANES), lambda b, h, r, pat: (b, 2 * hp + h)),
            pl.BlockSpec((1, 2, nq, nk), lambda b, h, r, pat: (pat[r], h, 0, 0)),
        ],
        out_specs=pl.BlockSpec((nq, LANES), lambda b, h, r, pat: (b * nblk + r, h)),
    )
    return pl.pallas_call(
        functools.partial(_na_body, rows_n=rows_n),
        grid_spec=grid_spec,
        out_shape=jax.ShapeDtypeStruct((t, NA_HEADS * HEAD_DIM), BF16),
        compiler_params=_cparams(("parallel", "parallel", "arbitrary")),
        name="na_attn",
    )(jnp.asarray(pat_id), hcat, hcat, hcat, bias)


def _diff_body(lam_ref, q_ref, k_ref, v_ref, g_ref, o_ref, vt_ref, acc_ref, *, tk, lambda_init):
    j = pl.program_id(2)
    tq = q_ref.shape[0]
    seq = k_ref.shape[0]
    nkv = seq // tk

    @pl.when(j == 0)
    def _():
        for c in range(nkv):
            vt_ref[c] = v_ref[c * tk:(c + 1) * tk, :].astype(F32).T.astype(BF16)

    q = q_ref[...].astype(F32)
    lane = lax.broadcasted_iota(I32, (tq, LANES), 1)
    q1 = jnp.where(lane < HEAD_DIM, q, 0.0).T
    q2 = jnp.where(lane >= HEAD_DIM, q, 0.0).T
    rhs = jnp.concatenate([q1, q2], axis=1).astype(BF16)
    acc_ref[...] = jnp.zeros_like(acc_ref)

    def step(i, carry):
        m, l = carry
        kt = k_ref[pl.ds(pl.multiple_of(i * tk, tk), tk), :]
        st = jnp.dot(kt, rhs, preferred_element_type=F32)
        m_new = jnp.maximum(m, jnp.max(st, axis=0, keepdims=True))
        alpha = jnp.exp2(m - m_new)
        p = jnp.exp2(st - m_new)
        l = alpha * l + jnp.sum(p, axis=0, keepdims=True)
        pv = jnp.dot(vt_ref[i], p.astype(BF16), preferred_element_type=F32)
        acc_ref[...] = alpha * acc_ref[...] + pv
        return m_new, l

    m0 = jnp.full((1, 2 * tq), NEG_BIG, F32)
    l0 = jnp.zeros((1, 2 * tq), F32)
    _, l = lax.fori_loop(0, nkv, step, (m0, l0))
    ot = acc_ref[...] / l
    dt = ot[:, :tq] - lam_ref[0] * ot[:, tq:]
    ms = jnp.mean(dt * dt, axis=0, keepdims=True)
    y = dt * lax.rsqrt(ms + LN_EPS) * g_ref[...] * (1.0 - lambda_init)
    o_ref[...] = y.T.astype(o_ref.dtype)


def _diff_attention(hcat, lam, subln_g, lambda_init, batch, seq, tq=256, tk=512):
    t = hcat.shape[0]
    tq = min(tq, seq)
    tk = min(tk, seq)
    nq = seq // tq
    qoff = 3 * NA_HEADS * HEAD_DIM // LANES
    koff = qoff + DIFF_HEADS
    voff = koff + DIFF_HEADS
    grid_spec = pltpu.PrefetchScalarGridSpec(
        num_scalar_prefetch=1,
        grid=(batch, DIFF_HEADS, nq),
        in_specs=[
            pl.BlockSpec((tq, LANES), lambda b, h, j, lam: (b * nq + j, qoff + h)),
            pl.BlockSpec((seq, LANES), lambda b, h, j, lam: (b, koff + h)),
            pl.BlockSpec((seq, LANES), lambda b, h, j, lam: (b, voff + h)),
            pl.BlockSpec((LANES, 1), lambda b, h, j, lam: (0, 0)),
        ],
        out_specs=pl.BlockSpec((tq, LANES), lambda b, h, j, lam: (b * nq + j, h)),
        scratch_shapes=[pltpu.VMEM((seq // tk, LANES, tk), BF16),
                        pltpu.VMEM((LANES, 2 * tq), F32)],
    )
    return pl.pallas_call(
        functools.partial(_diff_body, tk=tk, lambda_init=lambda_init),
        grid_spec=grid_spec,
        out_shape=jax.ShapeDtypeStruct((t, DIFF_HEADS * 2 * HEAD_DIM), BF16),
        compiler_params=_cparams(("parallel", "parallel", "arbitrary")),
        name="diff_attn",
    )(lam, hcat, hcat, hcat, subln_g.astype(F32).reshape(LANES, 1))


def _swa_body(sink_ref, q_ref, kv_ref, o_ref):
    kvh = pl.program_id(1)
    n = pl.program_id(2)
    nq = q_ref.shape[0]
    seq = kv_ref.shape[0]
    nk = min(3 * SWA_BLOCK, seq)
    start = pl.multiple_of(jnp.clip((n - 1) * SWA_BLOCK, 0, seq - nk), SWA_BLOCK)
    kv = kv_ref[pl.ds(start, nk), :]
    kv_sw = pltpu.roll(kv.astype(F32), HEAD_DIM, 1).astype(BF16)
    qpos = n * SWA_BLOCK + lax.broadcasted_iota(I32, (nq, nk), 0)
    kpos = start + lax.broadcasted_iota(I32, (nq, nk), 1)
    valid = jnp.abs(kpos - qpos) <= SWA_WINDOW
    lane = lax.broadcasted_iota(I32, (nq, LANES), 1)
    low = lane < HEAD_DIM
    group = SWA_Q_HEADS // SWA_KV_HEADS
    for c in range(group // 2):
        q = q_ref[:, c * LANES:(c + 1) * LANES].astype(F32)
        res = []
        for odd in range(2):
            g = 2 * c + odd
            qm = jnp.where(lane >= HEAD_DIM if odd else low, q, 0.0).astype(BF16)
            kmat = kv_sw if odd else kv
            vmat = kv if odd else kv_sw
            s = lax.dot_general(qm, kmat, (((1,), (1,)), ((), ())), preferred_element_type=F32)
            s = jnp.where(valid, s, NEG_BIG)
            sink = sink_ref[kvh * group + g]
            m = jnp.maximum(jnp.max(s, axis=-1, keepdims=True), sink)
            e = jnp.exp2(s - m)
            denom = jnp.sum(e, axis=-1, keepdims=True) + jnp.exp2(sink - m)
            pv = jnp.dot(e.astype(BF16), vmat, preferred_element_type=F32)
            res.append(pv / denom)
        o_ref[:, c * LANES:(c + 1) * LANES] = jnp.where(low, res[0], res[1]).astype(o_ref.dtype)


def _swa_attention(hcat, sinks, batch, seq):
    t = hcat.shape[0]
    nb = seq // SWA_BLOCK
    group = SWA_Q_HEADS // SWA_KV_HEADS
    qw = group * HEAD_DIM
    kvoff = SWA_Q_HEADS * HEAD_DIM // LANES
    grid_spec = pltpu.PrefetchScalarGridSpec(
        num_scalar_prefetch=1,
        grid=(batch, SWA_KV_HEADS, nb),
        in_specs=[
            pl.BlockSpec((SWA_BLOCK, qw), lambda b, h, n, s: (b * nb + n, h)),
            pl.BlockSpec((seq, LANES), lambda b, h, n, s: (b, kvoff + h)),
        ],
        out_specs=pl.BlockSpec((SWA_BLOCK, qw), lambda b, h, n, s: (b * nb + n, h)),
    )
    return pl.pallas_call(
        _swa_body,
        grid_spec=grid_spec,
        out_shape=jax.ShapeDtypeStruct((t, SWA_Q_HEADS * HEAD_DIM), BF16),
        compiler_params=_cparams(("parallel", "parallel", "arbitrary")),
        name="swa_attn",
    )(sinks.astype(F32) * LOG2E, hcat, hcat)


def _outproj_body(*refs, n_in):
    a_refs = refs[:n_in]
    w_ref, x_ref, g_ref, b_ref, o_ref, obf_ref = refs[n_in:]
    acc = None
    off = 0
    for a_ref in a_refs:
        ka = a_ref.shape[1]
        d = jnp.dot(a_ref[...], w_ref[off:off + ka, :], preferred_element_type=F32)
        acc = d if acc is None else acc + d
        off += ka
    y = DN_ALPHA * x_ref[...] + acc
    out = _layer_norm(y, g_ref[...], b_ref[...])
    o_ref[...] = out
    obf_ref[...] = out.astype(BF16)


def _outproj_ln(acts, w_bf, x2, g, b, tm=512):
    t, d = x2.shape
    tm = min(tm, t)
    in_specs = [pl.BlockSpec((tm, a.shape[1]), lambda i: (i, 0)) for a in acts]
    in_specs += [pl.BlockSpec(w_bf.shape, lambda i: (0, 0)),
                 pl.BlockSpec((tm, d), lambda i: (i, 0)),
                 pl.BlockSpec((1, d), lambda i: (0, 0)),
                 pl.BlockSpec((1, d), lambda i: (0, 0))]
    return pl.pallas_call(
        functools.partial(_outproj_body, n_in=len(acts)),
        grid=(t // tm,),
        in_specs=in_specs,
        out_specs=[pl.BlockSpec((tm, d), lambda i: (i, 0)), pl.BlockSpec((tm, d), lambda i: (i, 0))],
        out_shape=[jax.ShapeDtypeStruct((t, d), F32), jax.ShapeDtypeStruct((t, d), BF16)],
        compiler_params=_cparams(("parallel",)),
        name="outproj_ln",
    )(*acts, w_bf, x2, g.astype(F32).reshape(1, d), b.astype(F32).reshape(1, d))


def _router_body(h_ref, whi_ref, wlo_ref, bias_ref, idx_ref, gate_ref, pos_ref, cnt_ref, base_ref):
    i = pl.program_id(0)
    tm = h_ref.shape[0]
    gsz = N_EXPERTS // N_GROUPS

    @pl.when(i == 0)
    def _():
        base_ref[...] = jnp.zeros_like(base_ref)

    h = h_ref[...]
    h_hi = h.astype(BF16)
    h_lo = (h - h_hi.astype(F32)).astype(BF16)
    dn = (((1,), (1,)), ((), ()))
    whi = whi_ref[...]
    logits = (lax.dot_general(whi, h_hi, dn, preferred_element_type=F32)
              + lax.dot_general(whi, h_lo, dn, preferred_element_type=F32)
              + lax.dot_general(wlo_ref[...], h_hi, dn, preferred_element_type=F32))
    scores = 1.0 / (1.0 + jnp.exp(-logits))
    choice = scores + bias_ref[...]

    iota_g = lax.broadcasted_iota(I32, (gsz, tm), 0)
    gscore = []
    for g in range(N_GROUPS):
        cg = choice[g * gsz:(g + 1) * gsz, :]
        m1 = jnp.max(cg, axis=0, keepdims=True)
        first = jnp.min(jnp.where(cg == m1, iota_g, gsz), axis=0, keepdims=True)
        m2 = jnp.max(jnp.where(iota_g == first, -jnp.inf, cg), axis=0, keepdims=True)
        gscore.append(m1 + m2)
    pieces = []
    for g in range(N_GROUPS):
        rank = jnp.zeros((1, tm), I32)
        for o in range(N_GROUPS):
            if o == g:
                continue
            beats = (gscore[o] > gscore[g]) if o > g else (gscore[o] >= gscore[g])
            rank = rank + beats.astype(I32)
        keep = rank < TOPK_GROUPS
        pieces.append(jnp.where(keep, choice[g * gsz:(g + 1) * gsz, :], -jnp.inf))
    masked = jnp.concatenate(pieces, axis=0)

    iota_e = lax.broadcasted_iota(I32, (N_EXPERTS, tm), 0)
    sel_all = jnp.zeros((N_EXPERTS, tm), F32)
    idxs, gates, sels = [], [], []
    for _ in range(TOP_K):
        mx = jnp.max(masked, axis=0, keepdims=True)
        idx = jnp.min(jnp.where(masked == mx, iota_e, N_EXPERTS), axis=0, keepdims=True)
        sel = iota_e == idx
        gates.append(jnp.sum(jnp.where(sel, scores, 0.0), axis=0, keepdims=True))
        masked = jnp.where(sel, -jnp.inf, masked)
        sel_all = sel_all + sel.astype(F32)
        idxs.append(idx)
        sels.append(sel)
    gsum = gates[0]
    for gk in gates[1:]:
        gsum = gsum + gk
    gate_ref[...] = jnp.concatenate(gates, axis=0) / gsum * ROUTED_SCALE
    idx_ref[...] = jnp.concatenate(idxs, axis=0)

    tri = (lax.broadcasted_iota(I32, (tm, tm), 0) < lax.broadcasted_iota(I32, (tm, tm), 1))
    cum = jnp.dot(sel_all.astype(BF16), tri.astype(F32).astype(BF16), preferred_element_type=F32)
    tot = cum + base_ref[...]
    pos = [jnp.sum(jnp.where(sel, tot, 0.0), axis=0, keepdims=True) for sel in sels]
    pos_ref[...] = jnp.concatenate(pos, axis=0).astype(I32)
    base_ref[...] = base_ref[...] + jnp.sum(sel_all, axis=1, keepdims=True)
    cnt_ref[...] = base_ref[...]


def _router(h, router_w, router_bias, tm=512):
    t, d = h.shape
    tm = min(tm, t)
    wt = router_w.astype(F32).T
    whi = wt.astype(BF16)
    wlo = (wt - whi.astype(F32)).astype(BF16)
    return pl.pallas_call(
        _router_body,
        grid=(t // tm,),
        in_specs=[pl.BlockSpec((tm, d), lambda i: (i, 0)),
                  pl.BlockSpec((N_EXPERTS, d), lambda i: (0, 0)),
                  pl.BlockSpec((N_EXPERTS, d), lambda i: (0, 0)),
                  pl.BlockSpec((N_EXPERTS, 1), lambda i: (0, 0))],
        out_specs=[pl.BlockSpec((TOP_K, tm), lambda i: (0, i)),
                   pl.BlockSpec((TOP_K, tm), lambda i: (0, i)),
                   pl.BlockSpec((TOP_K, tm), lambda i: (0, i)),
                   pl.BlockSpec((N_EXPERTS, 1), lambda i: (0, 0))],
        out_shape=[jax.ShapeDtypeStruct((TOP_K, t), I32),
                   jax.ShapeDtypeStruct((TOP_K, t), F32),
                   jax.ShapeDtypeStruct((TOP_K, t), I32),
                   jax.ShapeDtypeStruct((N_EXPERTS, 1), F32)],
        scratch_shapes=[pltpu.VMEM((N_EXPERTS, 1), F32)],
        compiler_params=_cparams(("arbitrary",)),
        name="router",
    )(h, whi, wlo, router_bias.astype(F32).reshape(N_EXPERTS, 1))


def _expert_body(be_ref, na_ref, x_ref, wg_ref, wu_ref, wd_ref, y_ref, wgb, wub, wdb):
    i = pl.program_id(0)

    @pl.when(i < na_ref[0])
    def _():
        prev = be_ref[jnp.maximum(i - 1, 0)]

        @pl.when((i == 0) | (be_ref[i] != prev))
        def _():
            wgb[...] = wg_ref[0].astype(BF16)
            wub[...] = wu_ref[0].astype(BF16)
            wdb[...] = wd_ref[0].astype(BF16)

        x = x_ref[...]
        g = jnp.dot(x, wgb[...], preferred_element_type=F32)
        u = jnp.dot(x, wub[...], preferred_element_type=F32)
        hmid = (_silu(g) * u).astype(BF16)
        y_ref[...] = jnp.dot(hmid, wdb[...], preferred_element_type=F32).astype(y_ref.dtype)


def _expert_matmul(xs, blk_expert, n_active, w_gate, w_up, w_down):
    n_slots, d = xs.shape
    bm = EXPERT_BLOCK
    n_blocks = n_slots // bm
    ff = w_gate.shape[2]

    def row_map(i, be, na):
        return (jnp.minimum(i, na[0] - 1), 0)

    def w_map(i, be, na):
        return (be[jnp.minimum(i, na[0] - 1)], 0, 0)

    grid_spec = pltpu.PrefetchScalarGridSpec(
        num_scalar_prefetch=2,
        grid=(n_blocks,),
        in_specs=[pl.BlockSpec((bm, d), row_map),
                  pl.BlockSpec((1, d, ff), w_map),
                  pl.BlockSpec((1, d, ff), w_map),
                  pl.BlockSpec((1, ff, d), w_map)],
        out_specs=pl.BlockSpec((bm, d), row_map),
        scratch_shapes=[pltpu.VMEM((d, ff), BF16), pltpu.VMEM((d, ff), BF16), pltpu.VMEM((ff, d), BF16)],
    )
    return pl.pallas_call(
        _expert_body,
        grid_spec=grid_spec,
        out_shape=jax.ShapeDtypeStruct((n_slots, d), BF16),
        compiler_params=_cparams(("arbitrary",)),
        name="expert_mlp",
    )(blk_expert, n_active, xs, w_gate, w_up, w_down)


def _combine_body(h_ref, yg_ref, gt_ref, sg_ref, su_ref, sd_ref, g_ref, b_ref, o_ref):
    h = h_ref[...]
    hb = h.astype(BF16)
    a = jnp.dot(hb, sg_ref[...], preferred_element_type=F32)
    u = jnp.dot(hb, su_ref[...], preferred_element_type=F32)
    moe = jnp.dot((_silu(a) * u).astype(BF16), sd_ref[...], preferred_element_type=F32)
    gt = gt_ref[...]
    for k in range(TOP_K):
        moe = moe + gt[:, k:k + 1] * yg_ref[k].astype(F32)
    o_ref[...] = _layer_norm(DN_ALPHA * h + moe, g_ref[...], b_ref[...])


def _combine_ln(h, yg, gates_t, sh_gate, sh_up, sh_down, g, b, tm=256):
    t, d = h.shape
    tm = min(tm, t)
    ff = sh_gate.shape[1]
    return pl.pallas_call(
        _combine_body,
        grid=(t // tm,),
        in_specs=[pl.BlockSpec((tm, d), lambda i: (i, 0)),
                  pl.BlockSpec((TOP_K, tm, d), lambda i: (0, i, 0)),
                  pl.BlockSpec((tm, TOP_K), lambda i: (i, 0)),
                  pl.BlockSpec((d, ff), lambda i: (0, 0)),
                  pl.BlockSpec((d, ff), lambda i: (0, 0)),
                  pl.BlockSpec((ff, d), lambda i: (0, 0)),
                  pl.BlockSpec((1, d), lambda i: (0, 0)),
                  pl.BlockSpec((1, d), lambda i: (0, 0))],
        out_specs=pl.BlockSpec((tm, d), lambda i: (i, 0)),
        out_shape=jax.ShapeDtypeStruct((t, d), F32),
        compiler_params=_cparams(("parallel",)),
        name="combine_ln",
    )(h, yg, gates_t, sh_gate.astype(BF16), sh_up.astype(BF16), sh_down.astype(BF16),
      g.astype(F32).reshape(1, d), b.astype(F32).reshape(1, d))


def _moe_layer(h, h_bf, router_w, router_bias, w_gate, w_up, w_down, sh_gate, sh_up, sh_down, g, b):
    t, d = h.shape
    bm = EXPERT_BLOCK
    idx, gates, pos, cnt = _router(h, router_w, router_bias)
    counts = cnt[:, 0].astype(I32)
    padded = (counts + bm - 1) // bm * bm
    pad_end = jnp.cumsum(padded)
    pad_start = pad_end - padded
    slot = pad_start[idx] + pos
    n_blocks = t * TOP_K // bm + N_EXPERTS
    n_slots = n_blocks * bm
    tok = jnp.broadcast_to(jnp.arange(t, dtype=I32)[None, :], (TOP_K, t))
    tok_buf = jnp.zeros((n_slots,), I32).at[slot.reshape(-1)].set(tok.reshape(-1))
    blk_expert = jnp.minimum(
        jnp.searchsorted(pad_end, jnp.arange(n_blocks, dtype=I32) * bm, side='right'),
        N_EXPERTS - 1).astype(I32)
    n_active = (pad_end[-1:] // bm).astype(I32)
    xs = jnp.take(h_bf, tok_buf, axis=0)
    ys = _expert_matmul(xs, blk_expert, n_active, w_gate, w_up, w_down)
    yg = jnp.take(ys, slot, axis=0)
    return _combine_ln(h, yg, gates.T, sh_gate, sh_up, sh_down, g, b)


def kernel(x, l0_w_in, l0_w_out, l0_na_rpb, l0_diff_lq1, l0_diff_lk1, l0_diff_lq2, l0_diff_lk2, l0_diff_subln_g, l0_ln1_g, l0_ln1_b, l0_router_w, l0_router_bias, l0_expert_w_gate, l0_expert_w_up, l0_expert_w_down, l0_shared_w_gate, l0_shared_w_up, l0_shared_w_down, l0_ln2_g, l0_ln2_b, l1_w_in, l1_w_out, l1_swa_sinks, l1_ln1_g, l1_ln1_b, l1_router_w, l1_router_bias, l1_expert_w_gate, l1_expert_w_up, l1_expert_w_down, l1_shared_w_gate, l1_shared_w_up, l1_shared_w_down, l1_ln2_g, l1_ln2_b):
    batch, seq, d = x.shape
    t = batch * seq
    x2 = x.reshape(t, d).astype(F32)
    tab = _rope_table(seq)
    qscale = HEAD_DIM ** -0.5 * LOG2E
    na_w = NA_HEADS * HEAD_DIM
    dq_w = DIFF_HEADS * 2 * HEAD_DIM

    col_scale = jnp.concatenate([
        jnp.full((na_w,), qscale, F32), jnp.ones((2 * na_w,), F32),
        jnp.full((dq_w,), qscale, F32), jnp.ones((2 * dq_w,), F32)])
    w_in0 = (l0_w_in.astype(F32) * col_scale).astype(BF16)
    per = LANES
    rope0 = [0] * (3 * na_w // per) + [1] * (2 * dq_w // per) + [0] * (dq_w // per)
    hcat0 = _inproj(x2, w_in0, tab, rope0, seq)
    oa = _na_attention(hcat0, l0_na_rpb, batch, seq)
    lambda_init = 0.8 - 0.6 * math.exp(-0.3 * 0)
    lam = (jnp.exp(jnp.sum(l0_diff_lq1.astype(F32) * l0_diff_lk1.astype(F32)))
           - jnp.exp(jnp.sum(l0_diff_lq2.astype(F32) * l0_diff_lk2.astype(F32))) + lambda_init)
    od = _diff_attention(hcat0, lam.reshape(1).astype(F32), l0_diff_subln_g, lambda_init, batch, seq)
    h, h_bf = _outproj_ln([oa, od], l0_w_out.astype(BF16), x2, l0_ln1_g, l0_ln1_b)
    x2 = _moe_layer(h, h_bf, l0_router_w, l0_router_bias, l0_expert_w_gate, l0_expert_w_up,
                    l0_expert_w_down, l0_shared_w_gate, l0_shared_w_up, l0_shared_w_down,
                    l0_ln2_g, l0_ln2_b)

    q_w = SWA_Q_HEADS * HEAD_DIM
    kv_w = SWA_KV_HEADS * HEAD_DIM
    w1 = l1_w_in.astype(F32)
    wq = w1[:, :q_w] * qscale
    wk = w1[:, q_w:q_w + kv_w].reshape(d, SWA_KV_HEADS, HEAD_DIM)
    wv = w1[:, q_w + kv_w:].reshape(d, SWA_KV_HEADS, HEAD_DIM)
    wkv = jnp.concatenate([wk, wv], axis=-1).reshape(d, 2 * kv_w)
    w_in1 = jnp.concatenate([wq, wkv], axis=1).astype(BF16)
    rope1 = [1] * (q_w // per) + [2] * (2 * kv_w // per)
    hcat1 = _inproj(x2, w_in1, tab, rope1, seq)
    o1 = _swa_attention(hcat1, l1_swa_sinks, batch, seq)
    h, h_bf = _outproj_ln([o1], l1_w_out.astype(BF16), x2, l1_ln1_g, l1_ln1_b)
    x2 = _moe_layer(h, h_bf, l1_router_w, l1_router_bias, l1_expert_w_gate, l1_expert_w_up,
                    l1_expert_w_down, l1_shared_w_gate, l1_shared_w_up, l1_shared_w_down,
                    l1_ln2_g, l1_ln2_b)
    return x2.reshape(batch, seq, d).astype(x.dtype)
```

```python
import functools
import math

import numpy as np
import jax
import jax.numpy as jnp
from jax import lax
from jax.experimental import pallas as pl
from jax.experimental.pallas import tpu as pltpu
from jax.experimental.pallas import tpu_sc as plsc

F32 = jnp.float32
BF16 = jnp.bfloat16
I32 = jnp.int32

HEAD_DIM = 64
GRID_W = 64
NA_HEADS = 8
NA_WIN_ROWS = 8
NA_WIN_COLS = 16
DIFF_HEADS = 4
SWA_Q_HEADS = 16
SWA_KV_HEADS = 4
SWA_WINDOW = 128
SWA_BLOCK = 128
ROPE_THETA = 10000.0
N_EXPERTS = 256
TOP_K = 8
N_GROUPS = 8
TOPK_GROUPS = 4
ROUTED_SCALE = 2.5
LN_EPS = 1e-5
DEPTH = 2
DN_ALPHA = (2 * DEPTH) ** 0.25

LOG2E = 1.4426950408889634
NEG_BIG = -1e30
LANES = 128
NA_KEY_ROWS = NA_WIN_ROWS + 1
EXPERT_BLOCK = 256
PACK_SPLIT = 4
SC_CORES = 2
SC_SUBCORES = 16
SC_WINDOW = 128
VMEM_LIMIT = 56 * 1024 * 1024


def _cparams(sem):
    return pltpu.CompilerParams(dimension_semantics=sem, vmem_limit_bytes=VMEM_LIMIT)


def _silu(x):
    return x / (1.0 + jnp.exp(-x))


def _pack_rows(y):
    half = y.shape[1] // 2
    bits = pltpu.bitcast(y.astype(BF16).astype(F32), I32)
    lo = lax.shift_right_logical(bits[:, :half], 16)
    hi = bits[:, half:] & jnp.int32(-65536)
    return hi | lo


def _unpack_words(w):
    lo = pltpu.bitcast(lax.shift_left(w, 16), F32)
    hi = pltpu.bitcast(w & jnp.int32(-65536), F32)
    return lo, hi


def _unpack_rows(ref_at):
    los, his = [], []
    for c in range(PACK_SPLIT):
        lo, hi = _unpack_words(ref_at(c))
        los.append(lo)
        his.append(hi)
    return jnp.concatenate(los + his, axis=1)


def _layer_norm(y, g, b):
    mu = jnp.mean(y, axis=-1, keepdims=True)
    yc = y - mu
    var = jnp.mean(yc * yc, axis=-1, keepdims=True)
    return yc * lax.rsqrt(var + LN_EPS) * g + b


def _inproj_body(x_ref, w_ref, tab_ref, o_ref, *, chunk, rope_kind):
    tm = x_ref.shape[0]
    n_out = w_ref.shape[1]
    x = x_ref[...].astype(BF16)
    lane = lax.broadcasted_iota(I32, (tm, LANES), 1)
    first_half = (lane % HEAD_DIM) < (HEAD_DIM // 2)
    per = chunk // LANES
    for c in range(n_out // chunk):
        h = jnp.dot(x, w_ref[:, c * chunk:(c + 1) * chunk], preferred_element_type=F32)
        kinds = rope_kind[c * per:(c + 1) * per]
        if any(kinds):
            pieces = []
            for j, kind in enumerate(kinds):
                t = h[:, j * LANES:(j + 1) * LANES]
                if kind:
                    base = (kind - 1) * 2 * LANES
                    cos = tab_ref[:, base:base + LANES]
                    sin = tab_ref[:, base + LANES:base + 2 * LANES]
                    rot = jnp.where(first_half, pltpu.roll(t, LANES - HEAD_DIM // 2, 1),
                                    pltpu.roll(t, HEAD_DIM // 2, 1))
                    t = t * cos + rot * sin
                pieces.append(t)
            h = jnp.concatenate(pieces, axis=1)
        o_ref[:, c * chunk:(c + 1) * chunk] = h.astype(o_ref.dtype)


def _inproj(x2, w_bf, tab, rope_kind, seq, tm=512, chunk=256):
    t, d = x2.shape
    n_out = w_bf.shape[1]
    tm = min(tm, seq)
    sb = seq // tm
    return pl.pallas_call(
        functools.partial(_inproj_body, chunk=chunk, rope_kind=tuple(rope_kind)),
        grid=(t // tm,),
        in_specs=[pl.BlockSpec((tm, d), lambda i: (i, 0)),
                  pl.BlockSpec((d, n_out), lambda i: (0, 0)),
                  pl.BlockSpec((tm, tab.shape[1]), lambda i: (i % sb, 0))],
        out_specs=pl.BlockSpec((tm, n_out), lambda i: (i, 0)),
        out_shape=jax.ShapeDtypeStruct((t, n_out), BF16),
        compiler_params=_cparams(("parallel",)),
        name="inproj",
    )(x2, w_bf, tab)


def _rope_table(seq):
    half = HEAD_DIM // 2
    inv = 1.0 / (ROPE_THETA ** (jnp.arange(0, HEAD_DIM, 2, dtype=F32) / HEAD_DIM))
    ang = jnp.arange(seq, dtype=F32)[:, None] * inv[None, :]
    cos = jnp.cos(ang)
    sin = jnp.sin(ang)
    cos64 = jnp.concatenate([cos, cos], axis=-1)
    sin64 = jnp.concatenate([-sin, sin], axis=-1)
    one = jnp.ones((seq, HEAD_DIM), F32)
    zero = jnp.zeros((seq, HEAD_DIM), F32)
    del half
    return jnp.concatenate([cos64, cos64, sin64, sin64, cos64, one, sin64, zero], axis=-1)


def _na_body(pat_ref, q_ref, k_ref, v_ref, bias_ref, o_ref, *, rows_n):
    del pat_ref
    r = pl.program_id(2)
    nq = q_ref.shape[0]
    nk = NA_KEY_ROWS * GRID_W
    ks = jnp.clip(2 * r - NA_WIN_ROWS // 2, 0, rows_n - NA_KEY_ROWS)
    start = pl.multiple_of(ks * GRID_W, GRID_W)
    kwin = k_ref[pl.ds(start, nk), :]
    vwin = v_ref[pl.ds(start, nk), :]
    q = q_ref[...].astype(F32)
    lane = lax.broadcasted_iota(I32, (nq, LANES), 1)
    outs = []
    for hh in range(2):
        in_head = (lane < HEAD_DIM) if hh == 0 else (lane >= HEAD_DIM)
        qm = jnp.where(in_head, q, 0.0).astype(BF16)
        s = lax.dot_general(qm, kwin, (((1,), (1,)), ((), ())), preferred_element_type=F32)
        s = s + bias_ref[0, hh]
        m = jnp.max(s, axis=-1, keepdims=True)
        p = jnp.exp2(s - m)
        l = jnp.sum(p, axis=-1, keepdims=True)
        pv = jnp.dot(p.astype(BF16), vwin, preferred_element_type=F32)
        outs.append(pv / l)
    o_ref[...] = jnp.where(lane < HEAD_DIM, outs[0], outs[1]).astype(o_ref.dtype)


def _na_bias(rpb, rows_n):
    half = NA_WIN_ROWS // 2
    pats, pat_id = [], []
    for blk in range(rows_n // 2):
        r0 = 2 * blk
        ks = min(max(r0 - half, 0), rows_n - NA_KEY_ROWS)
        starts = tuple(min(max(r0 + i - half, 0), rows_n - NA_WIN_ROWS) - ks for i in range(2))
        key = (r0 - ks, starts)
        if key not in pats:
            pats.append(key)
        pat_id.append(pats.index(key))
    wq = np.arange(GRID_W)
    wk = np.arange(GRID_W)
    col_start = np.clip(wq - NA_WIN_COLS // 2, 0, GRID_W - NA_WIN_COLS)
    col_off = wk[None, :] - col_start[:, None]
    col_valid = (col_off >= 0) & (col_off < NA_WIN_COLS)
    dc = np.clip(wk[None, :] - wq[:, None], -(NA_WIN_COLS - 1), NA_WIN_COLS - 1) + (NA_WIN_COLS - 1)
    npat = len(pats)
    nq, nk = 2 * GRID_W, NA_KEY_ROWS * GRID_W
    dr_idx = np.zeros((npat, nq, nk), np.int32)
    dc_idx = np.zeros((npat, nq, nk), np.int32)
    valid = np.zeros((npat, nq, nk), bool)
    for p, (r0rel, starts) in enumerate(pats):
        for qi in range(2):
            for kr in range(NA_KEY_ROWS):
                row_ok = starts[qi] <= kr < starts[qi] + NA_WIN_ROWS
                dr = kr - (r0rel + qi) + (NA_WIN_ROWS - 1)
                qs = slice(qi * GRID_W, (qi + 1) * GRID_W)
                kslc = slice(kr * GRID_W, (kr + 1) * GRID_W)
                dr_idx[p, qs, kslc] = min(max(dr, 0), 2 * NA_WIN_ROWS - 2)
                dc_idx[p, qs, kslc] = dc
                valid[p, qs, kslc] = col_valid & row_ok
    bias = rpb.astype(F32)[:, dr_idx, dc_idx] * LOG2E
    bias = jnp.where(valid[None], bias, NEG_BIG)
    return jnp.transpose(bias, (1, 0, 2, 3)), np.asarray(pat_id, np.int32)


def _na_attention(hcat, rpb, batch, seq):
    t = hcat.shape[0]
    rows_n = seq // GRID_W
    nblk = rows_n // 2
    nq = 2 * GRID_W
    bias, pat_id = _na_bias(rpb, rows_n)
    nk = NA_KEY_ROWS * GRID_W
    hp = NA_HEADS // 2
    grid_spec = pltpu.PrefetchScalarGridSpec(
        num_scalar_prefetch=1,
        grid=(batch, hp, nblk),
        in_specs=[
            pl.BlockSpec((nq, LANES), lambda b, h, r, pat: (b * nblk + r, h)),
            pl.BlockSpec((seq, LANES), lambda b, h, r, pat: (b, hp + h)),
            pl.BlockSpec((seq, LANES), lambda b, h, r, pat: (b, 2 * hp + h)),
            pl.BlockSpec((1, 2, nq, nk), lambda b, h, r, pat: (pat[r], h, 0, 0)),
        ],
        out_specs=pl.BlockSpec((nq, LANES), lambda b, h, r, pat: (b * nblk + r, h)),
    )
    return pl.pallas_call(
        functools.partial(_na_body, rows_n=rows_n),
        grid_spec=grid_spec,
        out_shape=jax.ShapeDtypeStruct((t, NA_HEADS * HEAD_DIM), BF16),
        compiler_params=_cparams(("parallel", "parallel", "arbitrary")),
        name="na_attn",
    )(jnp.asarray(pat_id), hcat, hcat, hcat, bias)


def _diff_body(lam_ref, q_ref, k_ref, v_ref, g_ref, o_ref, vt_ref, acc_ref, *, tk, lambda_init):
    j = pl.program_id(2)
    tq = q_ref.shape[0]
    seq = k_ref.shape[0]
    nkv = seq // tk

    @pl.when(j == 0)
    def _():
        for c in range(nkv):
            vt_ref[c] = v_ref[c * tk:(c + 1) * tk, :].astype(F32).T.astype(BF16)

    q = q_ref[...].astype(F32)
    lane = lax.broadcasted_iota(I32, (tq, LANES), 1)
    q1 = jnp.where(lane < HEAD_DIM, q, 0.0).T
    q2 = jnp.where(lane >= HEAD_DIM, q, 0.0).T
    rhs = jnp.concatenate([q1, q2], axis=1).astype(BF16)
    acc_ref[...] = jnp.zeros_like(acc_ref)

    def step(i, carry):
        m, l = carry
        kt = k_ref[pl.ds(pl.multiple_of(i * tk, tk), tk), :]
        st = jnp.dot(kt, rhs, preferred_element_type=F32)
        m_new = jnp.maximum(m, jnp.max(st, axis=0, keepdims=True))
        alpha = jnp.exp2(m - m_new)
        p = jnp.exp2(st - m_new)
        l = alpha * l + jnp.sum(p, axis=0, keepdims=True)
        pv = jnp.dot(vt_ref[i], p.astype(BF16), preferred_element_type=F32)
        acc_ref[...] = alpha * acc_ref[...] + pv
        return m_new, l

    m0 = jnp.full((1, 2 * tq), NEG_BIG, F32)
    l0 = jnp.zeros((1, 2 * tq), F32)
    _, l = lax.fori_loop(0, nkv, step, (m0, l0))
    ot = acc_ref[...] / l
    dt = ot[:, :tq] - lam_ref[0] * ot[:, tq:]
    ms = jnp.mean(dt * dt, axis=0, keepdims=True)
    y = dt * lax.rsqrt(ms + LN_EPS) * g_ref[...] * (1.0 - lambda_init)
    o_ref[...] = y.T.astype(o_ref.dtype)


def _diff_attention(hcat, lam, subln_g, lambda_init, batch, seq, tq=256, tk=512):
    t = hcat.shape[0]
    tq = min(tq, seq)
    tk = min(tk, seq)
    nq = seq // tq
    qoff = 3 * NA_HEADS * HEAD_DIM // LANES
    koff = qoff + DIFF_HEADS
    voff = koff + DIFF_HEADS
    grid_spec = pltpu.PrefetchScalarGridSpec(
        num_scalar_prefetch=1,
        grid=(batch, DIFF_HEADS, nq),
        in_specs=[
            pl.BlockSpec((tq, LANES), lambda b, h, j, lam: (b * nq + j, qoff + h)),
            pl.BlockSpec((seq, LANES), lambda b, h, j, lam: (b, koff + h)),
            pl.BlockSpec((seq, LANES), lambda b, h, j, lam: (b, voff + h)),
            pl.BlockSpec((LANES, 1), lambda b, h, j, lam: (0, 0)),
        ],
        out_specs=pl.BlockSpec((tq, LANES), lambda b, h, j, lam: (b * nq + j, h)),
        scratch_shapes=[pltpu.VMEM((seq // tk, LANES, tk), BF16),
                        pltpu.VMEM((LANES, 2 * tq), F32)],
    )
    return pl.pallas_call(
        functools.partial(_diff_body, tk=tk, lambda_init=lambda_init),
        grid_spec=grid_spec,
        out_shape=jax.ShapeDtypeStruct((t, DIFF_HEADS * 2 * HEAD_DIM), BF16),
        compiler_params=_cparams(("parallel", "parallel", "arbitrary")),
        name="diff_attn",
    )(lam, hcat, hcat, hcat, subln_g.astype(F32).reshape(LANES, 1))


def _swa_body(sink_ref, q_ref, kv_ref, o_ref):
    kvh = pl.program_id(1)
    n = pl.program_id(2)
    nq = q_ref.shape[0]
    seq = kv_ref.shape[0]
    nk = min(3 * SWA_BLOCK, seq)
    start = pl.multiple_of(jnp.clip((n - 1) * SWA_BLOCK, 0, seq - nk), SWA_BLOCK)
    kv = kv_ref[pl.ds(start, nk), :]
    kv_sw = pltpu.roll(kv.astype(F32), HEAD_DIM, 1).astype(BF16)
    qpos = n * SWA_BLOCK + lax.broadcasted_iota(I32, (nq, nk), 0)
    kpos = start + lax.broadcasted_iota(I32, (nq, nk), 1)
    valid = jnp.abs(kpos - qpos) <= SWA_WINDOW
    lane = lax.broadcasted_iota(I32, (nq, LANES), 1)
    low = lane < HEAD_DIM
    group = SWA_Q_HEADS // SWA_KV_HEADS
    for c in range(group // 2):
        q = q_ref[:, c * LANES:(c + 1) * LANES].astype(F32)
        res = []
        for odd in range(2):
            g = 2 * c + odd
            qm = jnp.where(lane >= HEAD_DIM if odd else low, q, 0.0).astype(BF16)
            kmat = kv_sw if odd else kv
            vmat = kv if odd else kv_sw
            s = lax.dot_general(qm, kmat, (((1,), (1,)), ((), ())), preferred_element_type=F32)
            s = jnp.where(valid, s, NEG_BIG)
            sink = sink_ref[kvh * group + g]
            m = jnp.maximum(jnp.max(s, axis=-1, keepdims=True), sink)
            e = jnp.exp2(s - m)
            denom = jnp.sum(e, axis=-1, keepdims=True) + jnp.exp2(sink - m)
            pv = jnp.dot(e.astype(BF16), vmat, preferred_element_type=F32)
            res.append(pv / denom)
        o_ref[:, c * LANES:(c + 1) * LANES] = jnp.where(low, res[0], res[1]).astype(o_ref.dtype)


def _swa_attention(hcat, sinks, batch, seq):
    t = hcat.shape[0]
    nb = seq // SWA_BLOCK
    group = SWA_Q_HEADS // SWA_KV_HEADS
    qw = group * HEAD_DIM
    kvoff = SWA_Q_HEADS * HEAD_DIM // LANES
    grid_spec = pltpu.PrefetchScalarGridSpec(
        num_scalar_prefetch=1,
        grid=(batch, SWA_KV_HEADS, nb),
        in_specs=[
            pl.BlockSpec((SWA_BLOCK, qw), lambda b, h, n, s: (b * nb + n, h)),
            pl.BlockSpec((seq, LANES), lambda b, h, n, s: (b, kvoff + h)),
        ],
        out_specs=pl.BlockSpec((SWA_BLOCK, qw), lambda b, h, n, s: (b * nb + n, h)),
    )
    return pl.pallas_call(
        _swa_body,
        grid_spec=grid_spec,
        out_shape=jax.ShapeDtypeStruct((t, SWA_Q_HEADS * HEAD_DIM), BF16),
        compiler_params=_cparams(("parallel", "parallel", "arbitrary")),
        name="swa_attn",
    )(sinks.astype(F32) * LOG2E, hcat, hcat)


def _outproj_body(*refs, n_in):
    a_refs = refs[:n_in]
    w_ref, x_ref, g_ref, b_ref, o_ref, opk_ref = refs[n_in:]
    acc = None
    off = 0
    for a_ref in a_refs:
        ka = a_ref.shape[1]
        d = jnp.dot(a_ref[...], w_ref[off:off + ka, :], preferred_element_type=F32)
        acc = d if acc is None else acc + d
        off += ka
    y = DN_ALPHA * x_ref[...] + acc
    out = _layer_norm(y, g_ref[...], b_ref[...])
    o_ref[...] = out
    packed = _pack_rows(out)
    for c in range(PACK_SPLIT):
        opk_ref[c] = packed[:, c * LANES:(c + 1) * LANES]


def _outproj_ln(acts, w_bf, x2, g, b, tm=512):
    t, d = x2.shape
    tm = min(tm, t)
    in_specs = [pl.BlockSpec((tm, a.shape[1]), lambda i: (i, 0)) for a in acts]
    in_specs += [pl.BlockSpec(w_bf.shape, lambda i: (0, 0)),
                 pl.BlockSpec((tm, d), lambda i: (i, 0)),
                 pl.BlockSpec((1, d), lambda i: (0, 0)),
                 pl.BlockSpec((1, d), lambda i: (0, 0))]
    return pl.pallas_call(
        functools.partial(_outproj_body, n_in=len(acts)),
        grid=(t // tm,),
        in_specs=in_specs,
        out_specs=[pl.BlockSpec((tm, d), lambda i: (i, 0)),
                   pl.BlockSpec((PACK_SPLIT, tm, LANES), lambda i: (0, i, 0))],
        out_shape=[jax.ShapeDtypeStruct((t, d), F32), jax.ShapeDtypeStruct((PACK_SPLIT, t, LANES), I32)],
        compiler_params=_cparams(("parallel",)),
        name="outproj_ln",
    )(*acts, w_bf, x2, g.astype(F32).reshape(1, d), b.astype(F32).reshape(1, d))


def _router_body(h_ref, whi_ref, wlo_ref, bias_ref, idx_ref, gate_ref, pos_ref, cnt_ref, base_ref):
    i = pl.program_id(0)
    tm = h_ref.shape[0]
    gsz = N_EXPERTS // N_GROUPS

    @pl.when(i == 0)
    def _():
        base_ref[...] = jnp.zeros_like(base_ref)

    h = h_ref[...]
    h_hi = h.astype(BF16)
    h_lo = (h - h_hi.astype(F32)).astype(BF16)
    dn = (((1,), (1,)), ((), ()))
    whi = whi_ref[...]
    logits = (lax.dot_general(whi, h_hi, dn, preferred_element_type=F32)
              + lax.dot_general(whi, h_lo, dn, preferred_element_type=F32)
              + lax.dot_general(wlo_ref[...], h_hi, dn, preferred_element_type=F32))
    scores = 1.0 / (1.0 + jnp.exp(-logits))
    choice = scores + bias_ref[...]

    iota_g = lax.broadcasted_iota(I32, (gsz, tm), 0)
    gscore = []
    for g in range(N_GROUPS):
        cg = choice[g * gsz:(g + 1) * gsz, :]
        m1 = jnp.max(cg, axis=0, keepdims=True)
        first = jnp.min(jnp.where(cg == m1, iota_g, gsz), axis=0, keepdims=True)
        m2 = jnp.max(jnp.where(iota_g == first, -jnp.inf, cg), axis=0, keepdims=True)
        gscore.append(m1 + m2)
    pieces = []
    for g in range(N_GROUPS):
        rank = jnp.zeros((1, tm), I32)
        for o in range(N_GROUPS):
            if o == g:
                continue
            beats = (gscore[o] > gscore[g]) if o > g else (gscore[o] >= gscore[g])
            rank = rank + beats.astype(I32)
        keep = rank < TOPK_GROUPS
        pieces.append(jnp.where(keep, choice[g * gsz:(g + 1) * gsz, :], -jnp.inf))
    masked = jnp.concatenate(pieces, axis=0)

    iota_e = lax.broadcasted_iota(I32, (N_EXPERTS, tm), 0)
    sel_all = jnp.zeros((N_EXPERTS, tm), F32)
    idxs, gates, sels = [], [], []
    for _ in range(TOP_K):
        mx = jnp.max(masked, axis=0, keepdims=True)
        idx = jnp.min(jnp.where(masked == mx, iota_e, N_EXPERTS), axis=0, keepdims=True)
        sel = iota_e == idx
        gates.append(jnp.sum(jnp.where(sel, scores, 0.0), axis=0, keepdims=True))
        masked = jnp.where(sel, -jnp.inf, masked)
        sel_all = sel_all + sel.astype(F32)
        idxs.append(idx)
        sels.append(sel)
    gsum = gates[0]
    for gk in gates[1:]:
        gsum = gsum + gk
    gate_ref[...] = jnp.concatenate(gates, axis=0) / gsum * ROUTED_SCALE
    idx_ref[...] = jnp.concatenate(idxs, axis=0)

    tri = (lax.broadcasted_iota(I32, (tm, tm), 0) < lax.broadcasted_iota(I32, (tm, tm), 1))
    cum = jnp.dot(sel_all.astype(BF16), tri.astype(F32).astype(BF16), preferred_element_type=F32)
    tot = cum + base_ref[...]
    pos = [jnp.sum(jnp.where(sel, tot, 0.0), axis=0, keepdims=True) for sel in sels]
    pos_ref[...] = jnp.concatenate(pos, axis=0).astype(I32)
    base_ref[...] = base_ref[...] + jnp.sum(sel_all, axis=1, keepdims=True)
    cnt_ref[...] = base_ref[...]


def _router(h, router_w, router_bias, tm=512):
    t, d = h.shape
    tm = min(tm, t)
    wt = router_w.astype(F32).T
    whi = wt.astype(BF16)
    wlo = (wt - whi.astype(F32)).astype(BF16)
    return pl.pallas_call(
        _router_body,
        grid=(t // tm,),
        in_specs=[pl.BlockSpec((tm, d), lambda i: (i, 0)),
                  pl.BlockSpec((N_EXPERTS, d), lambda i: (0, 0)),
                  pl.BlockSpec((N_EXPERTS, d), lambda i: (0, 0)),
                  pl.BlockSpec((N_EXPERTS, 1), lambda i: (0, 0))],
        out_specs=[pl.BlockSpec((TOP_K, tm), lambda i: (0, i)),
                   pl.BlockSpec((TOP_K, tm), lambda i: (0, i)),
                   pl.BlockSpec((TOP_K, tm), lambda i: (0, i)),
                   pl.BlockSpec((N_EXPERTS, 1), lambda i: (0, 0))],
        out_shape=[jax.ShapeDtypeStruct((TOP_K, t), I32),
                   jax.ShapeDtypeStruct((TOP_K, t), F32),
                   jax.ShapeDtypeStruct((TOP_K, t), I32),
                   jax.ShapeDtypeStruct((N_EXPERTS, 1), F32)],
        scratch_shapes=[pltpu.VMEM((N_EXPERTS, 1), F32)],
        compiler_params=_cparams(("arbitrary",)),
        name="router",
    )(h, whi, wlo, router_bias.astype(F32).reshape(N_EXPERTS, 1))


def _expert_body(be_ref, na_ref, bv_ref, x_ref, wg_ref, wu_ref, wd_ref, y_ref, wgb, wub, wdb):
    i = pl.program_id(0)
    bm = x_ref.shape[1]

    @pl.when(i < na_ref[0])
    def _():
        prev = be_ref[jnp.maximum(i - 1, 0)]

        @pl.when((i == 0) | (be_ref[i] != prev))
        def _():
            wgb[...] = wg_ref[0].astype(BF16)
            wub[...] = wu_ref[0].astype(BF16)
            wdb[...] = wd_ref[0].astype(BF16)

        keep = lax.broadcasted_iota(I32, (bm, LANES), 0) < bv_ref[i]
        x = _unpack_rows(lambda c: jnp.where(keep, x_ref[c], 0)).astype(BF16)
        g = jnp.dot(x, wgb[...], preferred_element_type=F32)
        u = jnp.dot(x, wub[...], preferred_element_type=F32)
        hmid = (_silu(g) * u).astype(BF16)
        packed = _pack_rows(jnp.dot(hmid, wdb[...], preferred_element_type=F32))
        for c in range(PACK_SPLIT):
            y_ref[c] = packed[:, c * LANES:(c + 1) * LANES]


def _expert_matmul(xs, blk_expert, n_active, blk_valid, w_gate, w_up, w_down):
    n_slots = xs.shape[1]
    bm = EXPERT_BLOCK
    n_blocks = n_slots // bm
    d, ff = w_gate.shape[1], w_gate.shape[2]

    def row_map(i, be, na, bv):
        return (0, jnp.minimum(i, na[0] - 1), 0)

    def w_map(i, be, na, bv):
        return (be[jnp.minimum(i, na[0] - 1)], 0, 0)

    grid_spec = pltpu.PrefetchScalarGridSpec(
        num_scalar_prefetch=3,
        grid=(n_blocks,),
        in_specs=[pl.BlockSpec((PACK_SPLIT, bm, LANES), row_map),
                  pl.BlockSpec((1, d, ff), w_map),
                  pl.BlockSpec((1, d, ff), w_map),
                  pl.BlockSpec((1, ff, d), w_map)],
        out_specs=pl.BlockSpec((PACK_SPLIT, bm, LANES), row_map),
        scratch_shapes=[pltpu.VMEM((d, ff), BF16), pltpu.VMEM((d, ff), BF16), pltpu.VMEM((ff, d), BF16)],
    )
    return pl.pallas_call(
        _expert_body,
        grid_spec=grid_spec,
        out_shape=jax.ShapeDtypeStruct((PACK_SPLIT, n_slots, LANES), I32),
        compiler_params=_cparams(("arbitrary",)),
        name="expert_mlp",
    )(blk_expert, n_active, blk_valid, xs, w_gate, w_up, w_down)


def _slot_body(idx_ref, pos_ref, ps_ref, slot_ref, *, n_slots):
    tm = idx_ref.shape[1]
    iota_e = lax.broadcasted_iota(I32, (N_EXPERTS, tm), 0)
    ps = ps_ref[...]
    rows = []
    for k in range(TOP_K):
        start = jnp.sum(jnp.where(iota_e == idx_ref[k:k + 1, :], ps, 0.0), axis=0, keepdims=True)
        rows.append(start.astype(I32) + pos_ref[k:k + 1, :])
    slot = jnp.concatenate(rows, axis=0)
    for c in range(PACK_SPLIT):
        slot_ref[c * TOP_K:(c + 1) * TOP_K, :] = slot + c * n_slots


def _slots(idx, pos, pad_start, n_slots, tm=512):
    t = idx.shape[1]
    tm = min(tm, t)
    return pl.pallas_call(
        functools.partial(_slot_body, n_slots=n_slots),
        grid=(t // tm,),
        in_specs=[pl.BlockSpec((TOP_K, tm), lambda i: (0, i)),
                  pl.BlockSpec((TOP_K, tm), lambda i: (0, i)),
                  pl.BlockSpec((N_EXPERTS, 1), lambda i: (0, 0))],
        out_specs=pl.BlockSpec((PACK_SPLIT * TOP_K, tm), lambda i: (0, i)),
        out_shape=jax.ShapeDtypeStruct((PACK_SPLIT * TOP_K, t), I32),
        compiler_params=_cparams(("parallel",)),
        name="slots",
    )(idx, pos, pad_start.astype(F32).reshape(N_EXPERTS, 1))


def _sc_mesh():
    return plsc.VectorSubcoreMesh(core_axis_name="core", subcore_axis_name="subcore",
                                  num_cores=SC_CORES, num_subcores=SC_SUBCORES)


def _sc_scatter_rows(rows, idx, n_out):
    t = idx.shape[1]
    nj = t // SC_WINDOW

    @functools.partial(pl.kernel, out_type=jax.ShapeDtypeStruct((n_out, LANES), I32),
                       mesh=_sc_mesh(), scratch_types=[], name="sc_dispatch")
    def run(rows_hbm, idx_hbm, out_hbm):
        def body(rows_vmem, idx_vmem):
            for k in range(TOP_K):
                pltpu.sync_copy(rows_vmem, out_hbm.at[idx_vmem.at[k]])

        pltpu.emit_pipeline(
            body,
            grid=(rows.shape[0] // SC_WINDOW,),
            in_specs=[pl.BlockSpec((SC_WINDOW, LANES), lambda s: (s, 0)),
                      pl.BlockSpec((TOP_K, SC_WINDOW), lambda s: (s // nj, s % nj))],
            out_specs=[],
            core_axis_name=("core", "subcore"),
            dimension_semantics=(pltpu.PARALLEL,),
        )(rows_hbm, idx_hbm)

    return run(rows, idx)


def _sc_gather_rows(src, idx):
    nr, t = idx.shape
    nj = t // SC_WINDOW

    @functools.partial(pl.kernel, out_type=jax.ShapeDtypeStruct((nr * t, LANES), I32),
                       mesh=_sc_mesh(), scratch_types=[], name="sc_combine")
    def run(src_hbm, idx_hbm, out_hbm):
        def body(idx_vmem, out_vmem):
            pltpu.sync_copy(src_hbm.at[idx_vmem.at[0]], out_vmem)

        pltpu.emit_pipeline(
            body,
            grid=(nr * nj,),
            in_specs=[pl.BlockSpec((1, SC_WINDOW), lambda s: (s // nj, s % nj))],
            out_specs=[pl.BlockSpec((SC_WINDOW, LANES), lambda s: (s, 0))],
            core_axis_name=("core", "subcore"),
            dimension_semantics=(pltpu.PARALLEL,),
        )(idx_hbm, out_hbm)

    return run(src, idx)


def _combine_body(h_ref, yg_ref, gt_ref, sg_ref, su_ref, sd_ref, g_ref, b_ref, o_ref):
    h = h_ref[...]
    hb = h.astype(BF16)
    a = jnp.dot(hb, sg_ref[...], preferred_element_type=F32)
    u = jnp.dot(hb, su_ref[...], preferred_element_type=F32)
    moe = jnp.dot((_silu(a) * u).astype(BF16), sd_ref[...], preferred_element_type=F32)
    gt = gt_ref[...]
    for k in range(TOP_K):
        moe = moe + gt[:, k:k + 1] * _unpack_rows(lambda c, k=k: yg_ref[c, k])
    o_ref[...] = _layer_norm(DN_ALPHA * h + moe, g_ref[...], b_ref[...])


def _combine_ln(h, yg, gates_t, sh_gate, sh_up, sh_down, g, b, tm=256):
    t, d = h.shape
    tm = min(tm, t)
    ff = sh_gate.shape[1]
    return pl.pallas_call(
        _combine_body,
        grid=(t // tm,),
        in_specs=[pl.BlockSpec((tm, d), lambda i: (i, 0)),
                  pl.BlockSpec((PACK_SPLIT, TOP_K, tm, LANES), lambda i: (0, 0, i, 0)),
                  pl.BlockSpec((tm, TOP_K), lambda i: (i, 0)),
                  pl.BlockSpec((d, ff), lambda i: (0, 0)),
                  pl.BlockSpec((d, ff), lambda i: (0, 0)),
                  pl.BlockSpec((ff, d), lambda i: (0, 0)),
                  pl.BlockSpec((1, d), lambda i: (0, 0)),
                  pl.BlockSpec((1, d), lambda i: (0, 0))],
        out_specs=pl.BlockSpec((tm, d), lambda i: (i, 0)),
        out_shape=jax.ShapeDtypeStruct((t, d), F32),
        compiler_params=_cparams(("parallel",)),
        name="combine_ln",
    )(h, yg, gates_t, sh_gate.astype(BF16), sh_up.astype(BF16), sh_down.astype(BF16),
      g.astype(F32).reshape(1, d), b.astype(F32).reshape(1, d))


def _moe_layer(h, h_pk, router_w, router_bias, w_gate, w_up, w_down, sh_gate, sh_up, sh_down, g, b):
    t, d = h.shape
    bm = EXPERT_BLOCK
    idx, gates, pos, cnt = _router(h, router_w, router_bias)
    counts = cnt[:, 0].astype(I32)
    padded = (counts + bm - 1) // bm * bm
    e_iota = jnp.arange(N_EXPERTS, dtype=I32)
    pad_end = jnp.sum(jnp.where(e_iota[None, :] <= e_iota[:, None], padded[None, :], 0), axis=1)
    pad_start = pad_end - padded
    n_blocks = t * TOP_K // bm + N_EXPERTS
    n_slots = n_blocks * bm
    blk_start = jnp.arange(n_blocks, dtype=I32) * bm
    blk_expert = jnp.minimum(
        jnp.sum((blk_start[:, None] >= pad_end[None, :]).astype(I32), axis=1), N_EXPERTS - 1)
    real_end = jnp.sum(jnp.where(e_iota[None, :] == blk_expert[:, None], (pad_start + counts)[None, :], 0), axis=1)
    blk_valid = jnp.clip(real_end - blk_start, 0, bm).astype(I32)
    n_active = (pad_end[-1:] // bm).astype(I32)

    slot4 = _slots(idx, pos, pad_start, n_slots)
    xs = _sc_scatter_rows(h_pk.reshape(PACK_SPLIT * t, LANES), slot4, PACK_SPLIT * n_slots)
    ys = _expert_matmul(xs.reshape(PACK_SPLIT, n_slots, LANES), blk_expert, n_active, blk_valid,
                        w_gate, w_up, w_down)
    yg = _sc_gather_rows(ys.reshape(PACK_SPLIT * n_slots, LANES), slot4)
    return _combine_ln(h, yg.reshape(PACK_SPLIT, TOP_K, t, LANES), gates.T, sh_gate, sh_up, sh_down, g, b)


def kernel(x, l0_w_in, l0_w_out, l0_na_rpb, l0_diff_lq1, l0_diff_lk1, l0_diff_lq2, l0_diff_lk2, l0_diff_subln_g, l0_ln1_g, l0_ln1_b, l0_router_w, l0_router_bias, l0_expert_w_gate, l0_expert_w_up, l0_expert_w_down, l0_shared_w_gate, l0_shared_w_up, l0_shared_w_down, l0_ln2_g, l0_ln2_b, l1_w_in, l1_w_out, l1_swa_sinks, l1_ln1_g, l1_ln1_b, l1_router_w, l1_router_bias, l1_expert_w_gate, l1_expert_w_up, l1_expert_w_down, l1_shared_w_gate, l1_shared_w_up, l1_shared_w_down, l1_ln2_g, l1_ln2_b):
    batch, seq, d = x.shape
    t = batch * seq
    x2 = x.reshape(t, d).astype(F32)
    tab = _rope_table(seq)
    qscale = HEAD_DIM ** -0.5 * LOG2E
    na_w = NA_HEADS * HEAD_DIM
    dq_w = DIFF_HEADS * 2 * HEAD_DIM

    col_scale = jnp.concatenate([
        jnp.full((na_w,), qscale, F32), jnp.ones((2 * na_w,), F32),
        jnp.full((dq_w,), qscale, F32), jnp.ones((2 * dq_w,), F32)])
    w_in0 = (l0_w_in.astype(F32) * col_scale).astype(BF16)
    per = LANES
    rope0 = [0] * (3 * na_w // per) + [1] * (2 * dq_w // per) + [0] * (dq_w // per)
    hcat0 = _inproj(x2, w_in0, tab, rope0, seq)
    oa = _na_attention(hcat0, l0_na_rpb, batch, seq)
    lambda_init = 0.8 - 0.6 * math.exp(-0.3 * 0)
    lam = (jnp.exp(jnp.sum(l0_diff_lq1.astype(F32) * l0_diff_lk1.astype(F32)))
           - jnp.exp(jnp.sum(l0_diff_lq2.astype(F32) * l0_diff_lk2.astype(F32))) + lambda_init)
    od = _diff_attention(hcat0, lam.reshape(1).astype(F32), l0_diff_subln_g, lambda_init, batch, seq)
    h, h_pk = _outproj_ln([oa, od], l0_w_out.astype(BF16), x2, l0_ln1_g, l0_ln1_b)
    x2 = _moe_layer(h, h_pk, l0_router_w, l0_router_bias, l0_expert_w_gate, l0_expert_w_up,
                    l0_expert_w_down, l0_shared_w_gate, l0_shared_w_up, l0_shared_w_down,
                    l0_ln2_g, l0_ln2_b)

    q_w = SWA_Q_HEADS * HEAD_DIM
    kv_w = SWA_KV_HEADS * HEAD_DIM
    w1 = l1_w_in.astype(F32)
    wq = w1[:, :q_w] * qscale
    wk = w1[:, q_w:q_w + kv_w].reshape(d, SWA_KV_HEADS, HEAD_DIM)
    wv = w1[:, q_w + kv_w:].reshape(d, SWA_KV_HEADS, HEAD_DIM)
    wkv = jnp.concatenate([wk, wv], axis=-1).reshape(d, 2 * kv_w)
    w_in1 = jnp.concatenate([wq, wkv], axis=1).astype(BF16)
    rope1 = [1] * (q_w // per) + [2] * (2 * kv_w // per)
    hcat1 = _inproj(x2, w_in1, tab, rope1, seq)
    o1 = _swa_attention(hcat1, l1_swa_sinks, batch, seq)
    h, h_pk = _outproj_ln([o1], l1_w_out.astype(BF16), x2, l1_ln1_g, l1_ln1_b)
    x2 = _moe_layer(h, h_pk, l1_router_w, l1_router_bias, l1_expert_w_gate, l1_expert_w_up,
                    l1_expert_w_down, l1_shared_w_gate, l1_shared_w_up, l1_shared_w_down,
                    l1_ln2_g, l1_ln2_b)
    return x2.reshape(batch, seq, d).astype(x.dtype)
```

```python
import functools
import math

import numpy as np
import jax
import jax.numpy as jnp
from jax import lax
from jax.experimental import pallas as pl
from jax.experimental.pallas import tpu as pltpu
from jax.experimental.pallas import tpu_sc as plsc

F32 = jnp.float32
BF16 = jnp.bfloat16
I32 = jnp.int32

HEAD_DIM = 64
GRID_W = 64
NA_HEADS = 8
NA_WIN_ROWS = 8
NA_WIN_COLS = 16
DIFF_HEADS = 4
SWA_Q_HEADS = 16
SWA_KV_HEADS = 4
SWA_WINDOW = 128
SWA_BLOCK = 128
ROPE_THETA = 10000.0
N_EXPERTS = 256
TOP_K = 8
N_GROUPS = 8
TOPK_GROUPS = 4
ROUTED_SCALE = 2.5
LN_EPS = 1e-5
DEPTH = 2
DN_ALPHA = (2 * DEPTH) ** 0.25

LOG2E = 1.4426950408889634
NEG_BIG = -1e30
LANES = 128
NA_KEY_ROWS = NA_WIN_ROWS + 1
EXPERT_BLOCK = 256
DIFF_UNROLL = 4
DIFF_ONES_ROWS = 16
PACK_SPLIT = 4
SC_CORES = 2
SC_SUBCORES = 16
SC_WINDOW = 128
VMEM_LIMIT = 56 * 1024 * 1024


def _cparams(sem):
    return pltpu.CompilerParams(dimension_semantics=sem, vmem_limit_bytes=VMEM_LIMIT)


def _silu(x):
    return x / (1.0 + jnp.exp(-x))


def _pack_rows(y):
    half = y.shape[1] // 2
    bits = pltpu.bitcast(y.astype(BF16).astype(F32), I32)
    lo = lax.shift_right_logical(bits[:, :half], 16)
    hi = bits[:, half:] & jnp.int32(-65536)
    return hi | lo


def _unpack_words(w):
    lo = pltpu.bitcast(lax.shift_left(w, 16), F32)
    hi = pltpu.bitcast(w & jnp.int32(-65536), F32)
    return lo, hi


def _unpack_rows(ref_at):
    los, his = [], []
    for c in range(PACK_SPLIT):
        lo, hi = _unpack_words(ref_at(c))
        los.append(lo)
        his.append(hi)
    return jnp.concatenate(los + his, axis=1)


def _layer_norm(y, g, b):
    mu = jnp.mean(y, axis=-1, keepdims=True)
    yc = y - mu
    var = jnp.mean(yc * yc, axis=-1, keepdims=True)
    return yc * lax.rsqrt(var + LN_EPS) * g + b


def _inproj_body(x_ref, w_ref, tab_ref, o_ref, *, chunk, rope_kind):
    tm = x_ref.shape[0]
    n_out = w_ref.shape[1]
    x = x_ref[...].astype(BF16)
    lane = lax.broadcasted_iota(I32, (tm, LANES), 1)
    first_half = (lane % HEAD_DIM) < (HEAD_DIM // 2)
    per = chunk // LANES
    for c in range(n_out // chunk):
        h = jnp.dot(x, w_ref[:, c * chunk:(c + 1) * chunk], preferred_element_type=F32)
        kinds = rope_kind[c * per:(c + 1) * per]
        if any(kinds):
            pieces = []
            for j, kind in enumerate(kinds):
                t = h[:, j * LANES:(j + 1) * LANES]
                if kind:
                    base = (kind - 1) * 2 * LANES
                    cos = tab_ref[:, base:base + LANES]
                    sin = tab_ref[:, base + LANES:base + 2 * LANES]
                    rot = jnp.where(first_half, pltpu.roll(t, LANES - HEAD_DIM // 2, 1),
                                    pltpu.roll(t, HEAD_DIM // 2, 1))
                    t = t * cos + rot * sin
                pieces.append(t)
            h = jnp.concatenate(pieces, axis=1)
        o_ref[:, c * chunk:(c + 1) * chunk] = h.astype(o_ref.dtype)


def _inproj(x2, w_bf, tab, rope_kind, seq, tm=512, chunk=256):
    t, d = x2.shape
    n_out = w_bf.shape[1]
    tm = min(tm, seq)
    sb = seq // tm
    return pl.pallas_call(
        functools.partial(_inproj_body, chunk=chunk, rope_kind=tuple(rope_kind)),
        grid=(t // tm,),
        in_specs=[pl.BlockSpec((tm, d), lambda i: (i, 0)),
                  pl.BlockSpec((d, n_out), lambda i: (0, 0)),
                  pl.BlockSpec((tm, tab.shape[1]), lambda i: (i % sb, 0))],
        out_specs=pl.BlockSpec((tm, n_out), lambda i: (i, 0)),
        out_shape=jax.ShapeDtypeStruct((t, n_out), BF16),
        compiler_params=_cparams(("parallel",)),
        name="inproj",
    )(x2, w_bf, tab)


def _rope_table(seq):
    half = HEAD_DIM // 2
    inv = 1.0 / (ROPE_THETA ** (jnp.arange(0, HEAD_DIM, 2, dtype=F32) / HEAD_DIM))
    ang = jnp.arange(seq, dtype=F32)[:, None] * inv[None, :]
    cos = jnp.cos(ang)
    sin = jnp.sin(ang)
    cos64 = jnp.concatenate([cos, cos], axis=-1)
    sin64 = jnp.concatenate([-sin, sin], axis=-1)
    one = jnp.ones((seq, HEAD_DIM), F32)
    zero = jnp.zeros((seq, HEAD_DIM), F32)
    del half
    return jnp.concatenate([cos64, cos64, sin64, sin64, cos64, one, sin64, zero], axis=-1)


def _na_body(pat_ref, q_ref, k_ref, v_ref, bias_ref, o_ref, *, rows_n):
    del pat_ref
    r = pl.program_id(2)
    nq = q_ref.shape[0]
    nk = NA_KEY_ROWS * GRID_W
    ks = jnp.clip(2 * r - NA_WIN_ROWS // 2, 0, rows_n - NA_KEY_ROWS)
    start = pl.multiple_of(ks * GRID_W, GRID_W)
    kwin = k_ref[pl.ds(start, nk), :]
    vwin = v_ref[pl.ds(start, nk), :]
    q = q_ref[...].astype(F32)
    lane = lax.broadcasted_iota(I32, (nq, LANES), 1)
    outs = []
    for hh in range(2):
        in_head = (lane < HEAD_DIM) if hh == 0 else (lane >= HEAD_DIM)
        qm = jnp.where(in_head, q, 0.0).astype(BF16)
        s = lax.dot_general(qm, kwin, (((1,), (1,)), ((), ())), preferred_element_type=F32)
        s = s + bias_ref[0, hh]
        m = jnp.max(s, axis=-1, keepdims=True)
        p = jnp.exp2(s - m)
        l = jnp.sum(p, axis=-1, keepdims=True)
        pv = jnp.dot(p.astype(BF16), vwin, preferred_element_type=F32)
        outs.append(pv / l)
    o_ref[...] = jnp.where(lane < HEAD_DIM, outs[0], outs[1]).astype(o_ref.dtype)


def _na_bias(rpb, rows_n):
    half = NA_WIN_ROWS // 2
    pats, pat_id = [], []
    for blk in range(rows_n // 2):
        r0 = 2 * blk
        ks = min(max(r0 - half, 0), rows_n - NA_KEY_ROWS)
        starts = tuple(min(max(r0 + i - half, 0), rows_n - NA_WIN_ROWS) - ks for i in range(2))
        key = (r0 - ks, starts)
        if key not in pats:
            pats.append(key)
        pat_id.append(pats.index(key))
    wq = np.arange(GRID_W)
    wk = np.arange(GRID_W)
    col_start = np.clip(wq - NA_WIN_COLS // 2, 0, GRID_W - NA_WIN_COLS)
    col_off = wk[None, :] - col_start[:, None]
    col_valid = (col_off >= 0) & (col_off < NA_WIN_COLS)
    dc = np.clip(wk[None, :] - wq[:, None], -(NA_WIN_COLS - 1), NA_WIN_COLS - 1) + (NA_WIN_COLS - 1)
    n_dc = 2 * NA_WIN_COLS - 1
    onehot = (dc.reshape(-1)[None, :] == np.arange(n_dc)[:, None]).astype(np.float32)
    heads, n_dr = rpb.shape[0], rpb.shape[1]
    col = jnp.dot(rpb.astype(F32).reshape(heads * n_dr, n_dc), jnp.asarray(onehot),
                  precision=lax.Precision.HIGHEST).reshape(heads, n_dr, GRID_W, GRID_W)
    col = jnp.where(col_valid[None, None], col * LOG2E, NEG_BIG)
    masked_blk = jnp.full((heads, GRID_W, GRID_W), NEG_BIG, F32)
    tables = []
    for r0rel, starts in pats:
        qrows = []
        for qi in range(2):
            blks = []
            for kr in range(NA_KEY_ROWS):
                row_ok = starts[qi] <= kr < starts[qi] + NA_WIN_ROWS
                dr = kr - (r0rel + qi) + (NA_WIN_ROWS - 1)
                blks.append(col[:, dr] if row_ok else masked_blk)
            qrows.append(jnp.concatenate(blks, axis=-1))
        tables.append(jnp.concatenate(qrows, axis=1))
    return jnp.stack(tables, axis=0), np.asarray(pat_id, np.int32)


def _na_attention(hcat, rpb, batch, seq):
    t = hcat.shape[0]
    rows_n = seq // GRID_W
    nblk = rows_n // 2
    nq = 2 * GRID_W
    bias, pat_id = _na_bias(rpb, rows_n)
    nk = NA_KEY_ROWS * GRID_W
    hp = NA_HEADS // 2
    grid_spec = pltpu.PrefetchScalarGridSpec(
        num_scalar_prefetch=1,
        grid=(batch, hp, nblk),
        in_specs=[
            pl.BlockSpec((nq, LANES), lambda b, h, r, pat: (b * nblk + r, h)),
            pl.BlockSpec((seq, LANES), lambda b, h, r, pat: (b, hp + h)),
            pl.BlockSpec((seq, LANES), lambda b, h, r, pat: (b, 2 * hp + h)),
            pl.BlockSpec((1, 2, nq, nk), lambda b, h, r, pat: (pat[r], h, 0, 0)),
        ],
        out_specs=pl.BlockSpec((nq, LANES), lambda b, h, r, pat: (b * nblk + r, h)),
    )
    return pl.pallas_call(
        functools.partial(_na_body, rows_n=rows_n),
        grid_spec=grid_spec,
        out_shape=jax.ShapeDtypeStruct((t, NA_HEADS * HEAD_DIM), BF16),
        compiler_params=_cparams(("parallel", "parallel", "arbitrary")),
        name="na_attn",
    )(jnp.asarray(pat_id), hcat, hcat, hcat, bias)


def _diff_body(lam_ref, q_ref, k_ref, v_ref, g_ref, o_ref, vt_ref, acc_ref, sta_ref, stb_ref, *, tk, lambda_init):
    j = pl.program_id(2)
    tq = q_ref.shape[0]
    seq = k_ref.shape[0]
    nkv = seq // tk

    @pl.when(j == 0)
    def _():
        ones = jnp.ones((DIFF_ONES_ROWS, tk), BF16)
        for c in range(nkv):
            vt_ref[c, :LANES, :] = v_ref[c * tk:(c + 1) * tk, :].astype(F32).T.astype(BF16)
            vt_ref[c, LANES:, :] = ones

    q = q_ref[...].astype(F32)
    lane = lax.broadcasted_iota(I32, (tq, LANES), 1)
    q1 = jnp.where(lane < HEAD_DIM, q, 0.0).T
    q2 = jnp.where(lane >= HEAD_DIM, q, 0.0).T
    rhs = jnp.concatenate([q1, q2], axis=1).astype(BF16)

    def scores(i, dst_ref):
        start = pl.multiple_of(jnp.minimum(i, nkv - 1) * tk, tk)
        dst_ref[...] = jnp.dot(k_ref[pl.ds(start, tk), :], rhs, preferred_element_type=F32)

    def consume(i, src_ref, m):
        st = src_ref[...]
        m_new = jnp.maximum(m, jnp.max(st, axis=0, keepdims=True))
        p = jnp.exp2((st - m_new).astype(BF16))
        pv = jnp.dot(vt_ref[i], p, preferred_element_type=F32)
        acc_ref[...] = jnp.exp2(m - m_new) * acc_ref[...] + pv
        return m_new

    acc_ref[...] = jnp.zeros_like(acc_ref)
    scores(0, sta_ref)

    unroll = min(DIFF_UNROLL, nkv)

    def group(ii, m):
        bufs = (sta_ref, stb_ref)
        for u in range(unroll):
            i = unroll * ii + u
            scores(i + 1, bufs[(u + 1) % 2])
            m = consume(i, bufs[u % 2], m)
        return m

    lax.fori_loop(0, nkv // unroll, group, jnp.full((1, 2 * tq), NEG_BIG, F32))
    acc = acc_ref[...]
    ot = acc[:LANES, :] / acc[LANES:LANES + 1, :]
    dt = ot[:, :tq] - lam_ref[0] * ot[:, tq:]
    ms = jnp.mean(dt * dt, axis=0, keepdims=True)
    y = dt * lax.rsqrt(ms + LN_EPS) * g_ref[...] * (1.0 - lambda_init)
    o_ref[...] = y.T.astype(o_ref.dtype)


def _diff_attention(hcat, lam, subln_g, lambda_init, batch, seq, tq=256, tk=512):
    t = hcat.shape[0]
    tq = min(tq, seq)
    tk = min(tk, seq)
    nq = seq // tq
    qoff = 3 * NA_HEADS * HEAD_DIM // LANES
    koff = qoff + DIFF_HEADS
    voff = koff + DIFF_HEADS
    grid_spec = pltpu.PrefetchScalarGridSpec(
        num_scalar_prefetch=1,
        grid=(batch, DIFF_HEADS, nq),
        in_specs=[
            pl.BlockSpec((tq, LANES), lambda b, h, j, lam: (b * nq + j, qoff + h)),
            pl.BlockSpec((seq, LANES), lambda b, h, j, lam: (b, koff + h)),
            pl.BlockSpec((seq, LANES), lambda b, h, j, lam: (b, voff + h)),
            pl.BlockSpec((LANES, 1), lambda b, h, j, lam: (0, 0)),
        ],
        out_specs=pl.BlockSpec((tq, LANES), lambda b, h, j, lam: (b * nq + j, h)),
        scratch_shapes=[pltpu.VMEM((seq // tk, LANES + DIFF_ONES_ROWS, tk), BF16),
                        pltpu.VMEM((LANES + DIFF_ONES_ROWS, 2 * tq), F32),
                        pltpu.VMEM((tk, 2 * tq), F32),
                        pltpu.VMEM((tk, 2 * tq), F32)],
    )
    return pl.pallas_call(
        functools.partial(_diff_body, tk=tk, lambda_init=lambda_init),
        grid_spec=grid_spec,
        out_shape=jax.ShapeDtypeStruct((t, DIFF_HEADS * 2 * HEAD_DIM), BF16),
        compiler_params=_cparams(("parallel", "parallel", "arbitrary")),
        name="diff_attn",
    )(lam, hcat, hcat, hcat, subln_g.astype(F32).reshape(LANES, 1))


def _swa_body(sink_ref, q_ref, kv_ref, o_ref):
    kvh = pl.program_id(1)
    n = pl.program_id(2)
    nq = q_ref.shape[0]
    seq = kv_ref.shape[0]
    nk = min(3 * SWA_BLOCK, seq)
    start = pl.multiple_of(jnp.clip((n - 1) * SWA_BLOCK, 0, seq - nk), SWA_BLOCK)
    kv = kv_ref[pl.ds(start, nk), :]
    kv_sw = pltpu.roll(kv.astype(F32), HEAD_DIM, 1).astype(BF16)
    qpos = n * SWA_BLOCK + lax.broadcasted_iota(I32, (nq, nk), 0)
    kpos = start + lax.broadcasted_iota(I32, (nq, nk), 1)
    valid = jnp.abs(kpos - qpos) <= SWA_WINDOW
    lane = lax.broadcasted_iota(I32, (nq, LANES), 1)
    low = lane < HEAD_DIM
    group = SWA_Q_HEADS // SWA_KV_HEADS
    for c in range(group // 2):
        q = q_ref[:, c * LANES:(c + 1) * LANES].astype(F32)
        res = []
        for odd in range(2):
            g = 2 * c + odd
            qm = jnp.where(lane >= HEAD_DIM if odd else low, q, 0.0).astype(BF16)
            kmat = kv_sw if odd else kv
            vmat = kv if odd else kv_sw
            s = lax.dot_general(qm, kmat, (((1,), (1,)), ((), ())), preferred_element_type=F32)
            s = jnp.where(valid, s, NEG_BIG)
            sink = sink_ref[kvh * group + g]
            m = jnp.maximum(jnp.max(s, axis=-1, keepdims=True), sink)
            e = jnp.exp2(s - m)
            denom = jnp.sum(e, axis=-1, keepdims=True) + jnp.exp2(sink - m)
            pv = jnp.dot(e.astype(BF16), vmat, preferred_element_type=F32)
            res.append(pv / denom)
        o_ref[:, c * LANES:(c + 1) * LANES] = jnp.where(low, res[0], res[1]).astype(o_ref.dtype)


def _swa_attention(hcat, sinks, batch, seq):
    t = hcat.shape[0]
    nb = seq // SWA_BLOCK
    group = SWA_Q_HEADS // SWA_KV_HEADS
    qw = group * HEAD_DIM
    kvoff = SWA_Q_HEADS * HEAD_DIM // LANES
    grid_spec = pltpu.PrefetchScalarGridSpec(
        num_scalar_prefetch=1,
        grid=(batch, SWA_KV_HEADS, nb),
        in_specs=[
            pl.BlockSpec((SWA_BLOCK, qw), lambda b, h, n, s: (b * nb + n, h)),
            pl.BlockSpec((seq, LANES), lambda b, h, n, s: (b, kvoff + h)),
        ],
        out_specs=pl.BlockSpec((SWA_BLOCK, qw), lambda b, h, n, s: (b * nb + n, h)),
    )
    return pl.pallas_call(
        _swa_body,
        grid_spec=grid_spec,
        out_shape=jax.ShapeDtypeStruct((t, SWA_Q_HEADS * HEAD_DIM), BF16),
        compiler_params=_cparams(("parallel", "parallel", "arbitrary")),
        name="swa_attn",
    )(sinks.astype(F32) * LOG2E, hcat, hcat)


def _outproj_body(*refs, n_in):
    a_refs = refs[:n_in]
    w_ref, x_ref, g_ref, b_ref, o_ref, opk_ref = refs[n_in:]
    acc = None
    off = 0
    for a_ref in a_refs:
        ka = a_ref.shape[1]
        d = jnp.dot(a_ref[...], w_ref[off:off + ka, :], preferred_element_type=F32)
        acc = d if acc is None else acc + d
        off += ka
    y = DN_ALPHA * x_ref[...] + acc
    out = _layer_norm(y, g_ref[...], b_ref[...])
    o_ref[...] = out
    packed = _pack_rows(out)
    for c in range(PACK_SPLIT):
        opk_ref[c] = packed[:, c * LANES:(c + 1) * LANES]


def _outproj_ln(acts, w_bf, x2, g, b, tm=512):
    t, d = x2.shape
    tm = min(tm, t)
    in_specs = [pl.BlockSpec((tm, a.shape[1]), lambda i: (i, 0)) for a in acts]
    in_specs += [pl.BlockSpec(w_bf.shape, lambda i: (0, 0)),
                 pl.BlockSpec((tm, d), lambda i: (i, 0)),
                 pl.BlockSpec((1, d), lambda i: (0, 0)),
                 pl.BlockSpec((1, d), lambda i: (0, 0))]
    return pl.pallas_call(
        functools.partial(_outproj_body, n_in=len(acts)),
        grid=(t // tm,),
        in_specs=in_specs,
        out_specs=[pl.BlockSpec((tm, d), lambda i: (i, 0)),
                   pl.BlockSpec((PACK_SPLIT, tm, LANES), lambda i: (0, i, 0))],
        out_shape=[jax.ShapeDtypeStruct((t, d), F32), jax.ShapeDtypeStruct((PACK_SPLIT, t, LANES), I32)],
        compiler_params=_cparams(("parallel",)),
        name="outproj_ln",
    )(*acts, w_bf, x2, g.astype(F32).reshape(1, d), b.astype(F32).reshape(1, d))


def _router_body(h_ref, whi_ref, wlo_ref, bias_ref, idx_ref, gate_ref, pos_ref, cnt_ref, base_ref):
    i = pl.program_id(0)
    tm = h_ref.shape[0]
    gsz = N_EXPERTS // N_GROUPS

    @pl.when(i == 0)
    def _():
        base_ref[...] = jnp.zeros_like(base_ref)

    h = h_ref[...]
    h_hi = h.astype(BF16)
    h_lo = (h - h_hi.astype(F32)).astype(BF16)
    dn = (((1,), (1,)), ((), ()))
    whi = whi_ref[...]
    logits = (lax.dot_general(whi, h_hi, dn, preferred_element_type=F32)
              + lax.dot_general(whi, h_lo, dn, preferred_element_type=F32)
              + lax.dot_general(wlo_ref[...], h_hi, dn, preferred_element_type=F32))
    scores = 1.0 / (1.0 + jnp.exp(-logits))
    choice = scores + bias_ref[...]

    iota_g = lax.broadcasted_iota(I32, (gsz, tm), 0)
    gscore = []
    for g in range(N_GROUPS):
        cg = choice[g * gsz:(g + 1) * gsz, :]
        m1 = jnp.max(cg, axis=0, keepdims=True)
        first = jnp.min(jnp.where(cg == m1, iota_g, gsz), axis=0, keepdims=True)
        m2 = jnp.max(jnp.where(iota_g == first, -jnp.inf, cg), axis=0, keepdims=True)
        gscore.append(m1 + m2)
    pieces = []
    for g in range(N_GROUPS):
        rank = jnp.zeros((1, tm), I32)
        for o in range(N_GROUPS):
            if o == g:
                continue
            beats = (gscore[o] > gscore[g]) if o > g else (gscore[o] >= gscore[g])
            rank = rank + beats.astype(I32)
        keep = rank < TOPK_GROUPS
        pieces.append(jnp.where(keep, choice[g * gsz:(g + 1) * gsz, :], -jnp.inf))
    masked = jnp.concatenate(pieces, axis=0)

    iota_e = lax.broadcasted_iota(I32, (N_EXPERTS, tm), 0)
    sel_all = jnp.zeros((N_EXPERTS, tm), F32)
    idxs, gates, sels = [], [], []
    for _ in range(TOP_K):
        mx = jnp.max(masked, axis=0, keepdims=True)
        idx = jnp.min(jnp.where(masked == mx, iota_e, N_EXPERTS), axis=0, keepdims=True)
        sel = iota_e == idx
        gates.append(jnp.sum(jnp.where(sel, scores, 0.0), axis=0, keepdims=True))
        masked = jnp.where(sel, -jnp.inf, masked)
        sel_all = sel_all + sel.astype(F32)
        idxs.append(idx)
        sels.append(sel)
    gsum = gates[0]
    for gk in gates[1:]:
        gsum = gsum + gk
    gate_ref[...] = jnp.concatenate(gates, axis=0) / gsum * ROUTED_SCALE
    idx_ref[...] = jnp.concatenate(idxs, axis=0)

    tri = (lax.broadcasted_iota(I32, (tm, tm), 0) < lax.broadcasted_iota(I32, (tm, tm), 1))
    cum = jnp.dot(sel_all.astype(BF16), tri.astype(F32).astype(BF16), preferred_element_type=F32)
    tot = cum + base_ref[...]
    pos = [jnp.sum(jnp.where(sel, tot, 0.0), axis=0, keepdims=True) for sel in sels]
    pos_ref[...] = jnp.concatenate(pos, axis=0).astype(I32)
    base_ref[...] = base_ref[...] + jnp.sum(sel_all, axis=1, keepdims=True)
    cnt_ref[...] = base_ref[...]


def _router(h, router_w, router_bias, tm=512):
    t, d = h.shape
    tm = min(tm, t)
    wt = router_w.astype(F32).T
    whi = wt.astype(BF16)
    wlo = (wt - whi.astype(F32)).astype(BF16)
    return pl.pallas_call(
        _router_body,
        grid=(t // tm,),
        in_specs=[pl.BlockSpec((tm, d), lambda i: (i, 0)),
                  pl.BlockSpec((N_EXPERTS, d), lambda i: (0, 0)),
                  pl.BlockSpec((N_EXPERTS, d), lambda i: (0, 0)),
                  pl.BlockSpec((N_EXPERTS, 1), lambda i: (0, 0))],
        out_specs=[pl.BlockSpec((TOP_K, tm), lambda i: (0, i)),
                   pl.BlockSpec((TOP_K, tm), lambda i: (0, i)),
                   pl.BlockSpec((TOP_K, tm), lambda i: (0, i)),
                   pl.BlockSpec((N_EXPERTS, 1), lambda i: (0, 0))],
        out_shape=[jax.ShapeDtypeStruct((TOP_K, t), I32),
                   jax.ShapeDtypeStruct((TOP_K, t), F32),
                   jax.ShapeDtypeStruct((TOP_K, t), I32),
                   jax.ShapeDtypeStruct((N_EXPERTS, 1), F32)],
        scratch_shapes=[pltpu.VMEM((N_EXPERTS, 1), F32)],
        compiler_params=_cparams(("arbitrary",)),
        name="router",
    )(h, whi, wlo, router_bias.astype(F32).reshape(N_EXPERTS, 1))


def _expert_body(be_ref, na_ref, bv_ref, x_ref, wg_ref, wu_ref, wd_ref, y_ref, wgb, wub, wdb):
    i = pl.program_id(0)
    bm = x_ref.shape[1]

    @pl.when(i < na_ref[0])
    def _():
        prev = be_ref[jnp.maximum(i - 1, 0)]

        @pl.when((i == 0) | (be_ref[i] != prev))
        def _():
            wgb[...] = wg_ref[0].astype(BF16)
            wub[...] = wu_ref[0].astype(BF16)
            wdb[...] = wd_ref[0].astype(BF16)

        keep = lax.broadcasted_iota(I32, (bm, LANES), 0) < bv_ref[i]
        x = _unpack_rows(lambda c: jnp.where(keep, x_ref[c], 0)).astype(BF16)
        g = jnp.dot(x, wgb[...], preferred_element_type=F32)
        u = jnp.dot(x, wub[...], preferred_element_type=F32)
        hmid = (_silu(g) * u).astype(BF16)
        packed = _pack_rows(jnp.dot(hmid, wdb[...], preferred_element_type=F32))
        for c in range(PACK_SPLIT):
            y_ref[c] = packed[:, c * LANES:(c + 1) * LANES]


def _expert_matmul(xs, blk_expert, n_active, blk_valid, w_gate, w_up, w_down):
    n_slots = xs.shape[1]
    bm = EXPERT_BLOCK
    n_blocks = n_slots // bm
    d, ff = w_gate.shape[1], w_gate.shape[2]

    def row_map(i, be, na, bv):
        return (0, jnp.minimum(i, na[0] - 1), 0)

    def w_map(i, be, na, bv):
        return (be[jnp.minimum(i, na[0] - 1)], 0, 0)

    grid_spec = pltpu.PrefetchScalarGridSpec(
        num_scalar_prefetch=3,
        grid=(n_blocks,),
        in_specs=[pl.BlockSpec((PACK_SPLIT, bm, LANES), row_map),
                  pl.BlockSpec((1, d, ff), w_map),
                  pl.BlockSpec((1, d, ff), w_map),
                  pl.BlockSpec((1, ff, d), w_map)],
        out_specs=pl.BlockSpec((PACK_SPLIT, bm, LANES), row_map),
        scratch_shapes=[pltpu.VMEM((d, ff), BF16), pltpu.VMEM((d, ff), BF16), pltpu.VMEM((ff, d), BF16)],
    )
    return pl.pallas_call(
        _expert_body,
        grid_spec=grid_spec,
        out_shape=jax.ShapeDtypeStruct((PACK_SPLIT, n_slots, LANES), I32),
        compiler_params=_cparams(("arbitrary",)),
        name="expert_mlp",
    )(blk_expert, n_active, blk_valid, xs, w_gate, w_up, w_down)


def _slot_body(idx_ref, pos_ref, ps_ref, slot_ref, *, n_slots):
    tm = idx_ref.shape[1]
    iota_e = lax.broadcasted_iota(I32, (N_EXPERTS, tm), 0)
    ps = ps_ref[...]
    rows = []
    for k in range(TOP_K):
        start = jnp.sum(jnp.where(iota_e == idx_ref[k:k + 1, :], ps, 0.0), axis=0, keepdims=True)
        rows.append(start.astype(I32) + pos_ref[k:k + 1, :])
    slot = jnp.concatenate(rows, axis=0)
    for c in range(PACK_SPLIT):
        slot_ref[c * TOP_K:(c + 1) * TOP_K, :] = slot + c * n_slots


def _slots(idx, pos, pad_start, n_slots, tm=512):
    t = idx.shape[1]
    tm = min(tm, t)
    return pl.pallas_call(
        functools.partial(_slot_body, n_slots=n_slots),
        grid=(t // tm,),
        in_specs=[pl.BlockSpec((TOP_K, tm), lambda i: (0, i)),
                  pl.BlockSpec((TOP_K, tm), lambda i: (0, i)),
                  pl.BlockSpec((N_EXPERTS, 1), lambda i: (0, 0))],
        out_specs=pl.BlockSpec((PACK_SPLIT * TOP_K, tm), lambda i: (0, i)),
        out_shape=jax.ShapeDtypeStruct((PACK_SPLIT * TOP_K, t), I32),
        compiler_params=_cparams(("parallel",)),
        name="slots",
    )(idx, pos, pad_start.astype(F32).reshape(N_EXPERTS, 1))


def _sc_mesh():
    return plsc.VectorSubcoreMesh(core_axis_name="core", subcore_axis_name="subcore",
                                  num_cores=SC_CORES, num_subcores=SC_SUBCORES)


def _sc_scatter_rows(rows, idx, n_out):
    t = idx.shape[1]
    nj = t // SC_WINDOW

    @functools.partial(pl.kernel, out_type=jax.ShapeDtypeStruct((n_out, LANES), I32),
                       mesh=_sc_mesh(), scratch_types=[], name="sc_dispatch")
    def run(rows_hbm, idx_hbm, out_hbm):
        def body(rows_vmem, idx_vmem):
            for k in range(TOP_K):
                pltpu.sync_copy(rows_vmem, out_hbm.at[idx_vmem.at[k]])

        pltpu.emit_pipeline(
            body,
            grid=(rows.shape[0] // SC_WINDOW,),
            in_specs=[pl.BlockSpec((SC_WINDOW, LANES), lambda s: (s, 0)),
                      pl.BlockSpec((TOP_K, SC_WINDOW), lambda s: (s // nj, s % nj))],
            out_specs=[],
            core_axis_name=("core", "subcore"),
            dimension_semantics=(pltpu.PARALLEL,),
        )(rows_hbm, idx_hbm)

    return run(rows, idx)


def _sc_gather_rows(src, idx):
    nr, t = idx.shape
    nj = t // SC_WINDOW

    @functools.partial(pl.kernel, out_type=jax.ShapeDtypeStruct((nr * t, LANES), I32),
                       mesh=_sc_mesh(), scratch_types=[], name="sc_combine")
    def run(src_hbm, idx_hbm, out_hbm):
        def body(idx_vmem, out_vmem):
            pltpu.sync_copy(src_hbm.at[idx_vmem.at[0]], out_vmem)

        pltpu.emit_pipeline(
            body,
            grid=(nr * nj,),
            in_specs=[pl.BlockSpec((1, SC_WINDOW), lambda s: (s // nj, s % nj))],
            out_specs=[pl.BlockSpec((SC_WINDOW, LANES), lambda s: (s, 0))],
            core_axis_name=("core", "subcore"),
            dimension_semantics=(pltpu.PARALLEL,),
        )(idx_hbm, out_hbm)

    return run(src, idx)


def _combine_body(h_ref, yg_ref, gt_ref, sg_ref, su_ref, sd_ref, g_ref, b_ref, o_ref):
    h = h_ref[...]
    hb = h.astype(BF16)
    a = jnp.dot(hb, sg_ref[...], preferred_element_type=F32)
    u = jnp.dot(hb, su_ref[...], preferred_element_type=F32)
    moe = jnp.dot((_silu(a) * u).astype(BF16), sd_ref[...], preferred_element_type=F32)
    gt = gt_ref[...]
    for k in range(TOP_K):
        moe = moe + gt[:, k:k + 1] * _unpack_rows(lambda c, k=k: yg_ref[c, k])
    o_ref[...] = _layer_norm(DN_ALPHA * h + moe, g_ref[...], b_ref[...])


def _combine_ln(h, yg, gates_t, sh_gate, sh_up, sh_down, g, b, tm=256):
    t, d = h.shape
    tm = min(tm, t)
    ff = sh_gate.shape[1]
    return pl.pallas_call(
        _combine_body,
        grid=(t // tm,),
        in_specs=[pl.BlockSpec((tm, d), lambda i: (i, 0)),
                  pl.BlockSpec((PACK_SPLIT, TOP_K, tm, LANES), lambda i: (0, 0, i, 0)),
                  pl.BlockSpec((tm, TOP_K), lambda i: (i, 0)),
                  pl.BlockSpec((d, ff), lambda i: (0, 0)),
                  pl.BlockSpec((d, ff), lambda i: (0, 0)),
                  pl.BlockSpec((ff, d), lambda i: (0, 0)),
                  pl.BlockSpec((1, d), lambda i: (0, 0)),
                  pl.BlockSpec((1, d), lambda i: (0, 0))],
        out_specs=pl.BlockSpec((tm, d), lambda i: (i, 0)),
        out_shape=jax.ShapeDtypeStruct((t, d), F32),
        compiler_params=_cparams(("parallel",)),
        name="combine_ln",
    )(h, yg, gates_t, sh_gate.astype(BF16), sh_up.astype(BF16), sh_down.astype(BF16),
      g.astype(F32).reshape(1, d), b.astype(F32).reshape(1, d))


def _moe_layer(h, h_pk, router_w, router_bias, w_gate, w_up, w_down, sh_gate, sh_up, sh_down, g, b):
    t, d = h.shape
    bm = EXPERT_BLOCK
    idx, gates, pos, cnt = _router(h, router_w, router_bias)
    counts = cnt[:, 0].astype(I32)
    padded = (counts + bm - 1) // bm * bm
    e_iota = jnp.arange(N_EXPERTS, dtype=I32)
    pad_end = jnp.sum(jnp.where(e_iota[None, :] <= e_iota[:, None], padded[None, :], 0), axis=1)
    pad_start = pad_end - padded
    n_blocks = t * TOP_K // bm + N_EXPERTS
    n_slots = n_blocks * bm
    blk_start = jnp.arange(n_blocks, dtype=I32) * bm
    blk_expert = jnp.minimum(
        jnp.sum((blk_start[:, None] >= pad_end[None, :]).astype(I32), axis=1), N_EXPERTS - 1)
    real_end = jnp.sum(jnp.where(e_iota[None, :] == blk_expert[:, None], (pad_start + counts)[None, :], 0), axis=1)
    blk_valid = jnp.clip(real_end - blk_start, 0, bm).astype(I32)
    n_active = (pad_end[-1:] // bm).astype(I32)

    slot4 = _slots(idx, pos, pad_start, n_slots)
    xs = _sc_scatter_rows(h_pk.reshape(PACK_SPLIT * t, LANES), slot4, PACK_SPLIT * n_slots)
    ys = _expert_matmul(xs.reshape(PACK_SPLIT, n_slots, LANES), blk_expert, n_active, blk_valid,
                        w_gate, w_up, w_down)
    yg = _sc_gather_rows(ys.reshape(PACK_SPLIT * n_slots, LANES), slot4)
    return _combine_ln(h, yg.reshape(PACK_SPLIT, TOP_K, t, LANES), gates.T, sh_gate, sh_up, sh_down, g, b)


def kernel(x, l0_w_in, l0_w_out, l0_na_rpb, l0_diff_lq1, l0_diff_lk1, l0_diff_lq2, l0_diff_lk2, l0_diff_subln_g, l0_ln1_g, l0_ln1_b, l0_router_w, l0_router_bias, l0_expert_w_gate, l0_expert_w_up, l0_expert_w_down, l0_shared_w_gate, l0_shared_w_up, l0_shared_w_down, l0_ln2_g, l0_ln2_b, l1_w_in, l1_w_out, l1_swa_sinks, l1_ln1_g, l1_ln1_b, l1_router_w, l1_router_bias, l1_expert_w_gate, l1_expert_w_up, l1_expert_w_down, l1_shared_w_gate, l1_shared_w_up, l1_shared_w_down, l1_ln2_g, l1_ln2_b):
    batch, seq, d = x.shape
    t = batch * seq
    x2 = x.reshape(t, d).astype(F32)
    tab = _rope_table(seq)
    qscale = HEAD_DIM ** -0.5 * LOG2E
    na_w = NA_HEADS * HEAD_DIM
    dq_w = DIFF_HEADS * 2 * HEAD_DIM

    col_scale = jnp.concatenate([
        jnp.full((na_w,), qscale, F32), jnp.ones((2 * na_w,), F32),
        jnp.full((dq_w,), qscale, F32), jnp.ones((2 * dq_w,), F32)])
    w_in0 = (l0_w_in.astype(F32) * col_scale).astype(BF16)
    per = LANES
    rope0 = [0] * (3 * na_w // per) + [1] * (2 * dq_w // per) + [0] * (dq_w // per)
    hcat0 = _inproj(x2, w_in0, tab, rope0, seq)
    oa = _na_attention(hcat0, l0_na_rpb, batch, seq)
    lambda_init = 0.8 - 0.6 * math.exp(-0.3 * 0)
    lam = (jnp.exp(jnp.sum(l0_diff_lq1.astype(F32) * l0_diff_lk1.astype(F32)))
           - jnp.exp(jnp.sum(l0_diff_lq2.astype(F32) * l0_diff_lk2.astype(F32))) + lambda_init)
    od = _diff_attention(hcat0, lam.reshape(1).astype(F32), l0_diff_subln_g, lambda_init, batch, seq)
    h, h_pk = _outproj_ln([oa, od], l0_w_out.astype(BF16), x2, l0_ln1_g, l0_ln1_b)
    x2 = _moe_layer(h, h_pk, l0_router_w, l0_router_bias, l0_expert_w_gate, l0_expert_w_up,
                    l0_expert_w_down, l0_shared_w_gate, l0_shared_w_up, l0_shared_w_down,
                    l0_ln2_g, l0_ln2_b)

    q_w = SWA_Q_HEADS * HEAD_DIM
    kv_w = SWA_KV_HEADS * HEAD_DIM
    w1 = l1_w_in.astype(F32)
    wq = w1[:, :q_w] * qscale
    wk = w1[:, q_w:q_w + kv_w].reshape(d, SWA_KV_HEADS, HEAD_DIM)
    wv = w1[:, q_w + kv_w:].reshape(d, SWA_KV_HEADS, HEAD_DIM)
    wkv = jnp.concatenate([wk, wv], axis=-1).reshape(d, 2 * kv_w)
    w_in1 = jnp.concatenate([wq, wkv], axis=1).astype(BF16)
    rope1 = [1] * (q_w // per) + [2] * (2 * kv_w // per)
    hcat1 = _inproj(x2, w_in1, tab, rope1, seq)
    o1 = _swa_attention(hcat1, l1_swa_sinks, batch, seq)
    h, h_pk = _outproj_ln([o1], l1_w_out.astype(BF16), x2, l1_ln1_g, l1_ln1_b)
    x2 = _moe_layer(h, h_pk, l1_router_w, l1_router_bias, l1_expert_w_gate, l1_expert_w_up,
                    l1_expert_w_down, l1_shared_w_gate, l1_shared_w_up, l1_shared_w_down,
                    l1_ln2_g, l1_ln2_b)
    return x2.reshape(batch, seq, d).astype(x.dtype)
```

```python
import functools
import math

import numpy as np
import jax
import jax.numpy as jnp
from jax import lax
from jax.experimental import pallas as pl
from jax.experimental.pallas import tpu as pltpu
from jax.experimental.pallas import tpu_sc as plsc

F32 = jnp.float32
BF16 = jnp.bfloat16
I32 = jnp.int32

HEAD_DIM = 64
GRID_W = 64
NA_HEADS = 8
NA_WIN_ROWS = 8
NA_WIN_COLS = 16
DIFF_HEADS = 4
SWA_Q_HEADS = 16
SWA_KV_HEADS = 4
SWA_WINDOW = 128
SWA_BLOCK = 128
ROPE_THETA = 10000.0
N_EXPERTS = 256
TOP_K = 8
N_GROUPS = 8
TOPK_GROUPS = 4
ROUTED_SCALE = 2.5
LN_EPS = 1e-5
DEPTH = 2
DN_ALPHA = (2 * DEPTH) ** 0.25

LOG2E = 1.4426950408889634
NEG_BIG = -1e30
LANES = 128
NA_KEY_ROWS = NA_WIN_ROWS + 1
EXPERT_BLOCK = 512
EXPERT_CHAINS = 2
DIFF_UNROLL = 4
DIFF_ONES_ROWS = 16
PACK_SPLIT = 4
SC_CORES = 2
SC_SUBCORES = 16
SC_WINDOW = 128
VMEM_LIMIT = 56 * 1024 * 1024


def _cparams(sem):
    return pltpu.CompilerParams(dimension_semantics=sem, vmem_limit_bytes=VMEM_LIMIT)


def _silu(x):
    return x / (1.0 + jnp.exp(-x))


def _pack_rows(y):
    half = y.shape[1] // 2
    bits = pltpu.bitcast(y.astype(BF16).astype(F32), I32)
    lo = lax.shift_right_logical(bits[:, :half], 16)
    hi = bits[:, half:] & jnp.int32(-65536)
    return hi | lo


def _unpack_words(w):
    lo = pltpu.bitcast(lax.shift_left(w, 16), F32)
    hi = pltpu.bitcast(w & jnp.int32(-65536), F32)
    return lo, hi


def _unpack_rows(ref_at):
    los, his = [], []
    for c in range(PACK_SPLIT):
        lo, hi = _unpack_words(ref_at(c))
        los.append(lo)
        his.append(hi)
    return jnp.concatenate(los + his, axis=1)


def _layer_norm(y, g, b):
    mu = jnp.mean(y, axis=-1, keepdims=True)
    yc = y - mu
    var = jnp.mean(yc * yc, axis=-1, keepdims=True)
    return yc * lax.rsqrt(var + LN_EPS) * g + b


def _inproj_body(x_ref, w_ref, tab_ref, o_ref, *, chunk, rope_kind):
    tm = x_ref.shape[0]
    n_out = w_ref.shape[1]
    x = x_ref[...].astype(BF16)
    lane = lax.broadcasted_iota(I32, (tm, LANES), 1)
    first_half = (lane % HEAD_DIM) < (HEAD_DIM // 2)
    per = chunk // LANES
    for c in range(n_out // chunk):
        h = jnp.dot(x, w_ref[:, c * chunk:(c + 1) * chunk], preferred_element_type=F32)
        kinds = rope_kind[c * per:(c + 1) * per]
        if any(kinds):
            pieces = []
            for j, kind in enumerate(kinds):
                t = h[:, j * LANES:(j + 1) * LANES]
                if kind:
                    base = (kind - 1) * 2 * LANES
                    cos = tab_ref[:, base:base + LANES]
                    sin = tab_ref[:, base + LANES:base + 2 * LANES]
                    rot = jnp.where(first_half, pltpu.roll(t, LANES - HEAD_DIM // 2, 1),
                                    pltpu.roll(t, HEAD_DIM // 2, 1))
                    t = t * cos + rot * sin
                pieces.append(t)
            h = jnp.concatenate(pieces, axis=1)
        o_ref[:, c * chunk:(c + 1) * chunk] = h.astype(o_ref.dtype)


def _inproj(x2, w_bf, tab, rope_kind, seq, tm=512, chunk=256):
    t, d = x2.shape
    n_out = w_bf.shape[1]
    tm = min(tm, seq)
    sb = seq // tm
    return pl.pallas_call(
        functools.partial(_inproj_body, chunk=chunk, rope_kind=tuple(rope_kind)),
        grid=(t // tm,),
        in_specs=[pl.BlockSpec((tm, d), lambda i: (i, 0)),
                  pl.BlockSpec((d, n_out), lambda i: (0, 0)),
                  pl.BlockSpec((tm, tab.shape[1]), lambda i: (i % sb, 0))],
        out_specs=pl.BlockSpec((tm, n_out), lambda i: (i, 0)),
        out_shape=jax.ShapeDtypeStruct((t, n_out), BF16),
        compiler_params=_cparams(("parallel",)),
        name="inproj",
    )(x2, w_bf, tab)


def _rope_table(seq):
    half = HEAD_DIM // 2
    inv = 1.0 / (ROPE_THETA ** (jnp.arange(0, HEAD_DIM, 2, dtype=F32) / HEAD_DIM))
    ang = jnp.arange(seq, dtype=F32)[:, None] * inv[None, :]
    cos = jnp.cos(ang)
    sin = jnp.sin(ang)
    cos64 = jnp.concatenate([cos, cos], axis=-1)
    sin64 = jnp.concatenate([-sin, sin], axis=-1)
    one = jnp.ones((seq, HEAD_DIM), F32)
    zero = jnp.zeros((seq, HEAD_DIM), F32)
    del half
    return jnp.concatenate([cos64, cos64, sin64, sin64, cos64, one, sin64, zero], axis=-1)


def _na_body(pat_ref, q_ref, k_ref, v_ref, bias_ref, o_ref, *, rows_n):
    del pat_ref
    r = pl.program_id(2)
    nq = q_ref.shape[0]
    nk = NA_KEY_ROWS * GRID_W
    ks = jnp.clip(2 * r - NA_WIN_ROWS // 2, 0, rows_n - NA_KEY_ROWS)
    start = pl.multiple_of(ks * GRID_W, GRID_W)
    kwin = k_ref[pl.ds(start, nk), :]
    vwin = v_ref[pl.ds(start, nk), :]
    q = q_ref[...].astype(F32)
    lane = lax.broadcasted_iota(I32, (nq, LANES), 1)
    low = lane < HEAD_DIM
    qm = jnp.concatenate([jnp.where(low, q, 0.0), jnp.where(low, 0.0, q)], axis=0).astype(BF16)
    s = lax.dot_general(qm, kwin, (((1,), (1,)), ((), ())), preferred_element_type=F32)
    s = s + bias_ref[0].reshape(2 * nq, nk)
    m = jnp.max(s, axis=-1, keepdims=True)
    p = jnp.exp2(s - m)
    l = jnp.sum(p, axis=-1, keepdims=True)
    pv = jnp.dot(p.astype(BF16), vwin, preferred_element_type=F32) / l
    o_ref[...] = jnp.where(low, pv[:nq], pv[nq:]).astype(o_ref.dtype)


def _na_bias(rpb, rows_n):
    half = NA_WIN_ROWS // 2
    pats, pat_id = [], []
    for blk in range(rows_n // 2):
        r0 = 2 * blk
        ks = min(max(r0 - half, 0), rows_n - NA_KEY_ROWS)
        starts = tuple(min(max(r0 + i - half, 0), rows_n - NA_WIN_ROWS) - ks for i in range(2))
        key = (r0 - ks, starts)
        if key not in pats:
            pats.append(key)
        pat_id.append(pats.index(key))
    wq = np.arange(GRID_W)
    wk = np.arange(GRID_W)
    col_start = np.clip(wq - NA_WIN_COLS // 2, 0, GRID_W - NA_WIN_COLS)
    col_off = wk[None, :] - col_start[:, None]
    col_valid = (col_off >= 0) & (col_off < NA_WIN_COLS)
    dc = np.clip(wk[None, :] - wq[:, None], -(NA_WIN_COLS - 1), NA_WIN_COLS - 1) + (NA_WIN_COLS - 1)
    n_dc = 2 * NA_WIN_COLS - 1
    onehot = (dc.reshape(-1)[None, :] == np.arange(n_dc)[:, None]).astype(np.float32)
    heads, n_dr = rpb.shape[0], rpb.shape[1]
    col = jnp.dot(rpb.astype(F32).reshape(heads * n_dr, n_dc), jnp.asarray(onehot),
                  precision=lax.Precision.HIGHEST).reshape(heads, n_dr, GRID_W, GRID_W)
    col = jnp.where(col_valid[None, None], col * LOG2E, NEG_BIG)
    masked_blk = jnp.full((heads, GRID_W, GRID_W), NEG_BIG, F32)
    tables = []
    for r0rel, starts in pats:
        qrows = []
        for qi in range(2):
            blks = []
            for kr in range(NA_KEY_ROWS):
                row_ok = starts[qi] <= kr < starts[qi] + NA_WIN_ROWS
                dr = kr - (r0rel + qi) + (NA_WIN_ROWS - 1)
                blks.append(col[:, dr] if row_ok else masked_blk)
            qrows.append(jnp.concatenate(blks, axis=-1))
        tables.append(jnp.concatenate(qrows, axis=1))
    return jnp.stack(tables, axis=0), np.asarray(pat_id, np.int32)


def _na_attention(hcat, rpb, batch, seq):
    t = hcat.shape[0]
    rows_n = seq // GRID_W
    nblk = rows_n // 2
    nq = 2 * GRID_W
    bias, pat_id = _na_bias(rpb, rows_n)
    nk = NA_KEY_ROWS * GRID_W
    hp = NA_HEADS // 2
    grid_spec = pltpu.PrefetchScalarGridSpec(
        num_scalar_prefetch=1,
        grid=(batch, hp, nblk),
        in_specs=[
            pl.BlockSpec((nq, LANES), lambda b, h, r, pat: (b * nblk + r, h)),
            pl.BlockSpec((seq, LANES), lambda b, h, r, pat: (b, hp + h)),
            pl.BlockSpec((seq, LANES), lambda b, h, r, pat: (b, 2 * hp + h)),
            pl.BlockSpec((1, 2, nq, nk), lambda b, h, r, pat: (pat[r], h, 0, 0)),
        ],
        out_specs=pl.BlockSpec((nq, LANES), lambda b, h, r, pat: (b * nblk + r, h)),
    )
    return pl.pallas_call(
        functools.partial(_na_body, rows_n=rows_n),
        grid_spec=grid_spec,
        out_shape=jax.ShapeDtypeStruct((t, NA_HEADS * HEAD_DIM), BF16),
        compiler_params=_cparams(("parallel", "parallel", "arbitrary")),
        name="na_attn",
    )(jnp.asarray(pat_id), hcat, hcat, hcat, bias)


def _diff_body(lam_ref, q_ref, k_ref, v_ref, g_ref, o_ref, vt_ref, acc_ref, sta_ref, stb_ref, *, tk, lambda_init):
    j = pl.program_id(2)
    tq = q_ref.shape[0]
    seq = k_ref.shape[0]
    nkv = seq // tk

    @pl.when(j == 0)
    def _():
        ones = jnp.ones((DIFF_ONES_ROWS, tk), BF16)
        for c in range(nkv):
            vt_ref[c, :LANES, :] = v_ref[c * tk:(c + 1) * tk, :].astype(F32).T.astype(BF16)
            vt_ref[c, LANES:, :] = ones

    q = q_ref[...].astype(F32)
    lane = lax.broadcasted_iota(I32, (tq, LANES), 1)
    q1 = jnp.where(lane < HEAD_DIM, q, 0.0).T
    q2 = jnp.where(lane >= HEAD_DIM, q, 0.0).T
    rhs = jnp.concatenate([q1, q2], axis=1).astype(BF16)

    def scores(i, dst_ref):
        start = pl.multiple_of(jnp.minimum(i, nkv - 1) * tk, tk)
        dst_ref[...] = jnp.dot(k_ref[pl.ds(start, tk), :], rhs, preferred_element_type=F32)

    def consume(i, src_ref, m):
        st = src_ref[...]
        m_new = jnp.maximum(m, jnp.max(st, axis=0, keepdims=True))
        p = jnp.exp2((st - m_new).astype(BF16))
        pv = jnp.dot(vt_ref[i], p, preferred_element_type=F32)
        acc_ref[...] = jnp.exp2(m - m_new) * acc_ref[...] + pv
        return m_new

    acc_ref[...] = jnp.zeros_like(acc_ref)
    scores(0, sta_ref)

    unroll = min(DIFF_UNROLL, nkv)

    def group(ii, m):
        bufs = (sta_ref, stb_ref)
        for u in range(unroll):
            i = unroll * ii + u
            scores(i + 1, bufs[(u + 1) % 2])
            m = consume(i, bufs[u % 2], m)
        return m

    lax.fori_loop(0, nkv // unroll, group, jnp.full((1, 2 * tq), NEG_BIG, F32))
    acc = acc_ref[...]
    ot = acc[:LANES, :] / acc[LANES:LANES + 1, :]
    dt = ot[:, :tq] - lam_ref[0] * ot[:, tq:]
    ms = jnp.mean(dt * dt, axis=0, keepdims=True)
    y = dt * lax.rsqrt(ms + LN_EPS) * g_ref[...] * (1.0 - lambda_init)
    o_ref[...] = y.T.astype(o_ref.dtype)


def _diff_attention(hcat, lam, subln_g, lambda_init, batch, seq, tq=256, tk=512):
    t = hcat.shape[0]
    tq = min(tq, seq)
    tk = min(tk, seq)
    nq = seq // tq
    qoff = 3 * NA_HEADS * HEAD_DIM // LANES
    koff = qoff + DIFF_HEADS
    voff = koff + DIFF_HEADS
    grid_spec = pltpu.PrefetchScalarGridSpec(
        num_scalar_prefetch=1,
        grid=(batch, DIFF_HEADS, nq),
        in_specs=[
            pl.BlockSpec((tq, LANES), lambda b, h, j, lam: (b * nq + j, qoff + h)),
            pl.BlockSpec((seq, LANES), lambda b, h, j, lam: (b, koff + h)),
            pl.BlockSpec((seq, LANES), lambda b, h, j, lam: (b, voff + h)),
            pl.BlockSpec((LANES, 1), lambda b, h, j, lam: (0, 0)),
        ],
        out_specs=pl.BlockSpec((tq, LANES), lambda b, h, j, lam: (b * nq + j, h)),
        scratch_shapes=[pltpu.VMEM((seq // tk, LANES + DIFF_ONES_ROWS, tk), BF16),
                        pltpu.VMEM((LANES + DIFF_ONES_ROWS, 2 * tq), F32),
                        pltpu.VMEM((tk, 2 * tq), F32),
                        pltpu.VMEM((tk, 2 * tq), F32)],
    )
    return pl.pallas_call(
        functools.partial(_diff_body, tk=tk, lambda_init=lambda_init),
        grid_spec=grid_spec,
        out_shape=jax.ShapeDtypeStruct((t, DIFF_HEADS * 2 * HEAD_DIM), BF16),
        compiler_params=_cparams(("parallel", "parallel", "arbitrary")),
        name="diff_attn",
    )(lam, hcat, hcat, hcat, subln_g.astype(F32).reshape(LANES, 1))


def _swa_body(sink_ref, q_ref, kv_ref, o_ref, kvt_ref):
    kvh = pl.program_id(1)
    n = pl.program_id(2)
    nq = q_ref.shape[0]
    seq = kv_ref.shape[0]
    nblk = seq // SWA_BLOCK
    wb = min(3, nblk)
    nk = wb * SWA_BLOCK

    @pl.when(n == 0)
    def _():
        for c in range(nblk):
            kvt_ref[c] = kv_ref[c * SWA_BLOCK:(c + 1) * SWA_BLOCK, :].astype(F32).T.astype(BF16)

    b0 = jnp.clip(n - 1, 0, nblk - wb)
    start = pl.multiple_of(b0 * SWA_BLOCK, SWA_BLOCK)
    kv = kv_ref[pl.ds(start, nk), :]
    kvt = jnp.concatenate([kvt_ref[b0 + w] for w in range(wb)], axis=1)
    group = SWA_Q_HEADS // SWA_KV_HEADS
    half = LANES // 2

    def swap_halves(a):
        return jnp.concatenate([a[half:], a[:half]], axis=0)

    top = lax.broadcasted_iota(I32, (LANES, nq), 0) < half
    cols = []
    for c in range(group // 2):
        qt = q_ref[:, c * LANES:(c + 1) * LANES].astype(F32).T
        cols.append(jnp.where(top, qt, 0.0))
        cols.append(jnp.where(top, swap_halves(qt), 0.0))
    rhs = jnp.concatenate(cols, axis=1).astype(BF16)
    st = jnp.dot(kv, rhs, preferred_element_type=F32)
    kpos = start + lax.broadcasted_iota(I32, (nk, nq), 0)
    qpos = n * SWA_BLOCK + lax.broadcasted_iota(I32, (nk, nq), 1)
    mask = jnp.where(jnp.abs(kpos - qpos) <= SWA_WINDOW, 0.0, NEG_BIG)
    st = st + jnp.concatenate([mask] * group, axis=1)
    sink = jnp.concatenate([jnp.full((1, nq), sink_ref[kvh * group + g], F32) for g in range(group)], axis=1)
    m = jnp.maximum(jnp.max(st, axis=0, keepdims=True), sink)
    e = jnp.exp2(st - m)
    den = jnp.sum(e, axis=0, keepdims=True) + jnp.exp2(sink - m)
    ot = jnp.dot(kvt, e.astype(BF16), preferred_element_type=F32) / den
    for c in range(group // 2):
        even = ot[:, 2 * c * nq:(2 * c + 1) * nq]
        odd = ot[:, (2 * c + 1) * nq:(2 * c + 2) * nq]
        blk = jnp.where(top, swap_halves(even), odd)
        o_ref[:, c * LANES:(c + 1) * LANES] = blk.T.astype(o_ref.dtype)


def _swa_attention(hcat, sinks, batch, seq):
    t = hcat.shape[0]
    nb = seq // SWA_BLOCK
    group = SWA_Q_HEADS // SWA_KV_HEADS
    qw = group * HEAD_DIM
    kvoff = SWA_Q_HEADS * HEAD_DIM // LANES
    grid_spec = pltpu.PrefetchScalarGridSpec(
        num_scalar_prefetch=1,
        grid=(batch, SWA_KV_HEADS, nb),
        in_specs=[
            pl.BlockSpec((SWA_BLOCK, qw), lambda b, h, n, s: (b * nb + n, h)),
            pl.BlockSpec((seq, LANES), lambda b, h, n, s: (b, kvoff + h)),
        ],
        out_specs=pl.BlockSpec((SWA_BLOCK, qw), lambda b, h, n, s: (b * nb + n, h)),
        scratch_shapes=[pltpu.VMEM((nb, LANES, SWA_BLOCK), BF16)],
    )
    return pl.pallas_call(
        _swa_body,
        grid_spec=grid_spec,
        out_shape=jax.ShapeDtypeStruct((t, SWA_Q_HEADS * HEAD_DIM), BF16),
        compiler_params=_cparams(("parallel", "parallel", "arbitrary")),
        name="swa_attn",
    )(sinks.astype(F32) * LOG2E, hcat, hcat)


def _outproj_body(*refs, n_in):
    a_refs = refs[:n_in]
    w_ref, x_ref, g_ref, b_ref, o_ref, opk_ref = refs[n_in:]
    acc = None
    off = 0
    for a_ref in a_refs:
        ka = a_ref.shape[1]
        d = jnp.dot(a_ref[...], w_ref[off:off + ka, :], preferred_element_type=F32)
        acc = d if acc is None else acc + d
        off += ka
    y = DN_ALPHA * x_ref[...] + acc
    out = _layer_norm(y, g_ref[...], b_ref[...])
    o_ref[...] = out
    packed = _pack_rows(out)
    for c in range(PACK_SPLIT):
        opk_ref[c] = packed[:, c * LANES:(c + 1) * LANES]


def _outproj_ln(acts, w_bf, x2, g, b, tm=512):
    t, d = x2.shape
    tm = min(tm, t)
    in_specs = [pl.BlockSpec((tm, a.shape[1]), lambda i: (i, 0)) for a in acts]
    in_specs += [pl.BlockSpec(w_bf.shape, lambda i: (0, 0)),
                 pl.BlockSpec((tm, d), lambda i: (i, 0)),
                 pl.BlockSpec((1, d), lambda i: (0, 0)),
                 pl.BlockSpec((1, d), lambda i: (0, 0))]
    return pl.pallas_call(
        functools.partial(_outproj_body, n_in=len(acts)),
        grid=(t // tm,),
        in_specs=in_specs,
        out_specs=[pl.BlockSpec((tm, d), lambda i: (i, 0)),
                   pl.BlockSpec((PACK_SPLIT, tm, LANES), lambda i: (0, i, 0))],
        out_shape=[jax.ShapeDtypeStruct((t, d), F32), jax.ShapeDtypeStruct((PACK_SPLIT, t, LANES), I32)],
        compiler_params=_cparams(("parallel",)),
        name="outproj_ln",
    )(*acts, w_bf, x2, g.astype(F32).reshape(1, d), b.astype(F32).reshape(1, d))


def _router_body(h_ref, whi_ref, wlo_ref, bias_ref, idx_ref, gate_ref, pos_ref, cnt_ref, base_ref):
    i = pl.program_id(0)
    tm = h_ref.shape[0]
    gsz = N_EXPERTS // N_GROUPS

    @pl.when(i == 0)
    def _():
        base_ref[...] = jnp.zeros_like(base_ref)

    h = h_ref[...]
    h_hi = h.astype(BF16)
    h_lo = (h - h_hi.astype(F32)).astype(BF16)
    dn = (((1,), (1,)), ((), ()))
    whi = whi_ref[...]
    logits = (lax.dot_general(whi, h_hi, dn, preferred_element_type=F32)
              + lax.dot_general(whi, h_lo, dn, preferred_element_type=F32)
              + lax.dot_general(wlo_ref[...], h_hi, dn, preferred_element_type=F32))
    scores = 1.0 / (1.0 + jnp.exp(-logits))
    choice = scores + bias_ref[...]

    iota_g = lax.broadcasted_iota(I32, (gsz, tm), 0)
    gscore = []
    for g in range(N_GROUPS):
        cg = choice[g * gsz:(g + 1) * gsz, :]
        m1 = jnp.max(cg, axis=0, keepdims=True)
        first = jnp.min(jnp.where(cg == m1, iota_g, gsz), axis=0, keepdims=True)
        m2 = jnp.max(jnp.where(iota_g == first, -jnp.inf, cg), axis=0, keepdims=True)
        gscore.append(m1 + m2)
    pieces = []
    for g in range(N_GROUPS):
        rank = jnp.zeros((1, tm), I32)
        for o in range(N_GROUPS):
            if o == g:
                continue
            beats = (gscore[o] > gscore[g]) if o > g else (gscore[o] >= gscore[g])
            rank = rank + beats.astype(I32)
        keep = rank < TOPK_GROUPS
        pieces.append(jnp.where(keep, choice[g * gsz:(g + 1) * gsz, :], -jnp.inf))
    masked = jnp.concatenate(pieces, axis=0)

    iota_e = lax.broadcasted_iota(I32, (N_EXPERTS, tm), 0)
    sel_all = jnp.zeros((N_EXPERTS, tm), F32)
    idxs, gates, sels = [], [], []
    for _ in range(TOP_K):
        mx = jnp.max(masked, axis=0, keepdims=True)
        idx = jnp.min(jnp.where(masked == mx, iota_e, N_EXPERTS), axis=0, keepdims=True)
        sel = iota_e == idx
        gates.append(jnp.sum(jnp.where(sel, scores, 0.0), axis=0, keepdims=True))
        masked = jnp.where(sel, -jnp.inf, masked)
        sel_all = sel_all + sel.astype(F32)
        idxs.append(idx)
        sels.append(sel)
    gsum = gates[0]
    for gk in gates[1:]:
        gsum = gsum + gk
    gate_ref[...] = jnp.concatenate(gates, axis=0) / gsum * ROUTED_SCALE
    idx_ref[...] = jnp.concatenate(idxs, axis=0)

    tri = (lax.broadcasted_iota(I32, (tm, tm), 0) < lax.broadcasted_iota(I32, (tm, tm), 1))
    cum = jnp.dot(sel_all.astype(BF16), tri.astype(F32).astype(BF16), preferred_element_type=F32)
    tot = cum + base_ref[...]
    pos = [jnp.sum(jnp.where(sel, tot, 0.0), axis=0, keepdims=True) for sel in sels]
    pos_ref[...] = jnp.concatenate(pos, axis=0).astype(I32)
    base_ref[...] = base_ref[...] + jnp.sum(sel_all, axis=1, keepdims=True)
    cnt_ref[...] = base_ref[...]


def _router(h, router_w, router_bias, tm=512):
    t, d = h.shape
    tm = min(tm, t)
    wt = router_w.astype(F32).T
    whi = wt.astype(BF16)
    wlo = (wt - whi.astype(F32)).astype(BF16)
    return pl.pallas_call(
        _router_body,
        grid=(t // tm,),
        in_specs=[pl.BlockSpec((tm, d), lambda i: (i, 0)),
                  pl.BlockSpec((N_EXPERTS, d), lambda i: (0, 0)),
                  pl.BlockSpec((N_EXPERTS, d), lambda i: (0, 0)),
                  pl.BlockSpec((N_EXPERTS, 1), lambda i: (0, 0))],
        out_specs=[pl.BlockSpec((TOP_K, tm), lambda i: (0, i)),
                   pl.BlockSpec((TOP_K, tm), lambda i: (0, i)),
                   pl.BlockSpec((TOP_K, tm), lambda i: (0, i)),
                   pl.BlockSpec((N_EXPERTS, 1), lambda i: (0, 0))],
        out_shape=[jax.ShapeDtypeStruct((TOP_K, t), I32),
                   jax.ShapeDtypeStruct((TOP_K, t), F32),
                   jax.ShapeDtypeStruct((TOP_K, t), I32),
                   jax.ShapeDtypeStruct((N_EXPERTS, 1), F32)],
        scratch_shapes=[pltpu.VMEM((N_EXPERTS, 1), F32)],
        compiler_params=_cparams(("arbitrary",)),
        name="router",
    )(h, whi, wlo, router_bias.astype(F32).reshape(N_EXPERTS, 1))


def _expert_body(be_ref, na_ref, bv_ref, x_ref, wg_ref, wu_ref, wd_ref, y_ref, wgu, wdb):
    i = pl.program_id(0)
    bm = x_ref.shape[1]
    ff = wd_ref.shape[1]
    hm = bm // EXPERT_CHAINS

    @pl.when(i < na_ref[0])
    def _():
        prev = be_ref[jnp.maximum(i - 1, 0)]

        @pl.when((i == 0) | (be_ref[i] != prev))
        def _():
            wgu[:, :ff] = wg_ref[0].astype(BF16)
            wgu[:, ff:] = wu_ref[0].astype(BF16)
            wdb[...] = wd_ref[0].astype(BF16)

        valid = bv_ref[i]
        row = lax.broadcasted_iota(I32, (hm, LANES), 0)
        def run(n_chains):
            gus = []
            for hb in range(n_chains):
                rows = slice(hb * hm, (hb + 1) * hm)
                keep = row < valid - hb * hm
                x = _unpack_rows(lambda c: jnp.where(keep, x_ref[c, rows, :], 0)).astype(BF16)
                gus.append(jnp.dot(x, wgu[...], preferred_element_type=F32))
            ys = []
            for gu in gus:
                hmid = (_silu(gu[:, :ff]) * gu[:, ff:]).astype(BF16)
                ys.append(jnp.dot(hmid, wdb[...], preferred_element_type=F32))
            for hb, y in enumerate(ys):
                packed = _pack_rows(y)
                for c in range(PACK_SPLIT):
                    y_ref[c, hb * hm:(hb + 1) * hm, :] = packed[:, c * LANES:(c + 1) * LANES]
            if n_chains < EXPERT_CHAINS:
                y_ref[:, n_chains * hm:, :] = jnp.zeros((PACK_SPLIT, bm - n_chains * hm, LANES), I32)

        for n_chains in range(1, EXPERT_CHAINS + 1):
            lo = (n_chains - 1) * hm
            cond = valid > lo if n_chains == EXPERT_CHAINS else (valid > lo) & (valid <= lo + hm)
            pl.when(cond)(functools.partial(run, n_chains))


def _expert_matmul(xs, blk_expert, n_active, blk_valid, w_gate, w_up, w_down):
    n_slots = xs.shape[1]
    bm = EXPERT_BLOCK
    n_blocks = n_slots // bm
    d, ff = w_gate.shape[1], w_gate.shape[2]

    def row_map(i, be, na, bv):
        return (0, jnp.minimum(i, na[0] - 1), 0)

    def w_map(i, be, na, bv):
        return (be[jnp.minimum(i, na[0] - 1)], 0, 0)

    grid_spec = pltpu.PrefetchScalarGridSpec(
        num_scalar_prefetch=3,
        grid=(n_blocks,),
        in_specs=[pl.BlockSpec((PACK_SPLIT, bm, LANES), row_map),
                  pl.BlockSpec((1, d, ff), w_map),
                  pl.BlockSpec((1, d, ff), w_map),
                  pl.BlockSpec((1, ff, d), w_map)],
        out_specs=pl.BlockSpec((PACK_SPLIT, bm, LANES), row_map),
        scratch_shapes=[pltpu.VMEM((d, 2 * ff), BF16), pltpu.VMEM((ff, d), BF16)],
    )
    return pl.pallas_call(
        _expert_body,
        grid_spec=grid_spec,
        out_shape=jax.ShapeDtypeStruct((PACK_SPLIT, n_slots, LANES), I32),
        compiler_params=_cparams(("arbitrary",)),
        name="expert_mlp",
    )(blk_expert, n_active, blk_valid, xs, w_gate, w_up, w_down)


def _slot_body(idx_ref, pos_ref, ps_ref, slot_ref, *, n_slots):
    tm = idx_ref.shape[1]
    iota_e = lax.broadcasted_iota(I32, (N_EXPERTS, tm), 0)
    ps = ps_ref[...]
    rows = []
    for k in range(TOP_K):
        start = jnp.sum(jnp.where(iota_e == idx_ref[k:k + 1, :], ps, 0.0), axis=0, keepdims=True)
        rows.append(start.astype(I32) + pos_ref[k:k + 1, :])
    slot = jnp.concatenate(rows, axis=0)
    for c in range(PACK_SPLIT):
        slot_ref[c * TOP_K:(c + 1) * TOP_K, :] = slot + c * n_slots


def _slots(idx, pos, pad_start, n_slots, tm=512):
    t = idx.shape[1]
    tm = min(tm, t)
    return pl.pallas_call(
        functools.partial(_slot_body, n_slots=n_slots),
        grid=(t // tm,),
        in_specs=[pl.BlockSpec((TOP_K, tm), lambda i: (0, i)),
                  pl.BlockSpec((TOP_K, tm), lambda i: (0, i)),
                  pl.BlockSpec((N_EXPERTS, 1), lambda i: (0, 0))],
        out_specs=pl.BlockSpec((PACK_SPLIT * TOP_K, tm), lambda i: (0, i)),
        out_shape=jax.ShapeDtypeStruct((PACK_SPLIT * TOP_K, t), I32),
        compiler_params=_cparams(("parallel",)),
        name="slots",
    )(idx, pos, pad_start.astype(F32).reshape(N_EXPERTS, 1))


def _sc_mesh():
    return plsc.VectorSubcoreMesh(core_axis_name="core", subcore_axis_name="subcore",
                                  num_cores=SC_CORES, num_subcores=SC_SUBCORES)


def _sc_scatter_rows(rows, idx, n_out):
    t = idx.shape[1]
    nj = t // SC_WINDOW

    @functools.partial(pl.kernel, out_type=jax.ShapeDtypeStruct((n_out, LANES), I32),
                       mesh=_sc_mesh(), scratch_types=[], name="sc_dispatch")
    def run(rows_hbm, idx_hbm, out_hbm):
        def body(rows_vmem, idx_vmem):
            for k in range(TOP_K):
                pltpu.sync_copy(rows_vmem, out_hbm.at[idx_vmem.at[k]])

        pltpu.emit_pipeline(
            body,
            grid=(rows.shape[0] // SC_WINDOW,),
            in_specs=[pl.BlockSpec((SC_WINDOW, LANES), lambda s: (s, 0)),
                      pl.BlockSpec((TOP_K, SC_WINDOW), lambda s: (s // nj, s % nj))],
            out_specs=[],
            core_axis_name=("core", "subcore"),
            dimension_semantics=(pltpu.PARALLEL,),
        )(rows_hbm, idx_hbm)

    return run(rows, idx)


def _sc_gather_rows(src, idx):
    nr, t = idx.shape
    nj = t // SC_WINDOW

    @functools.partial(pl.kernel, out_type=jax.ShapeDtypeStruct((nr * t, LANES), I32),
                       mesh=_sc_mesh(), scratch_types=[], name="sc_combine")
    def run(src_hbm, idx_hbm, out_hbm):
        def body(idx_vmem, out_vmem):
            pltpu.sync_copy(src_hbm.at[idx_vmem.at[0]], out_vmem)

        pltpu.emit_pipeline(
            body,
            grid=(nr * nj,),
            in_specs=[pl.BlockSpec((1, SC_WINDOW), lambda s: (s // nj, s % nj))],
            out_specs=[pl.BlockSpec((SC_WINDOW, LANES), lambda s: (s, 0))],
            core_axis_name=("core", "subcore"),
            dimension_semantics=(pltpu.PARALLEL,),
        )(idx_hbm, out_hbm)

    return run(src, idx)


def _combine_body(h_ref, yg_ref, gt_ref, sg_ref, su_ref, sd_ref, g_ref, b_ref, o_ref):
    h = h_ref[...]
    hb = h.astype(BF16)
    a = jnp.dot(hb, sg_ref[...], preferred_element_type=F32)
    u = jnp.dot(hb, su_ref[...], preferred_element_type=F32)
    moe = jnp.dot((_silu(a) * u).astype(BF16), sd_ref[...], preferred_element_type=F32)
    gt = gt_ref[...]
    for k in range(TOP_K):
        moe = moe + gt[:, k:k + 1] * _unpack_rows(lambda c, k=k: yg_ref[c, k])
    o_ref[...] = _layer_norm(DN_ALPHA * h + moe, g_ref[...], b_ref[...])


def _combine_ln(h, yg, gates_t, sh_gate, sh_up, sh_down, g, b, tm=256):
    t, d = h.shape
    tm = min(tm, t)
    ff = sh_gate.shape[1]
    return pl.pallas_call(
        _combine_body,
        grid=(t // tm,),
        in_specs=[pl.BlockSpec((tm, d), lambda i: (i, 0)),
                  pl.BlockSpec((PACK_SPLIT, TOP_K, tm, LANES), lambda i: (0, 0, i, 0)),
                  pl.BlockSpec((tm, TOP_K), lambda i: (i, 0)),
                  pl.BlockSpec((d, ff), lambda i: (0, 0)),
                  pl.BlockSpec((d, ff), lambda i: (0, 0)),
                  pl.BlockSpec((ff, d), lambda i: (0, 0)),
                  pl.BlockSpec((1, d), lambda i: (0, 0)),
                  pl.BlockSpec((1, d), lambda i: (0, 0))],
        out_specs=pl.BlockSpec((tm, d), lambda i: (i, 0)),
        out_shape=jax.ShapeDtypeStruct((t, d), F32),
        compiler_params=_cparams(("parallel",)),
        name="combine_ln",
    )(h, yg, gates_t, sh_gate.astype(BF16), sh_up.astype(BF16), sh_down.astype(BF16),
      g.astype(F32).reshape(1, d), b.astype(F32).reshape(1, d))


def _moe_layer(h, h_pk, router_w, router_bias, w_gate, w_up, w_down, sh_gate, sh_up, sh_down, g, b):
    t, d = h.shape
    bm = EXPERT_BLOCK
    idx, gates, pos, cnt = _router(h, router_w, router_bias)
    counts = cnt[:, 0].astype(I32)
    padded = (counts + bm - 1) // bm * bm
    e_iota = jnp.arange(N_EXPERTS, dtype=I32)
    pad_end = jnp.sum(jnp.where(e_iota[None, :] <= e_iota[:, None], padded[None, :], 0), axis=1)
    pad_start = pad_end - padded
    n_blocks = t * TOP_K // bm + N_EXPERTS
    n_slots = n_blocks * bm
    blk_start = jnp.arange(n_blocks, dtype=I32) * bm
    blk_expert = jnp.minimum(
        jnp.sum((blk_start[:, None] >= pad_end[None, :]).astype(I32), axis=1), N_EXPERTS - 1)
    real_end = jnp.sum(jnp.where(e_iota[None, :] == blk_expert[:, None], (pad_start + counts)[None, :], 0), axis=1)
    blk_valid = jnp.clip(real_end - blk_start, 0, bm).astype(I32)
    n_active = (pad_end[-1:] // bm).astype(I32)

    slot4 = _slots(idx, pos, pad_start, n_slots)
    xs = _sc_scatter_rows(h_pk.reshape(PACK_SPLIT * t, LANES), slot4, PACK_SPLIT * n_slots)
    ys = _expert_matmul(xs.reshape(PACK_SPLIT, n_slots, LANES), blk_expert, n_active, blk_valid,
                        w_gate, w_up, w_down)
    yg = _sc_gather_rows(ys.reshape(PACK_SPLIT * n_slots, LANES), slot4)
    return _combine_ln(h, yg.reshape(PACK_SPLIT, TOP_K, t, LANES), gates.T, sh_gate, sh_up, sh_down, g, b)


def kernel(x, l0_w_in, l0_w_out, l0_na_rpb, l0_diff_lq1, l0_diff_lk1, l0_diff_lq2, l0_diff_lk2, l0_diff_subln_g, l0_ln1_g, l0_ln1_b, l0_router_w, l0_router_bias, l0_expert_w_gate, l0_expert_w_up, l0_expert_w_down, l0_shared_w_gate, l0_shared_w_up, l0_shared_w_down, l0_ln2_g, l0_ln2_b, l1_w_in, l1_w_out, l1_swa_sinks, l1_ln1_g, l1_ln1_b, l1_router_w, l1_router_bias, l1_expert_w_gate, l1_expert_w_up, l1_expert_w_down, l1_shared_w_gate, l1_shared_w_up, l1_shared_w_down, l1_ln2_g, l1_ln2_b):
    batch, seq, d = x.shape
    t = batch * seq
    x2 = x.reshape(t, d).astype(F32)
    tab = _rope_table(seq)
    qscale = HEAD_DIM ** -0.5 * LOG2E
    na_w = NA_HEADS * HEAD_DIM
    dq_w = DIFF_HEADS * 2 * HEAD_DIM

    col_scale = jnp.concatenate([
        jnp.full((na_w,), qscale, F32), jnp.ones((2 * na_w,), F32),
        jnp.full((dq_w,), qscale, F32), jnp.ones((2 * dq_w,), F32)])
    w_in0 = (l0_w_in.astype(F32) * col_scale).astype(BF16)
    per = LANES
    rope0 = [0] * (3 * na_w // per) + [1] * (2 * dq_w // per) + [0] * (dq_w // per)
    hcat0 = _inproj(x2, w_in0, tab, rope0, seq)
    oa = _na_attention(hcat0, l0_na_rpb, batch, seq)
    lambda_init = 0.8 - 0.6 * math.exp(-0.3 * 0)
    lam = (jnp.exp(jnp.sum(l0_diff_lq1.astype(F32) * l0_diff_lk1.astype(F32)))
           - jnp.exp(jnp.sum(l0_diff_lq2.astype(F32) * l0_diff_lk2.astype(F32))) + lambda_init)
    od = _diff_attention(hcat0, lam.reshape(1).astype(F32), l0_diff_subln_g, lambda_init, batch, seq)
    h, h_pk = _outproj_ln([oa, od], l0_w_out.astype(BF16), x2, l0_ln1_g, l0_ln1_b)
    x2 = _moe_layer(h, h_pk, l0_router_w, l0_router_bias, l0_expert_w_gate, l0_expert_w_up,
                    l0_expert_w_down, l0_shared_w_gate, l0_shared_w_up, l0_shared_w_down,
                    l0_ln2_g, l0_ln2_b)

    q_w = SWA_Q_HEADS * HEAD_DIM
    kv_w = SWA_KV_HEADS * HEAD_DIM
    w1 = l1_w_in.astype(F32)
    wq = w1[:, :q_w] * qscale
    wk = w1[:, q_w:q_w + kv_w].reshape(d, SWA_KV_HEADS, HEAD_DIM)
    wv = w1[:, q_w + kv_w:].reshape(d, SWA_KV_HEADS, HEAD_DIM)
    wkv = jnp.concatenate([wk, wv], axis=-1).reshape(d, 2 * kv_w)
    w_in1 = jnp.concatenate([wq, wkv], axis=1).astype(BF16)
    rope1 = [1] * (q_w // per) + [2] * (2 * kv_w // per)
    hcat1 = _inproj(x2, w_in1, tab, rope1, seq)
    o1 = _swa_attention(hcat1, l1_swa_sinks, batch, seq)
    h, h_pk = _outproj_ln([o1], l1_w_out.astype(BF16), x2, l1_ln1_g, l1_ln1_b)
    x2 = _moe_layer(h, h_pk, l1_router_w, l1_router_bias, l1_expert_w_gate, l1_expert_w_up,
                    l1_expert_w_down, l1_shared_w_gate, l1_shared_w_up, l1_shared_w_down,
                    l1_ln2_g, l1_ln2_b)
    return x2.reshape(batch, seq, d).astype(x.dtype)
```

```python
import functools
import math

import numpy as np
import jax
import jax.numpy as jnp
from jax import lax
from jax.experimental import pallas as pl
from jax.experimental.pallas import tpu as pltpu
from jax.experimental.pallas import tpu_sc as plsc

F32 = jnp.float32
BF16 = jnp.bfloat16
I32 = jnp.int32

HEAD_DIM = 64
GRID_W = 64
NA_HEADS = 8
NA_WIN_ROWS = 8
NA_WIN_COLS = 16
DIFF_HEADS = 4
SWA_Q_HEADS = 16
SWA_KV_HEADS = 4
SWA_WINDOW = 128
SWA_BLOCK = 128
ROPE_THETA = 10000.0
N_EXPERTS = 256
TOP_K = 8
N_GROUPS = 8
TOPK_GROUPS = 4
ROUTED_SCALE = 2.5
LN_EPS = 1e-5
DEPTH = 2
DN_ALPHA = (2 * DEPTH) ** 0.25

LOG2E = 1.4426950408889634
NEG_BIG = -1e30
LANES = 128
NA_KEY_ROWS = NA_WIN_ROWS + 1
SWA_SUB = 4
NA_SUB = 4
EXPERT_BLOCK = 512
EXPERT_CHAINS = 2
DIFF_ONES_ROWS = 16
PACK_SPLIT = 4
SC_CORES = 2
SC_SUBCORES = 16
SC_WINDOW = 128
VMEM_LIMIT = 56 * 1024 * 1024


def _cparams(sem):
    return pltpu.CompilerParams(dimension_semantics=sem, vmem_limit_bytes=VMEM_LIMIT)


def _silu(x):
    return x / (1.0 + jnp.exp(-x))


def _pack_rows(y):
    half = y.shape[1] // 2
    bits = pltpu.bitcast(y.astype(BF16).astype(F32), I32)
    lo = lax.shift_right_logical(bits[:, :half], 16)
    hi = bits[:, half:] & jnp.int32(-65536)
    return hi | lo


def _unpack_words(w):
    lo = pltpu.bitcast(lax.shift_left(w, 16), F32)
    hi = pltpu.bitcast(w & jnp.int32(-65536), F32)
    return lo, hi


def _unpack_rows(ref_at):
    los, his = [], []
    for c in range(PACK_SPLIT):
        lo, hi = _unpack_words(ref_at(c))
        los.append(lo)
        his.append(hi)
    return jnp.concatenate(los + his, axis=1)


def _layer_norm(y, g, b):
    mu = jnp.mean(y, axis=-1, keepdims=True)
    yc = y - mu
    var = jnp.mean(yc * yc, axis=-1, keepdims=True)
    return yc * lax.rsqrt(var + LN_EPS) * g + b


def _inproj_body(x_ref, w_ref, tab_ref, o_ref, *, chunk, rope_kind):
    tm = x_ref.shape[0]
    n_out = w_ref.shape[1]
    x = x_ref[...].astype(BF16)
    lane = lax.broadcasted_iota(I32, (tm, LANES), 1)
    first_half = (lane % HEAD_DIM) < (HEAD_DIM // 2)
    per = chunk // LANES
    for c in range(n_out // chunk):
        h = jnp.dot(x, w_ref[:, c * chunk:(c + 1) * chunk], preferred_element_type=F32)
        kinds = rope_kind[c * per:(c + 1) * per]
        if any(kinds):
            pieces = []
            for j, kind in enumerate(kinds):
                t = h[:, j * LANES:(j + 1) * LANES]
                if kind:
                    base = (kind - 1) * 2 * LANES
                    cos = tab_ref[:, base:base + LANES]
                    sin = tab_ref[:, base + LANES:base + 2 * LANES]
                    rot = jnp.where(first_half, pltpu.roll(t, LANES - HEAD_DIM // 2, 1),
                                    pltpu.roll(t, HEAD_DIM // 2, 1))
                    t = t * cos + rot * sin
                pieces.append(t)
            h = jnp.concatenate(pieces, axis=1)
        o_ref[:, c * chunk:(c + 1) * chunk] = h.astype(o_ref.dtype)


def _inproj(x2, w_bf, tab, rope_kind, seq, tm=512, chunk=256):
    t, d = x2.shape
    n_out = w_bf.shape[1]
    tm = min(tm, seq)
    sb = seq // tm
    return pl.pallas_call(
        functools.partial(_inproj_body, chunk=chunk, rope_kind=tuple(rope_kind)),
        grid=(t // tm,),
        in_specs=[pl.BlockSpec((tm, d), lambda i: (i, 0)),
                  pl.BlockSpec((d, n_out), lambda i: (0, 0)),
                  pl.BlockSpec((tm, tab.shape[1]), lambda i: (i % sb, 0))],
        out_specs=pl.BlockSpec((tm, n_out), lambda i: (i, 0)),
        out_shape=jax.ShapeDtypeStruct((t, n_out), BF16),
        compiler_params=_cparams(("parallel",)),
        name="inproj",
    )(x2, w_bf, tab)


def _rope_table(seq):
    half = HEAD_DIM // 2
    inv = 1.0 / (ROPE_THETA ** (jnp.arange(0, HEAD_DIM, 2, dtype=F32) / HEAD_DIM))
    ang = jnp.arange(seq, dtype=F32)[:, None] * inv[None, :]
    cos = jnp.cos(ang)
    sin = jnp.sin(ang)
    cos64 = jnp.concatenate([cos, cos], axis=-1)
    sin64 = jnp.concatenate([-sin, sin], axis=-1)
    one = jnp.ones((seq, HEAD_DIM), F32)
    zero = jnp.zeros((seq, HEAD_DIM), F32)
    del half
    return jnp.concatenate([cos64, cos64, sin64, sin64, cos64, one, sin64, zero], axis=-1)


def _na_body(pat_ref, q_ref, k_ref, v_ref, *rest, rows_n):
    del pat_ref
    bias_refs, o_ref = rest[:NA_SUB], rest[NA_SUB]
    nq = q_ref.shape[0] // NA_SUB
    nk = NA_KEY_ROWS * GRID_W
    low = lax.broadcasted_iota(I32, (nq, LANES), 1) < HEAD_DIM
    scores, vwins = [], []
    for sub in range(NA_SUB):
        r = NA_SUB * pl.program_id(2) + sub
        ks = jnp.clip(2 * r - NA_WIN_ROWS // 2, 0, rows_n - NA_KEY_ROWS)
        start = pl.multiple_of(ks * GRID_W, GRID_W)
        q = q_ref[sub * nq:(sub + 1) * nq, :].astype(F32)
        qm = jnp.concatenate([jnp.where(low, q, 0.0), jnp.where(low, 0.0, q)], axis=0).astype(BF16)
        s = lax.dot_general(qm, k_ref[pl.ds(start, nk), :], (((1,), (1,)), ((), ())),
                            preferred_element_type=F32)
        scores.append(s + bias_refs[sub][0].reshape(2 * nq, nk))
        vwins.append(v_ref[pl.ds(start, nk), :])
    for sub in range(NA_SUB):
        s = scores[sub]
        m = jnp.max(s, axis=-1, keepdims=True)
        p = jnp.exp2(s - m)
        l = jnp.sum(p, axis=-1, keepdims=True)
        pv = jnp.dot(p.astype(BF16), vwins[sub], preferred_element_type=F32) / l
        o_ref[sub * nq:(sub + 1) * nq, :] = jnp.where(low, pv[:nq], pv[nq:]).astype(o_ref.dtype)


def _na_bias(rpb, rows_n):
    half = NA_WIN_ROWS // 2
    pats, pat_id = [], []
    for blk in range(rows_n // 2):
        r0 = 2 * blk
        ks = min(max(r0 - half, 0), rows_n - NA_KEY_ROWS)
        starts = tuple(min(max(r0 + i - half, 0), rows_n - NA_WIN_ROWS) - ks for i in range(2))
        key = (r0 - ks, starts)
        if key not in pats:
            pats.append(key)
        pat_id.append(pats.index(key))
    wq = np.arange(GRID_W)
    wk = np.arange(GRID_W)
    col_start = np.clip(wq - NA_WIN_COLS // 2, 0, GRID_W - NA_WIN_COLS)
    col_off = wk[None, :] - col_start[:, None]
    col_valid = (col_off >= 0) & (col_off < NA_WIN_COLS)
    dc = np.clip(wk[None, :] - wq[:, None], -(NA_WIN_COLS - 1), NA_WIN_COLS - 1) + (NA_WIN_COLS - 1)
    n_dc = 2 * NA_WIN_COLS - 1
    onehot = (dc.reshape(-1)[None, :] == np.arange(n_dc)[:, None]).astype(np.float32)
    heads, n_dr = rpb.shape[0], rpb.shape[1]
    col = jnp.dot(rpb.astype(F32).reshape(heads * n_dr, n_dc), jnp.asarray(onehot),
                  precision=lax.Precision.HIGHEST).reshape(heads, n_dr, GRID_W, GRID_W)
    col = jnp.where(col_valid[None, None], col * LOG2E, NEG_BIG)
    masked_blk = jnp.full((heads, GRID_W, GRID_W), NEG_BIG, F32)
    tables = []
    for r0rel, starts in pats:
        qrows = []
        for qi in range(2):
            blks = []
            for kr in range(NA_KEY_ROWS):
                row_ok = starts[qi] <= kr < starts[qi] + NA_WIN_ROWS
                dr = kr - (r0rel + qi) + (NA_WIN_ROWS - 1)
                blks.append(col[:, dr] if row_ok else masked_blk)
            qrows.append(jnp.concatenate(blks, axis=-1))
        tables.append(jnp.concatenate(qrows, axis=1))
    return jnp.stack(tables, axis=0), np.asarray(pat_id, np.int32)


def _na_attention(hcat, rpb, batch, seq):
    t = hcat.shape[0]
    rows_n = seq // GRID_W
    nblk = rows_n // 2
    nq = 2 * GRID_W
    bias, pat_id = _na_bias(rpb, rows_n)
    nk = NA_KEY_ROWS * GRID_W
    hp = NA_HEADS // 2
    nstep = nblk // NA_SUB

    def bias_spec(sub):
        return pl.BlockSpec((1, 2, nq, nk), lambda b, h, r, pat: (pat[NA_SUB * r + sub], h, 0, 0))

    grid_spec = pltpu.PrefetchScalarGridSpec(
        num_scalar_prefetch=1,
        grid=(batch, hp, nstep),
        in_specs=[
            pl.BlockSpec((NA_SUB * nq, LANES), lambda b, h, r, pat: (b * nstep + r, h)),
            pl.BlockSpec((seq, LANES), lambda b, h, r, pat: (b, hp + h)),
            pl.BlockSpec((seq, LANES), lambda b, h, r, pat: (b, 2 * hp + h)),
        ] + [bias_spec(sub) for sub in range(NA_SUB)],
        out_specs=pl.BlockSpec((NA_SUB * nq, LANES), lambda b, h, r, pat: (b * nstep + r, h)),
    )
    return pl.pallas_call(
        functools.partial(_na_body, rows_n=rows_n),
        grid_spec=grid_spec,
        out_shape=jax.ShapeDtypeStruct((t, NA_HEADS * HEAD_DIM), BF16),
        compiler_params=_cparams(("parallel", "parallel", "arbitrary")),
        name="na_attn",
    )(jnp.asarray(pat_id), hcat, hcat, hcat, *([bias] * NA_SUB))


def _diff_body(lam_ref, q_ref, k_ref, v_ref, g_ref, o_ref, vt_ref, acc_ref, sta_ref, stb_ref, pa_ref, pb_ref,
               *, tk, lambda_init):
    j = pl.program_id(2)
    tq = q_ref.shape[0]
    seq = k_ref.shape[0]
    nkv = seq // tk

    @pl.when(j == 0)
    def _():
        ones = jnp.ones((DIFF_ONES_ROWS, tk), BF16)
        for c in range(nkv):
            vt_ref[c, :LANES, :] = v_ref[c * tk:(c + 1) * tk, :].astype(F32).T.astype(BF16)
            vt_ref[c, LANES:, :] = ones

    q = q_ref[...].astype(F32)
    lane = lax.broadcasted_iota(I32, (tq, LANES), 1)
    q1 = jnp.where(lane < HEAD_DIM, q, 0.0).T
    q2 = jnp.where(lane >= HEAD_DIM, q, 0.0).T
    rhs = jnp.concatenate([q1, q2], axis=1).astype(BF16)

    def scores(i, dst_ref):
        start = pl.multiple_of(i * tk, tk)
        dst_ref[...] = jnp.dot(k_ref[pl.ds(start, tk), :], rhs, preferred_element_type=F32)

    def softmax(src_ref, dst_ref, m):
        st = src_ref[...]
        m_new = jnp.maximum(m, jnp.max(st, axis=0, keepdims=True))
        dst_ref[...] = jnp.exp2((st - m_new).astype(BF16))
        return m_new, jnp.exp2(m - m_new)

    def weighted_values(i, p_ref, alpha):
        pv = jnp.dot(vt_ref[i], p_ref[...], preferred_element_type=F32)
        acc_ref[...] = alpha * acc_ref[...] + pv

    acc_ref[...] = jnp.zeros_like(acc_ref)
    scores(0, sta_ref)
    scores(1, stb_ref)
    m, a0 = softmax(sta_ref, pa_ref, jnp.full((1, 2 * tq), NEG_BIG, F32))

    def pair(ii, carry):
        m, a0 = carry
        i = 2 * ii
        scores(i + 2, sta_ref)
        m, a1 = softmax(stb_ref, pb_ref, m)
        weighted_values(i, pa_ref, a0)
        scores(i + 3, stb_ref)
        m, a0 = softmax(sta_ref, pa_ref, m)
        weighted_values(i + 1, pb_ref, a1)
        return m, a0

    m, a0 = lax.fori_loop(0, (nkv - 2) // 2, pair, (m, a0))
    m, a1 = softmax(stb_ref, pb_ref, m)
    weighted_values(nkv - 2, pa_ref, a0)
    weighted_values(nkv - 1, pb_ref, a1)
    acc = acc_ref[...]
    ot = acc[:LANES, :] / acc[LANES:LANES + 1, :]
    dt = ot[:, :tq] - lam_ref[0] * ot[:, tq:]
    ms = jnp.mean(dt * dt, axis=0, keepdims=True)
    y = dt * lax.rsqrt(ms + LN_EPS) * g_ref[...] * (1.0 - lambda_init)
    o_ref[...] = y.T.astype(o_ref.dtype)


def _diff_attention(hcat, lam, subln_g, lambda_init, batch, seq, tq=1024, tk=512):
    t = hcat.shape[0]
    tq = min(tq, seq)
    tk = min(tk, seq)
    nq = seq // tq
    qoff = 3 * NA_HEADS * HEAD_DIM // LANES
    koff = qoff + DIFF_HEADS
    voff = koff + DIFF_HEADS
    grid_spec = pltpu.PrefetchScalarGridSpec(
        num_scalar_prefetch=1,
        grid=(batch, DIFF_HEADS, nq),
        in_specs=[
            pl.BlockSpec((tq, LANES), lambda b, h, j, lam: (b * nq + j, qoff + h)),
            pl.BlockSpec((seq, LANES), lambda b, h, j, lam: (b, koff + h)),
            pl.BlockSpec((seq, LANES), lambda b, h, j, lam: (b, voff + h)),
            pl.BlockSpec((LANES, 1), lambda b, h, j, lam: (0, 0)),
        ],
        out_specs=pl.BlockSpec((tq, LANES), lambda b, h, j, lam: (b * nq + j, h)),
        scratch_shapes=[pltpu.VMEM((seq // tk, LANES + DIFF_ONES_ROWS, tk), BF16),
                        pltpu.VMEM((LANES + DIFF_ONES_ROWS, 2 * tq), F32),
                        pltpu.VMEM((tk, 2 * tq), F32),
                        pltpu.VMEM((tk, 2 * tq), F32),
                        pltpu.VMEM((tk, 2 * tq), BF16),
                        pltpu.VMEM((tk, 2 * tq), BF16)],
    )
    return pl.pallas_call(
        functools.partial(_diff_body, tk=tk, lambda_init=lambda_init),
        grid_spec=grid_spec,
        out_shape=jax.ShapeDtypeStruct((t, DIFF_HEADS * 2 * HEAD_DIM), BF16),
        compiler_params=_cparams(("parallel", "parallel", "arbitrary")),
        name="diff_attn",
    )(lam, hcat, hcat, hcat, subln_g.astype(F32).reshape(LANES, 1))


def _swa_body(sink_ref, q_ref, kv_ref, o_ref, kvt_ref):
    kvh = pl.program_id(1)
    nq = SWA_BLOCK
    nsub = q_ref.shape[0] // nq
    seq = kv_ref.shape[0]
    nblk = seq // SWA_BLOCK
    wb = min(3, nblk)
    nk = wb * SWA_BLOCK

    @pl.when(pl.program_id(2) == 0)
    def _():
        for c in range(nblk):
            kvt_ref[c] = kv_ref[c * SWA_BLOCK:(c + 1) * SWA_BLOCK, :].astype(F32).T.astype(BF16)

    group = SWA_Q_HEADS // SWA_KV_HEADS
    half = LANES // 2

    def swap_halves(a):
        return jnp.concatenate([a[half:], a[:half]], axis=0)

    top = lax.broadcasted_iota(I32, (LANES, nq), 0) < half
    sink = jnp.concatenate([jnp.full((1, nq), sink_ref[kvh * group + g], F32) for g in range(group)], axis=1)

    scores, kvts = [], []
    for sub in range(nsub):
        n = nsub * pl.program_id(2) + sub
        b0 = jnp.clip(n - 1, 0, nblk - wb)
        start = pl.multiple_of(b0 * SWA_BLOCK, SWA_BLOCK)
        cols = []
        for c in range(group // 2):
            qt = q_ref[sub * nq:(sub + 1) * nq, c * LANES:(c + 1) * LANES].astype(F32).T
            cols.append(jnp.where(top, qt, 0.0))
            cols.append(jnp.where(top, swap_halves(qt), 0.0))
        rhs = jnp.concatenate(cols, axis=1).astype(BF16)
        st = jnp.dot(kv_ref[pl.ds(start, nk), :], rhs, preferred_element_type=F32)
        kpos = start + lax.broadcasted_iota(I32, (nk, nq), 0)
        qpos = n * SWA_BLOCK + lax.broadcasted_iota(I32, (nk, nq), 1)
        mask = jnp.where(jnp.abs(kpos - qpos) <= SWA_WINDOW, 0.0, NEG_BIG)
        scores.append(st + jnp.concatenate([mask] * group, axis=1))
        kvts.append(jnp.concatenate([kvt_ref[b0 + w] for w in range(wb)], axis=1))
    for sub in range(nsub):
        st = scores[sub]
        m = jnp.maximum(jnp.max(st, axis=0, keepdims=True), sink)
        e = jnp.exp2(st - m)
        den = jnp.sum(e, axis=0, keepdims=True) + jnp.exp2(sink - m)
        ot = jnp.dot(kvts[sub], e.astype(BF16), preferred_element_type=F32) / den
        for c in range(group // 2):
            even = ot[:, 2 * c * nq:(2 * c + 1) * nq]
            odd = ot[:, (2 * c + 1) * nq:(2 * c + 2) * nq]
            blk = jnp.where(top, swap_halves(even), odd)
            o_ref[sub * nq:(sub + 1) * nq, c * LANES:(c + 1) * LANES] = blk.T.astype(o_ref.dtype)


def _swa_attention(hcat, sinks, batch, seq):
    t = hcat.shape[0]
    nb = seq // SWA_BLOCK
    group = SWA_Q_HEADS // SWA_KV_HEADS
    qw = group * HEAD_DIM
    kvoff = SWA_Q_HEADS * HEAD_DIM // LANES
    nsub = min(SWA_SUB, nb)
    nstep = nb // nsub
    grid_spec = pltpu.PrefetchScalarGridSpec(
        num_scalar_prefetch=1,
        grid=(batch, SWA_KV_HEADS, nstep),
        in_specs=[
            pl.BlockSpec((nsub * SWA_BLOCK, qw), lambda b, h, n, s: (b * nstep + n, h)),
            pl.BlockSpec((seq, LANES), lambda b, h, n, s: (b, kvoff + h)),
        ],
        out_specs=pl.BlockSpec((nsub * SWA_BLOCK, qw), lambda b, h, n, s: (b * nstep + n, h)),
        scratch_shapes=[pltpu.VMEM((nb, LANES, SWA_BLOCK), BF16)],
    )
    return pl.pallas_call(
        _swa_body,
        grid_spec=grid_spec,
        out_shape=jax.ShapeDtypeStruct((t, SWA_Q_HEADS * HEAD_DIM), BF16),
        compiler_params=_cparams(("parallel", "parallel", "arbitrary")),
        name="swa_attn",
    )(sinks.astype(F32) * LOG2E, hcat, hcat)


def _outproj_body(*refs, n_in):
    a_refs = refs[:n_in]
    w_ref, x_ref, g_ref, b_ref, o_ref, opk_ref = refs[n_in:]
    acc = None
    off = 0
    for a_ref in a_refs:
        ka = a_ref.shape[1]
        d = jnp.dot(a_ref[...], w_ref[off:off + ka, :], preferred_element_type=F32)
        acc = d if acc is None else acc + d
        off += ka
    y = DN_ALPHA * x_ref[...] + acc
    out = _layer_norm(y, g_ref[...], b_ref[...])
    o_ref[...] = out
    packed = _pack_rows(out)
    for c in range(PACK_SPLIT):
        opk_ref[c] = packed[:, c * LANES:(c + 1) * LANES]


def _outproj_ln(acts, w_bf, x2, g, b, tm=512):
    t, d = x2.shape
    tm = min(tm, t)
    in_specs = [pl.BlockSpec((tm, a.shape[1]), lambda i: (i, 0)) for a in acts]
    in_specs += [pl.BlockSpec(w_bf.shape, lambda i: (0, 0)),
                 pl.BlockSpec((tm, d), lambda i: (i, 0)),
                 pl.BlockSpec((1, d), lambda i: (0, 0)),
                 pl.BlockSpec((1, d), lambda i: (0, 0))]
    return pl.pallas_call(
        functools.partial(_outproj_body, n_in=len(acts)),
        grid=(t // tm,),
        in_specs=in_specs,
        out_specs=[pl.BlockSpec((tm, d), lambda i: (i, 0)),
                   pl.BlockSpec((PACK_SPLIT, tm, LANES), lambda i: (0, i, 0))],
        out_shape=[jax.ShapeDtypeStruct((t, d), F32), jax.ShapeDtypeStruct((PACK_SPLIT, t, LANES), I32)],
        compiler_params=_cparams(("parallel",)),
        name="outproj_ln",
    )(*acts, w_bf, x2, g.astype(F32).reshape(1, d), b.astype(F32).reshape(1, d))


def _router_body(h_ref, whi_ref, wlo_ref, bias_ref, idx_ref, gate_ref, pos_ref, cnt_ref, base_ref):
    i = pl.program_id(0)
    tm = h_ref.shape[0]
    gsz = N_EXPERTS // N_GROUPS

    @pl.when(i == 0)
    def _():
        base_ref[...] = jnp.zeros_like(base_ref)

    h = h_ref[...]
    h_hi = h.astype(BF16)
    h_lo = (h - h_hi.astype(F32)).astype(BF16)
    dn = (((1,), (1,)), ((), ()))
    whi = whi_ref[...]
    logits = (lax.dot_general(whi, h_hi, dn, preferred_element_type=F32)
              + lax.dot_general(whi, h_lo, dn, preferred_element_type=F32)
              + lax.dot_general(wlo_ref[...], h_hi, dn, preferred_element_type=F32))
    scores = 1.0 / (1.0 + jnp.exp(-logits))
    choice = scores + bias_ref[...]

    iota_g = lax.broadcasted_iota(I32, (gsz, tm), 0)
    gscore = []
    for g in range(N_GROUPS):
        cg = choice[g * gsz:(g + 1) * gsz, :]
        m1 = jnp.max(cg, axis=0, keepdims=True)
        first = jnp.min(jnp.where(cg == m1, iota_g, gsz), axis=0, keepdims=True)
        m2 = jnp.max(jnp.where(iota_g == first, -jnp.inf, cg), axis=0, keepdims=True)
        gscore.append(m1 + m2)
    pieces = []
    for g in range(N_GROUPS):
        rank = jnp.zeros((1, tm), I32)
        for o in range(N_GROUPS):
            if o == g:
                continue
            beats = (gscore[o] > gscore[g]) if o > g else (gscore[o] >= gscore[g])
            rank = rank + beats.astype(I32)
        keep = rank < TOPK_GROUPS
        pieces.append(jnp.where(keep, choice[g * gsz:(g + 1) * gsz, :], -jnp.inf))
    masked = jnp.concatenate(pieces, axis=0)

    iota_e = lax.broadcasted_iota(I32, (N_EXPERTS, tm), 0)
    sel_all = jnp.zeros((N_EXPERTS, tm), F32)
    idxs, gates, sels = [], [], []
    for _ in range(TOP_K):
        mx = jnp.max(masked, axis=0, keepdims=True)
        idx = jnp.min(jnp.where(masked == mx, iota_e, N_EXPERTS), axis=0, keepdims=True)
        sel = iota_e == idx
        gates.append(jnp.sum(jnp.where(sel, scores, 0.0), axis=0, keepdims=True))
        masked = jnp.where(sel, -jnp.inf, masked)
        sel_all = sel_all + sel.astype(F32)
        idxs.append(idx)
        sels.append(sel)
    gsum = gates[0]
    for gk in gates[1:]:
        gsum = gsum + gk
    gate_ref[...] = jnp.concatenate(gates, axis=0) / gsum * ROUTED_SCALE
    idx_ref[...] = jnp.concatenate(idxs, axis=0)

    tri = (lax.broadcasted_iota(I32, (tm, tm), 0) < lax.broadcasted_iota(I32, (tm, tm), 1))
    cum = jnp.dot(sel_all.astype(BF16), tri.astype(F32).astype(BF16), preferred_element_type=F32)
    tot = cum + base_ref[...]
    pos = [jnp.sum(jnp.where(sel, tot, 0.0), axis=0, keepdims=True) for sel in sels]
    pos_ref[...] = jnp.concatenate(pos, axis=0).astype(I32)
    base_ref[...] = base_ref[...] + jnp.sum(sel_all, axis=1, keepdims=True)
    cnt_ref[...] = base_ref[...]


def _router(h, router_w, router_bias, tm=512):
    t, d = h.shape
    tm = min(tm, t)
    wt = router_w.astype(F32).T
    whi = wt.astype(BF16)
    wlo = (wt - whi.astype(F32)).astype(BF16)
    return pl.pallas_call(
        _router_body,
        grid=(t // tm,),
        in_specs=[pl.BlockSpec((tm, d), lambda i: (i, 0)),
                  pl.BlockSpec((N_EXPERTS, d), lambda i: (0, 0)),
                  pl.BlockSpec((N_EXPERTS, d), lambda i: (0, 0)),
                  pl.BlockSpec((N_EXPERTS, 1), lambda i: (0, 0))],
        out_specs=[pl.BlockSpec((TOP_K, tm), lambda i: (0, i)),
                   pl.BlockSpec((TOP_K, tm), lambda i: (0, i)),
                   pl.BlockSpec((TOP_K, tm), lambda i: (0, i)),
                   pl.BlockSpec((N_EXPERTS, 1), lambda i: (0, 0))],
        out_shape=[jax.ShapeDtypeStruct((TOP_K, t), I32),
                   jax.ShapeDtypeStruct((TOP_K, t), F32),
                   jax.ShapeDtypeStruct((TOP_K, t), I32),
                   jax.ShapeDtypeStruct((N_EXPERTS, 1), F32)],
        scratch_shapes=[pltpu.VMEM((N_EXPERTS, 1), F32)],
        compiler_params=_cparams(("arbitrary",)),
        name="router",
    )(h, whi, wlo, router_bias.astype(F32).reshape(N_EXPERTS, 1))


def _expert_body(be_ref, na_ref, bv_ref, x_ref, wg_ref, wu_ref, wd_ref, y_ref, wgu, wdb):
    i = pl.program_id(0)
    bm = x_ref.shape[1]
    ff = wd_ref.shape[1]
    hm = bm // EXPERT_CHAINS

    @pl.when(i < na_ref[0])
    def _():
        prev = be_ref[jnp.maximum(i - 1, 0)]

        @pl.when((i == 0) | (be_ref[i] != prev))
        def _():
            wgu[:, :ff] = wg_ref[0].astype(BF16)
            wgu[:, ff:] = wu_ref[0].astype(BF16)
            wdb[...] = wd_ref[0].astype(BF16)

        valid = bv_ref[i]
        row = lax.broadcasted_iota(I32, (hm, LANES), 0)
        def run(n_chains):
            gus = []
            for hb in range(n_chains):
                rows = slice(hb * hm, (hb + 1) * hm)
                keep = row < valid - hb * hm
                x = _unpack_rows(lambda c: jnp.where(keep, x_ref[c, rows, :], 0)).astype(BF16)
                gus.append(jnp.dot(x, wgu[...], preferred_element_type=F32))
            ys = []
            for gu in gus:
                hmid = (_silu(gu[:, :ff]) * gu[:, ff:]).astype(BF16)
                ys.append(jnp.dot(hmid, wdb[...], preferred_element_type=F32))
            for hb, y in enumerate(ys):
                packed = _pack_rows(y)
                for c in range(PACK_SPLIT):
                    y_ref[c, hb * hm:(hb + 1) * hm, :] = packed[:, c * LANES:(c + 1) * LANES]
            if n_chains < EXPERT_CHAINS:
                y_ref[:, n_chains * hm:, :] = jnp.zeros((PACK_SPLIT, bm - n_chains * hm, LANES), I32)

        for n_chains in range(1, EXPERT_CHAINS + 1):
            lo = (n_chains - 1) * hm
            cond = valid > lo if n_chains == EXPERT_CHAINS else (valid > lo) & (valid <= lo + hm)
            pl.when(cond)(functools.partial(run, n_chains))


def _expert_matmul(xs, blk_expert, n_active, blk_valid, w_gate, w_up, w_down):
    n_slots = xs.shape[1]
    bm = EXPERT_BLOCK
    n_blocks = n_slots // bm
    d, ff = w_gate.shape[1], w_gate.shape[2]

    def row_map(i, be, na, bv):
        return (0, jnp.minimum(i, na[0] - 1), 0)

    def w_map(i, be, na, bv):
        return (be[jnp.minimum(i, na[0] - 1)], 0, 0)

    grid_spec = pltpu.PrefetchScalarGridSpec(
        num_scalar_prefetch=3,
        grid=(n_blocks,),
        in_specs=[pl.BlockSpec((PACK_SPLIT, bm, LANES), row_map),
                  pl.BlockSpec((1, d, ff), w_map),
                  pl.BlockSpec((1, d, ff), w_map),
                  pl.BlockSpec((1, ff, d), w_map)],
        out_specs=pl.BlockSpec((PACK_SPLIT, bm, LANES), row_map),
        scratch_shapes=[pltpu.VMEM((d, 2 * ff), BF16), pltpu.VMEM((ff, d), BF16)],
    )
    return pl.pallas_call(
        _expert_body,
        grid_spec=grid_spec,
        out_shape=jax.ShapeDtypeStruct((PACK_SPLIT, n_slots, LANES), I32),
        compiler_params=_cparams(("arbitrary",)),
        name="expert_mlp",
    )(blk_expert, n_active, blk_valid, xs, w_gate, w_up, w_down)


def _slot_body(idx_ref, pos_ref, ps_ref, slot_ref, *, n_slots):
    tm = idx_ref.shape[1]
    iota_e = lax.broadcasted_iota(I32, (N_EXPERTS, tm), 0)
    ps = ps_ref[...]
    rows = []
    for k in range(TOP_K):
        start = jnp.sum(jnp.where(iota_e == idx_ref[k:k + 1, :], ps, 0.0), axis=0, keepdims=True)
        rows.append(start.astype(I32) + pos_ref[k:k + 1, :])
    slot = jnp.concatenate(rows, axis=0)
    for c in range(PACK_SPLIT):
        slot_ref[c * TOP_K:(c + 1) * TOP_K, :] = slot + c * n_slots


def _slots(idx, pos, pad_start, n_slots, tm=512):
    t = idx.shape[1]
    tm = min(tm, t)
    return pl.pallas_call(
        functools.partial(_slot_body, n_slots=n_slots),
        grid=(t // tm,),
        in_specs=[pl.BlockSpec((TOP_K, tm), lambda i: (0, i)),
                  pl.BlockSpec((TOP_K, tm), lambda i: (0, i)),
                  pl.BlockSpec((N_EXPERTS, 1), lambda i: (0, 0))],
        out_specs=pl.BlockSpec((PACK_SPLIT * TOP_K, tm), lambda i: (0, i)),
        out_shape=jax.ShapeDtypeStruct((PACK_SPLIT * TOP_K, t), I32),
        compiler_params=_cparams(("parallel",)),
        name="slots",
    )(idx, pos, pad_start.astype(F32).reshape(N_EXPERTS, 1))


def _sc_mesh():
    return plsc.VectorSubcoreMesh(core_axis_name="core", subcore_axis_name="subcore",
                                  num_cores=SC_CORES, num_subcores=SC_SUBCORES)


def _sc_scatter_rows(rows, idx, n_out):
    t = idx.shape[1]
    nj = t // SC_WINDOW

    @functools.partial(pl.kernel, out_type=jax.ShapeDtypeStruct((n_out, LANES), I32),
                       mesh=_sc_mesh(), scratch_types=[], name="sc_dispatch")
    def run(rows_hbm, idx_hbm, out_hbm):
        def body(rows_vmem, idx_vmem):
            for k in range(TOP_K):
                pltpu.sync_copy(rows_vmem, out_hbm.at[idx_vmem.at[k]])

        pltpu.emit_pipeline(
            body,
            grid=(rows.shape[0] // SC_WINDOW,),
            in_specs=[pl.BlockSpec((SC_WINDOW, LANES), lambda s: (s, 0)),
                      pl.BlockSpec((TOP_K, SC_WINDOW), lambda s: (s // nj, s % nj))],
            out_specs=[],
            core_axis_name=("core", "subcore"),
            dimension_semantics=(pltpu.PARALLEL,),
        )(rows_hbm, idx_hbm)

    return run(rows, idx)


def _sc_gather_rows(src, idx):
    nr, t = idx.shape
    nj = t // SC_WINDOW

    @functools.partial(pl.kernel, out_type=jax.ShapeDtypeStruct((nr * t, LANES), I32),
                       mesh=_sc_mesh(), scratch_types=[], name="sc_combine")
    def run(src_hbm, idx_hbm, out_hbm):
        def body(idx_vmem, out_vmem):
            pltpu.sync_copy(src_hbm.at[idx_vmem.at[0]], out_vmem)

        pltpu.emit_pipeline(
            body,
            grid=(nr * nj,),
            in_specs=[pl.BlockSpec((1, SC_WINDOW), lambda s: (s // nj, s % nj))],
            out_specs=[pl.BlockSpec((SC_WINDOW, LANES), lambda s: (s, 0))],
            core_axis_name=("core", "subcore"),
            dimension_semantics=(pltpu.PARALLEL,),
        )(idx_hbm, out_hbm)

    return run(src, idx)


def _combine_body(h_ref, yg_ref, gt_ref, sg_ref, su_ref, sd_ref, g_ref, b_ref, o_ref):
    h = h_ref[...]
    hb = h.astype(BF16)
    a = jnp.dot(hb, sg_ref[...], preferred_element_type=F32)
    u = jnp.dot(hb, su_ref[...], preferred_element_type=F32)
    moe = jnp.dot((_silu(a) * u).astype(BF16), sd_ref[...], preferred_element_type=F32)
    gt = gt_ref[...]
    for k in range(TOP_K):
        moe = moe + gt[:, k:k + 1] * _unpack_rows(lambda c, k=k: yg_ref[c, k])
    o_ref[...] = _layer_norm(DN_ALPHA * h + moe, g_ref[...], b_ref[...])


def _combine_ln(h, yg, gates_t, sh_gate, sh_up, sh_down, g, b, tm=256):
    t, d = h.shape
    tm = min(tm, t)
    ff = sh_gate.shape[1]
    return pl.pallas_call(
        _combine_body,
        grid=(t // tm,),
        in_specs=[pl.BlockSpec((tm, d), lambda i: (i, 0)),
                  pl.BlockSpec((PACK_SPLIT, TOP_K, tm, LANES), lambda i: (0, 0, i, 0)),
                  pl.BlockSpec((tm, TOP_K), lambda i: (i, 0)),
                  pl.BlockSpec((d, ff), lambda i: (0, 0)),
                  pl.BlockSpec((d, ff), lambda i: (0, 0)),
                  pl.BlockSpec((ff, d), lambda i: (0, 0)),
                  pl.BlockSpec((1, d), lambda i: (0, 0)),
                  pl.BlockSpec((1, d), lambda i: (0, 0))],
        out_specs=pl.BlockSpec((tm, d), lambda i: (i, 0)),
        out_shape=jax.ShapeDtypeStruct((t, d), F32),
        compiler_params=_cparams(("parallel",)),
        name="combine_ln",
    )(h, yg, gates_t, sh_gate.astype(BF16), sh_up.astype(BF16), sh_down.astype(BF16),
      g.astype(F32).reshape(1, d), b.astype(F32).reshape(1, d))


def _moe_layer(h, h_pk, router_w, router_bias, w_gate, w_up, w_down, sh_gate, sh_up, sh_down, g, b):
    t, d = h.shape
    bm = EXPERT_BLOCK
    idx, gates, pos, cnt = _router(h, router_w, router_bias)
    counts = cnt[:, 0].astype(I32)
    padded = (counts + bm - 1) // bm * bm
    e_iota = jnp.arange(N_EXPERTS, dtype=I32)
    pad_end = jnp.sum(jnp.where(e_iota[None, :] <= e_iota[:, None], padded[None, :], 0), axis=1)
    pad_start = pad_end - padded
    n_blocks = t * TOP_K // bm + N_EXPERTS
    n_slots = n_blocks * bm
    blk_start = jnp.arange(n_blocks, dtype=I32) * bm
    blk_expert = jnp.minimum(
        jnp.sum((blk_start[:, None] >= pad_end[None, :]).astype(I32), axis=1), N_EXPERTS - 1)
    real_end = jnp.sum(jnp.where(e_iota[None, :] == blk_expert[:, None], (pad_start + counts)[None, :], 0), axis=1)
    blk_valid = jnp.clip(real_end - blk_start, 0, bm).astype(I32)
    n_active = (pad_end[-1:] // bm).astype(I32)

    slot4 = _slots(idx, pos, pad_start, n_slots)
    xs = _sc_scatter_rows(h_pk.reshape(PACK_SPLIT * t, LANES), slot4, PACK_SPLIT * n_slots)
    ys = _expert_matmul(xs.reshape(PACK_SPLIT, n_slots, LANES), blk_expert, n_active, blk_valid,
                        w_gate, w_up, w_down)
    yg = _sc_gather_rows(ys.reshape(PACK_SPLIT * n_slots, LANES), slot4)
    return _combine_ln(h, yg.reshape(PACK_SPLIT, TOP_K, t, LANES), gates.T, sh_gate, sh_up, sh_down, g, b)


def kernel(x, l0_w_in, l0_w_out, l0_na_rpb, l0_diff_lq1, l0_diff_lk1, l0_diff_lq2, l0_diff_lk2, l0_diff_subln_g, l0_ln1_g, l0_ln1_b, l0_router_w, l0_router_bias, l0_expert_w_gate, l0_expert_w_up, l0_expert_w_down, l0_shared_w_gate, l0_shared_w_up, l0_shared_w_down, l0_ln2_g, l0_ln2_b, l1_w_in, l1_w_out, l1_swa_sinks, l1_ln1_g, l1_ln1_b, l1_router_w, l1_router_bias, l1_expert_w_gate, l1_expert_w_up, l1_expert_w_down, l1_shared_w_gate, l1_shared_w_up, l1_shared_w_down, l1_ln2_g, l1_ln2_b):
    batch, seq, d = x.shape
    t = batch * seq
    x2 = x.reshape(t, d).astype(F32)
    tab = _rope_table(seq)
    qscale = HEAD_DIM ** -0.5 * LOG2E
    na_w = NA_HEADS * HEAD_DIM
    dq_w = DIFF_HEADS * 2 * HEAD_DIM

    col_scale = jnp.concatenate([
        jnp.full((na_w,), qscale, F32), jnp.ones((2 * na_w,), F32),
        jnp.full((dq_w,), qscale, F32), jnp.ones((2 * dq_w,), F32)])
    w_in0 = (l0_w_in.astype(F32) * col_scale).astype(BF16)
    per = LANES
    rope0 = [0] * (3 * na_w // per) + [1] * (2 * dq_w // per) + [0] * (dq_w // per)
    hcat0 = _inproj(x2, w_in0, tab, rope0, seq)
    oa = _na_attention(hcat0, l0_na_rpb, batch, seq)
    lambda_init = 0.8 - 0.6 * math.exp(-0.3 * 0)
    lam = (jnp.exp(jnp.sum(l0_diff_lq1.astype(F32) * l0_diff_lk1.astype(F32)))
           - jnp.exp(jnp.sum(l0_diff_lq2.astype(F32) * l0_diff_lk2.astype(F32))) + lambda_init)
    od = _diff_attention(hcat0, lam.reshape(1).astype(F32), l0_diff_subln_g, lambda_init, batch, seq)
    h, h_pk = _outproj_ln([oa, od], l0_w_out.astype(BF16), x2, l0_ln1_g, l0_ln1_b)
    x2 = _moe_layer(h, h_pk, l0_router_w, l0_router_bias, l0_expert_w_gate, l0_expert_w_up,
                    l0_expert_w_down, l0_shared_w_gate, l0_shared_w_up, l0_shared_w_down,
                    l0_ln2_g, l0_ln2_b)

    q_w = SWA_Q_HEADS * HEAD_DIM
    kv_w = SWA_KV_HEADS * HEAD_DIM
    w1 = l1_w_in.astype(F32)
    wq = w1[:, :q_w] * qscale
    wk = w1[:, q_w:q_w + kv_w].reshape(d, SWA_KV_HEADS, HEAD_DIM)
    wv = w1[:, q_w + kv_w:].reshape(d, SWA_KV_HEADS, HEAD_DIM)
    wkv = jnp.concatenate([wk, wv], axis=-1).reshape(d, 2 * kv_w)
    w_in1 = jnp.concatenate([wq, wkv], axis=1).astype(BF16)
    rope1 = [1] * (q_w // per) + [2] * (2 * kv_w // per)
    hcat1 = _inproj(x2, w_in1, tab, rope1, seq)
    o1 = _swa_attention(hcat1, l1_swa_sinks, batch, seq)
    h, h_pk = _outproj_ln([o1], l1_w_out.astype(BF16), x2, l1_ln1_g, l1_ln1_b)
    x2 = _moe_layer(h, h_pk, l1_router_w, l1_router_bias, l1_expert_w_gate, l1_expert_w_up,
                    l1_expert_w_down, l1_shared_w_gate, l1_shared_w_up, l1_shared_w_down,
                    l1_ln2_g, l1_ln2_b)
    return x2.reshape(batch, seq, d).astype(x.dtype)
```

```python
import functools
import math

import numpy as np
import jax
import jax.numpy as jnp
from jax import lax
from jax.experimental import pallas as pl
from jax.experimental.pallas import tpu as pltpu
from jax.experimental.pallas import tpu_sc as plsc

F32 = jnp.float32
BF16 = jnp.bfloat16
I32 = jnp.int32

HEAD_DIM = 64
GRID_W = 64
NA_HEADS = 8
NA_WIN_ROWS = 8
NA_WIN_COLS = 16
DIFF_HEADS = 4
SWA_Q_HEADS = 16
SWA_KV_HEADS = 4
SWA_WINDOW = 128
SWA_BLOCK = 128
ROPE_THETA = 10000.0
N_EXPERTS = 256
TOP_K = 8
N_GROUPS = 8
TOPK_GROUPS = 4
ROUTED_SCALE = 2.5
LN_EPS = 1e-5
DEPTH = 2
DN_ALPHA = (2 * DEPTH) ** 0.25

LOG2E = 1.4426950408889634
NEG_BIG = -1e30
LANES = 128
NA_KEY_ROWS = NA_WIN_ROWS + 1
SWA_SUB = 4
NA_SUB = 4
EXPERT_BLOCK = 512
EXPERT_CHAINS = 2
DIFF_ONES_ROWS = 16
PACK_SPLIT = 4
SC_CORES = 2
SC_SUBCORES = 16
SC_WINDOW = 128
VMEM_LIMIT = 56 * 1024 * 1024


def _cparams(sem):
    return pltpu.CompilerParams(dimension_semantics=sem, vmem_limit_bytes=VMEM_LIMIT)


def _silu(x):
    return x / (1.0 + jnp.exp(-x))


def _pack_rows(y):
    half = y.shape[1] // 2
    bits = pltpu.bitcast(y.astype(BF16).astype(F32), I32)
    lo = lax.shift_right_logical(bits[:, :half], 16)
    hi = bits[:, half:] & jnp.int32(-65536)
    return hi | lo


def _unpack_words(w):
    lo = pltpu.bitcast(lax.shift_left(w, 16), F32)
    hi = pltpu.bitcast(w & jnp.int32(-65536), F32)
    return lo, hi


def _unpack_rows(ref_at):
    los, his = [], []
    for c in range(PACK_SPLIT):
        lo, hi = _unpack_words(ref_at(c))
        los.append(lo)
        his.append(hi)
    return jnp.concatenate(los + his, axis=1)


def _layer_norm(y, g, b):
    mu = jnp.mean(y, axis=-1, keepdims=True)
    yc = y - mu
    var = jnp.mean(yc * yc, axis=-1, keepdims=True)
    return yc * lax.rsqrt(var + LN_EPS) * g + b


def _inproj_body(x_ref, w_ref, tab_ref, o_ref, *, chunk, rope_kind):
    tm = x_ref.shape[0]
    n_out = w_ref.shape[1]
    x = x_ref[...].astype(BF16)
    lane = lax.broadcasted_iota(I32, (tm, LANES), 1)
    first_half = (lane % HEAD_DIM) < (HEAD_DIM // 2)
    per = chunk // LANES
    for c in range(n_out // chunk):
        h = jnp.dot(x, w_ref[:, c * chunk:(c + 1) * chunk], preferred_element_type=F32)
        kinds = rope_kind[c * per:(c + 1) * per]
        if any(kinds):
            pieces = []
            for j, kind in enumerate(kinds):
                t = h[:, j * LANES:(j + 1) * LANES]
                if kind:
                    base = (kind - 1) * 2 * LANES
                    cos = tab_ref[:, base:base + LANES]
                    sin = tab_ref[:, base + LANES:base + 2 * LANES]
                    rot = jnp.where(first_half, pltpu.roll(t, LANES - HEAD_DIM // 2, 1),
                                    pltpu.roll(t, HEAD_DIM // 2, 1))
                    t = t * cos + rot * sin
                pieces.append(t)
            h = jnp.concatenate(pieces, axis=1)
        o_ref[:, c * chunk:(c + 1) * chunk] = h.astype(o_ref.dtype)


def _inproj(x2, w_bf, tab, rope_kind, seq, tm=512, chunk=256):
    t, d = x2.shape
    n_out = w_bf.shape[1]
    tm = min(tm, seq)
    sb = seq // tm
    return pl.pallas_call(
        functools.partial(_inproj_body, chunk=chunk, rope_kind=tuple(rope_kind)),
        grid=(t // tm,),
        in_specs=[pl.BlockSpec((tm, d), lambda i: (i, 0)),
                  pl.BlockSpec((d, n_out), lambda i: (0, 0)),
                  pl.BlockSpec((tm, tab.shape[1]), lambda i: (i % sb, 0))],
        out_specs=pl.BlockSpec((tm, n_out), lambda i: (i, 0)),
        out_shape=jax.ShapeDtypeStruct((t, n_out), BF16),
        compiler_params=_cparams(("parallel",)),
        name="inproj",
    )(x2, w_bf, tab)


def _rope_table(seq):
    half = HEAD_DIM // 2
    inv = 1.0 / (ROPE_THETA ** (jnp.arange(0, HEAD_DIM, 2, dtype=F32) / HEAD_DIM))
    ang = jnp.arange(seq, dtype=F32)[:, None] * inv[None, :]
    cos = jnp.cos(ang)
    sin = jnp.sin(ang)
    cos64 = jnp.concatenate([cos, cos], axis=-1)
    sin64 = jnp.concatenate([-sin, sin], axis=-1)
    one = jnp.ones((seq, HEAD_DIM), F32)
    zero = jnp.zeros((seq, HEAD_DIM), F32)
    del half
    return jnp.concatenate([cos64, cos64, sin64, sin64, cos64, one, sin64, zero], axis=-1)


def _na_body(pat_ref, q_ref, k_ref, v_ref, *rest, rows_n):
    del pat_ref
    bias_refs, o_ref = rest[:NA_SUB], rest[NA_SUB]
    nq = q_ref.shape[0] // NA_SUB
    nk = NA_KEY_ROWS * GRID_W
    low = lax.broadcasted_iota(I32, (nq, LANES), 1) < HEAD_DIM
    scores, vwins = [], []
    for sub in range(NA_SUB):
        r = NA_SUB * pl.program_id(2) + sub
        ks = jnp.clip(2 * r - NA_WIN_ROWS // 2, 0, rows_n - NA_KEY_ROWS)
        start = pl.multiple_of(ks * GRID_W, GRID_W)
        q = q_ref[sub * nq:(sub + 1) * nq, :].astype(F32)
        qm = jnp.concatenate([jnp.where(low, q, 0.0), jnp.where(low, 0.0, q)], axis=0).astype(BF16)
        s = lax.dot_general(qm, k_ref[pl.ds(start, nk), :], (((1,), (1,)), ((), ())),
                            preferred_element_type=F32)
        scores.append(s + bias_refs[sub][0].reshape(2 * nq, nk))
        vwins.append(v_ref[pl.ds(start, nk), :])
    for sub in range(NA_SUB):
        s = scores[sub]
        m = jnp.max(s, axis=-1, keepdims=True)
        p = jnp.exp2(s - m)
        l = jnp.sum(p, axis=-1, keepdims=True)
        pv = jnp.dot(p.astype(BF16), vwins[sub], preferred_element_type=F32) / l
        o_ref[sub * nq:(sub + 1) * nq, :] = jnp.where(low, pv[:nq], pv[nq:]).astype(o_ref.dtype)


def _na_bias(rpb, rows_n):
    half = NA_WIN_ROWS // 2
    pats, pat_id = [], []
    for blk in range(rows_n // 2):
        r0 = 2 * blk
        ks = min(max(r0 - half, 0), rows_n - NA_KEY_ROWS)
        starts = tuple(min(max(r0 + i - half, 0), rows_n - NA_WIN_ROWS) - ks for i in range(2))
        key = (r0 - ks, starts)
        if key not in pats:
            pats.append(key)
        pat_id.append(pats.index(key))
    wq = np.arange(GRID_W)
    wk = np.arange(GRID_W)
    col_start = np.clip(wq - NA_WIN_COLS // 2, 0, GRID_W - NA_WIN_COLS)
    col_off = wk[None, :] - col_start[:, None]
    col_valid = (col_off >= 0) & (col_off < NA_WIN_COLS)
    dc = np.clip(wk[None, :] - wq[:, None], -(NA_WIN_COLS - 1), NA_WIN_COLS - 1) + (NA_WIN_COLS - 1)
    n_dc = 2 * NA_WIN_COLS - 1
    onehot = (dc.reshape(-1)[None, :] == np.arange(n_dc)[:, None]).astype(np.float32)
    heads, n_dr = rpb.shape[0], rpb.shape[1]
    col = jnp.dot(rpb.astype(F32).reshape(heads * n_dr, n_dc), jnp.asarray(onehot),
                  precision=lax.Precision.HIGHEST).reshape(heads, n_dr, GRID_W, GRID_W)
    col = jnp.where(col_valid[None, None], col * LOG2E, NEG_BIG)
    masked_blk = jnp.full((heads, GRID_W, GRID_W), NEG_BIG, F32)
    tables = []
    for r0rel, starts in pats:
        qrows = []
        for qi in range(2):
            blks = []
            for kr in range(NA_KEY_ROWS):
                row_ok = starts[qi] <= kr < starts[qi] + NA_WIN_ROWS
                dr = kr - (r0rel + qi) + (NA_WIN_ROWS - 1)
                blks.append(col[:, dr] if row_ok else masked_blk)
            qrows.append(jnp.concatenate(blks, axis=-1))
        tables.append(jnp.concatenate(qrows, axis=1))
    return jnp.stack(tables, axis=0), np.asarray(pat_id, np.int32)


def _na_attention(hcat, rpb, batch, seq):
    t = hcat.shape[0]
    rows_n = seq // GRID_W
    nblk = rows_n // 2
    nq = 2 * GRID_W
    bias, pat_id = _na_bias(rpb, rows_n)
    nk = NA_KEY_ROWS * GRID_W
    hp = NA_HEADS // 2
    nstep = nblk // NA_SUB

    def bias_spec(sub):
        return pl.BlockSpec((1, 2, nq, nk), lambda b, h, r, pat: (pat[NA_SUB * r + sub], h, 0, 0))

    grid_spec = pltpu.PrefetchScalarGridSpec(
        num_scalar_prefetch=1,
        grid=(batch, hp, nstep),
        in_specs=[
            pl.BlockSpec((NA_SUB * nq, LANES), lambda b, h, r, pat: (b * nstep + r, h)),
            pl.BlockSpec((seq, LANES), lambda b, h, r, pat: (b, hp + h)),
            pl.BlockSpec((seq, LANES), lambda b, h, r, pat: (b, 2 * hp + h)),
        ] + [bias_spec(sub) for sub in range(NA_SUB)],
        out_specs=pl.BlockSpec((NA_SUB * nq, LANES), lambda b, h, r, pat: (b * nstep + r, h)),
    )
    return pl.pallas_call(
        functools.partial(_na_body, rows_n=rows_n),
        grid_spec=grid_spec,
        out_shape=jax.ShapeDtypeStruct((t, NA_HEADS * HEAD_DIM), BF16),
        compiler_params=_cparams(("parallel", "parallel", "arbitrary")),
        name="na_attn",
    )(jnp.asarray(pat_id), hcat, hcat, hcat, *([bias] * NA_SUB))


def _diff_body(lam_ref, q_ref, k_ref, v_ref, g_ref, o_ref, vt_ref, acc_ref, sta_ref, stb_ref, pa_ref, pb_ref,
               *, tk, lambda_init):
    j = pl.program_id(2)
    tq = q_ref.shape[0]
    seq = k_ref.shape[0]
    nkv = seq // tk

    @pl.when(j == 0)
    def _():
        ones = jnp.ones((DIFF_ONES_ROWS, tk), BF16)
        for c in range(nkv):
            vt_ref[c, :LANES, :] = v_ref[c * tk:(c + 1) * tk, :].astype(F32).T.astype(BF16)
            vt_ref[c, LANES:, :] = ones

    q = q_ref[...].astype(F32)
    lane = lax.broadcasted_iota(I32, (tq, LANES), 1)
    q1 = jnp.where(lane < HEAD_DIM, q, 0.0).T
    q2 = jnp.where(lane >= HEAD_DIM, q, 0.0).T
    rhs = jnp.concatenate([q1, q2], axis=1).astype(BF16)

    def scores(i, dst_ref):
        start = pl.multiple_of(i * tk, tk)
        dst_ref[...] = jnp.dot(k_ref[pl.ds(start, tk), :], rhs, preferred_element_type=F32)

    def softmax(src_ref, dst_ref, m):
        st = src_ref[...]
        m_new = jnp.maximum(m, jnp.max(st, axis=0, keepdims=True))
        dst_ref[...] = jnp.exp2((st - m_new).astype(BF16))
        return m_new, jnp.exp2(m - m_new)

    def weighted_values(i, p_ref, alpha):
        pv = jnp.dot(vt_ref[i], p_ref[...], preferred_element_type=F32)
        acc_ref[...] = alpha * acc_ref[...] + pv

    acc_ref[...] = jnp.zeros_like(acc_ref)
    scores(0, sta_ref)
    scores(1, stb_ref)
    m, a0 = softmax(sta_ref, pa_ref, jnp.full((1, 2 * tq), NEG_BIG, F32))

    def pair(ii, carry):
        m, a0 = carry
        i = 2 * ii
        scores(i + 2, sta_ref)
        m, a1 = softmax(stb_ref, pb_ref, m)
        weighted_values(i, pa_ref, a0)
        scores(i + 3, stb_ref)
        m, a0 = softmax(sta_ref, pa_ref, m)
        weighted_values(i + 1, pb_ref, a1)
        return m, a0

    m, a0 = lax.fori_loop(0, (nkv - 2) // 2, pair, (m, a0))
    m, a1 = softmax(stb_ref, pb_ref, m)
    weighted_values(nkv - 2, pa_ref, a0)
    weighted_values(nkv - 1, pb_ref, a1)
    acc = acc_ref[...]
    ot = acc[:LANES, :] / acc[LANES:LANES + 1, :]
    dt = ot[:, :tq] - lam_ref[0] * ot[:, tq:]
    ms = jnp.mean(dt * dt, axis=0, keepdims=True)
    y = dt * lax.rsqrt(ms + LN_EPS) * g_ref[...] * (1.0 - lambda_init)
    o_ref[...] = y.T.astype(o_ref.dtype)


def _diff_attention(hcat, lam, subln_g, lambda_init, batch, seq, tq=2048, tk=512):
    t = hcat.shape[0]
    tq = min(tq, seq)
    tk = min(tk, seq)
    nq = seq // tq
    qoff = 3 * NA_HEADS * HEAD_DIM // LANES
    koff = qoff + DIFF_HEADS
    voff = koff + DIFF_HEADS
    grid_spec = pltpu.PrefetchScalarGridSpec(
        num_scalar_prefetch=1,
        grid=(batch, DIFF_HEADS, nq),
        in_specs=[
            pl.BlockSpec((tq, LANES), lambda b, h, j, lam: (b * nq + j, qoff + h)),
            pl.BlockSpec((seq, LANES), lambda b, h, j, lam: (b, koff + h)),
            pl.BlockSpec((seq, LANES), lambda b, h, j, lam: (b, voff + h)),
            pl.BlockSpec((LANES, 1), lambda b, h, j, lam: (0, 0)),
        ],
        out_specs=pl.BlockSpec((tq, LANES), lambda b, h, j, lam: (b * nq + j, h)),
        scratch_shapes=[pltpu.VMEM((seq // tk, LANES + DIFF_ONES_ROWS, tk), BF16),
                        pltpu.VMEM((LANES + DIFF_ONES_ROWS, 2 * tq), F32),
                        pltpu.VMEM((tk, 2 * tq), F32),
                        pltpu.VMEM((tk, 2 * tq), F32),
                        pltpu.VMEM((tk, 2 * tq), BF16),
                        pltpu.VMEM((tk, 2 * tq), BF16)],
    )
    return pl.pallas_call(
        functools.partial(_diff_body, tk=tk, lambda_init=lambda_init),
        grid_spec=grid_spec,
        out_shape=jax.ShapeDtypeStruct((t, DIFF_HEADS * 2 * HEAD_DIM), BF16),
        compiler_params=_cparams(("parallel", "parallel", "arbitrary")),
        name="diff_attn",
    )(lam, hcat, hcat, hcat, subln_g.astype(F32).reshape(LANES, 1))


def _swa_body(sink_ref, q_ref, kv_ref, o_ref, kvt_ref):
    kvh = pl.program_id(1)
    nq = SWA_BLOCK
    nsub = q_ref.shape[0] // nq
    seq = kv_ref.shape[0]
    nblk = seq // SWA_BLOCK
    wb = min(3, nblk)
    nk = wb * SWA_BLOCK

    @pl.when(pl.program_id(2) == 0)
    def _():
        for c in range(nblk):
            kvt_ref[c] = kv_ref[c * SWA_BLOCK:(c + 1) * SWA_BLOCK, :].astype(F32).T.astype(BF16)

    group = SWA_Q_HEADS // SWA_KV_HEADS
    half = LANES // 2

    def swap_halves(a):
        return jnp.concatenate([a[half:], a[:half]], axis=0)

    top = lax.broadcasted_iota(I32, (LANES, nq), 0) < half
    sink = jnp.concatenate([jnp.full((1, nq), sink_ref[kvh * group + g], F32) for g in range(group)], axis=1)

    scores, kvts = [], []
    for sub in range(nsub):
        n = nsub * pl.program_id(2) + sub
        b0 = jnp.clip(n - 1, 0, nblk - wb)
        start = pl.multiple_of(b0 * SWA_BLOCK, SWA_BLOCK)
        cols = []
        for c in range(group // 2):
            qt = q_ref[sub * nq:(sub + 1) * nq, c * LANES:(c + 1) * LANES].astype(F32).T
            cols.append(jnp.where(top, qt, 0.0))
            cols.append(jnp.where(top, swap_halves(qt), 0.0))
        rhs = jnp.concatenate(cols, axis=1).astype(BF16)
        st = jnp.dot(kv_ref[pl.ds(start, nk), :], rhs, preferred_element_type=F32)
        kpos = start + lax.broadcasted_iota(I32, (nk, nq), 0)
        qpos = n * SWA_BLOCK + lax.broadcasted_iota(I32, (nk, nq), 1)
        mask = jnp.where(jnp.abs(kpos - qpos) <= SWA_WINDOW, 0.0, NEG_BIG)
        scores.append(st + jnp.concatenate([mask] * group, axis=1))
        kvts.append(jnp.concatenate([kvt_ref[b0 + w] for w in range(wb)], axis=1))
    for sub in range(nsub):
        st = scores[sub]
        m = jnp.maximum(jnp.max(st, axis=0, keepdims=True), sink)
        e = jnp.exp2(st - m)
        den = jnp.sum(e, axis=0, keepdims=True) + jnp.exp2(sink - m)
        ot = jnp.dot(kvts[sub], e.astype(BF16), preferred_element_type=F32) / den
        for c in range(group // 2):
            even = ot[:, 2 * c * nq:(2 * c + 1) * nq]
            odd = ot[:, (2 * c + 1) * nq:(2 * c + 2) * nq]
            blk = jnp.where(top, swap_halves(even), odd)
            o_ref[sub * nq:(sub + 1) * nq, c * LANES:(c + 1) * LANES] = blk.T.astype(o_ref.dtype)


def _swa_attention(hcat, sinks, batch, seq):
    t = hcat.shape[0]
    nb = seq // SWA_BLOCK
    group = SWA_Q_HEADS // SWA_KV_HEADS
    qw = group * HEAD_DIM
    kvoff = SWA_Q_HEADS * HEAD_DIM // LANES
    nsub = min(SWA_SUB, nb)
    nstep = nb // nsub
    grid_spec = pltpu.PrefetchScalarGridSpec(
        num_scalar_prefetch=1,
        grid=(batch, SWA_KV_HEADS, nstep),
        in_specs=[
            pl.BlockSpec((nsub * SWA_BLOCK, qw), lambda b, h, n, s: (b * nstep + n, h)),
            pl.BlockSpec((seq, LANES), lambda b, h, n, s: (b, kvoff + h)),
        ],
        out_specs=pl.BlockSpec((nsub * SWA_BLOCK, qw), lambda b, h, n, s: (b * nstep + n, h)),
        scratch_shapes=[pltpu.VMEM((nb, LANES, SWA_BLOCK), BF16)],
    )
    return pl.pallas_call(
        _swa_body,
        grid_spec=grid_spec,
        out_shape=jax.ShapeDtypeStruct((t, SWA_Q_HEADS * HEAD_DIM), BF16),
        compiler_params=_cparams(("parallel", "parallel", "arbitrary")),
        name="swa_attn",
    )(sinks.astype(F32) * LOG2E, hcat, hcat)


def _outproj_body(*refs, n_in):
    a_refs = refs[:n_in]
    w_ref, x_ref, g_ref, b_ref, o_ref, opk_ref = refs[n_in:]
    acc = None
    off = 0
    for a_ref in a_refs:
        ka = a_ref.shape[1]
        d = jnp.dot(a_ref[...], w_ref[off:off + ka, :], preferred_element_type=F32)
        acc = d if acc is None else acc + d
        off += ka
    y = DN_ALPHA * x_ref[...] + acc
    out = _layer_norm(y, g_ref[...], b_ref[...])
    o_ref[...] = out
    packed = _pack_rows(out)
    for c in range(PACK_SPLIT):
        opk_ref[c] = packed[:, c * LANES:(c + 1) * LANES]


def _outproj_ln(acts, w_bf, x2, g, b, tm=512):
    t, d = x2.shape
    tm = min(tm, t)
    in_specs = [pl.BlockSpec((tm, a.shape[1]), lambda i: (i, 0)) for a in acts]
    in_specs += [pl.BlockSpec(w_bf.shape, lambda i: (0, 0)),
                 pl.BlockSpec((tm, d), lambda i: (i, 0)),
                 pl.BlockSpec((1, d), lambda i: (0, 0)),
                 pl.BlockSpec((1, d), lambda i: (0, 0))]
    return pl.pallas_call(
        functools.partial(_outproj_body, n_in=len(acts)),
        grid=(t // tm,),
        in_specs=in_specs,
        out_specs=[pl.BlockSpec((tm, d), lambda i: (i, 0)),
                   pl.BlockSpec((PACK_SPLIT, tm, LANES), lambda i: (0, i, 0))],
        out_shape=[jax.ShapeDtypeStruct((t, d), F32), jax.ShapeDtypeStruct((PACK_SPLIT, t, LANES), I32)],
        compiler_params=_cparams(("parallel",)),
        name="outproj_ln",
    )(*acts, w_bf, x2, g.astype(F32).reshape(1, d), b.astype(F32).reshape(1, d))


def _router_body(h_ref, whi_ref, wlo_ref, bias_ref, idx_ref, gate_ref, pos_ref, cnt_ref, base_ref):
    i = pl.program_id(0)
    tm = h_ref.shape[0]
    gsz = N_EXPERTS // N_GROUPS

    @pl.when(i == 0)
    def _():
        base_ref[...] = jnp.zeros_like(base_ref)

    h = h_ref[...]
    h_hi = h.astype(BF16)
    h_lo = (h - h_hi.astype(F32)).astype(BF16)
    dn = (((1,), (1,)), ((), ()))
    whi = whi_ref[...]
    logits = (lax.dot_general(whi, h_hi, dn, preferred_element_type=F32)
              + lax.dot_general(whi, h_lo, dn, preferred_element_type=F32)
              + lax.dot_general(wlo_ref[...], h_hi, dn, preferred_element_type=F32))
    scores = 1.0 / (1.0 + jnp.exp(-logits))
    choice = scores + bias_ref[...]

    iota_g = lax.broadcasted_iota(I32, (gsz, tm), 0)
    gscore = []
    for g in range(N_GROUPS):
        cg = choice[g * gsz:(g + 1) * gsz, :]
        m1 = jnp.max(cg, axis=0, keepdims=True)
        first = jnp.min(jnp.where(cg == m1, iota_g, gsz), axis=0, keepdims=True)
        m2 = jnp.max(jnp.where(iota_g == first, -jnp.inf, cg), axis=0, keepdims=True)
        gscore.append(m1 + m2)
    pieces = []
    for g in range(N_GROUPS):
        rank = jnp.zeros((1, tm), I32)
        for o in range(N_GROUPS):
            if o == g:
                continue
            beats = (gscore[o] > gscore[g]) if o > g else (gscore[o] >= gscore[g])
            rank = rank + beats.astype(I32)
        keep = rank < TOPK_GROUPS
        pieces.append(jnp.where(keep, choice[g * gsz:(g + 1) * gsz, :], -jnp.inf))
    masked = jnp.concatenate(pieces, axis=0)

    iota_e = lax.broadcasted_iota(I32, (N_EXPERTS, tm), 0)
    sel_all = jnp.zeros((N_EXPERTS, tm), F32)
    idxs, gates, sels = [], [], []
    for _ in range(TOP_K):
        mx = jnp.max(masked, axis=0, keepdims=True)
        idx = jnp.min(jnp.where(masked == mx, iota_e, N_EXPERTS), axis=0, keepdims=True)
        sel = iota_e == idx
        gates.append(jnp.sum(jnp.where(sel, scores, 0.0), axis=0, keepdims=True))
        masked = jnp.where(sel, -jnp.inf, masked)
        sel_all = sel_all + sel.astype(F32)
        idxs.append(idx)
        sels.append(sel)
    gsum = gates[0]
    for gk in gates[1:]:
        gsum = gsum + gk
    gate_ref[...] = jnp.concatenate(gates, axis=0) / gsum * ROUTED_SCALE
    idx_ref[...] = jnp.concatenate(idxs, axis=0)

    tri = (lax.broadcasted_iota(I32, (tm, tm), 0) < lax.broadcasted_iota(I32, (tm, tm), 1))
    cum = jnp.dot(sel_all.astype(BF16), tri.astype(F32).astype(BF16), preferred_element_type=F32)
    tot = cum + base_ref[...]
    pos = [jnp.sum(jnp.where(sel, tot, 0.0), axis=0, keepdims=True) for sel in sels]
    pos_ref[...] = jnp.concatenate(pos, axis=0).astype(I32)
    base_ref[...] = base_ref[...] + jnp.sum(sel_all, axis=1, keepdims=True)
    cnt_ref[...] = base_ref[...]


def _router(h, router_w, router_bias, tm=512):
    t, d = h.shape
    tm = min(tm, t)
    wt = router_w.astype(F32).T
    whi = wt.astype(BF16)
    wlo = (wt - whi.astype(F32)).astype(BF16)
    return pl.pallas_call(
        _router_body,
        grid=(t // tm,),
        in_specs=[pl.BlockSpec((tm, d), lambda i: (i, 0)),
                  pl.BlockSpec((N_EXPERTS, d), lambda i: (0, 0)),
                  pl.BlockSpec((N_EXPERTS, d), lambda i: (0, 0)),
                  pl.BlockSpec((N_EXPERTS, 1), lambda i: (0, 0))],
        out_specs=[pl.BlockSpec((TOP_K, tm), lambda i: (0, i)),
                   pl.BlockSpec((TOP_K, tm), lambda i: (0, i)),
                   pl.BlockSpec((TOP_K, tm), lambda i: (0, i)),
                   pl.BlockSpec((N_EXPERTS, 1), lambda i: (0, 0))],
        out_shape=[jax.ShapeDtypeStruct((TOP_K, t), I32),
                   jax.ShapeDtypeStruct((TOP_K, t), F32),
                   jax.ShapeDtypeStruct((TOP_K, t), I32),
                   jax.ShapeDtypeStruct((N_EXPERTS, 1), F32)],
        scratch_shapes=[pltpu.VMEM((N_EXPERTS, 1), F32)],
        compiler_params=_cparams(("arbitrary",)),
        name="router",
    )(h, whi, wlo, router_bias.astype(F32).reshape(N_EXPERTS, 1))


def _expert_body(be_ref, na_ref, bv_ref, ord_ref, nxt_ref, x_ref, wg_hbm, wu_hbm, wd_hbm, y_ref,
                 wgu, wdb, wg_buf, wu_buf, wd_buf, sem):
    i = pl.program_id(0)
    bm = x_ref.shape[1]
    ff = wd_hbm.shape[1]
    hm = bm // EXPERT_CHAINS

    def weight_copies(e, slot):
        return [pltpu.make_async_copy(hbm.at[e], buf.at[slot], sem.at[slot])
                for hbm, buf in ((wg_hbm, wg_buf), (wu_hbm, wu_buf), (wd_hbm, wd_buf))]

    @pl.when(i < na_ref[0])
    def _():
        prev = be_ref[jnp.maximum(i - 1, 0)]
        slot = ord_ref[i] % 2

        @pl.when(i == 0)
        def _():
            for cp in weight_copies(be_ref[0], 0):
                cp.start()

        @pl.when((i == 0) | (be_ref[i] != prev))
        def _():
            for cp in weight_copies(be_ref[i], slot):
                cp.wait()
            wgu[:, :ff] = wg_buf[slot].astype(BF16)
            wgu[:, ff:] = wu_buf[slot].astype(BF16)
            wdb[...] = wd_buf[slot].astype(BF16)

            @pl.when(nxt_ref[i] >= 0)
            def _():
                for cp in weight_copies(nxt_ref[i], 1 - slot):
                    cp.start()

        valid = bv_ref[i]
        row = lax.broadcasted_iota(I32, (hm, LANES), 0)
        def run(n_chains):
            gus = []
            for hb in range(n_chains):
                rows = slice(hb * hm, (hb + 1) * hm)
                keep = row < valid - hb * hm
                x = _unpack_rows(lambda c: jnp.where(keep, x_ref[c, rows, :], 0)).astype(BF16)
                gus.append(jnp.dot(x, wgu[...], preferred_element_type=F32))
            ys = []
            for gu in gus:
                hmid = (_silu(gu[:, :ff]) * gu[:, ff:]).astype(BF16)
                ys.append(jnp.dot(hmid, wdb[...], preferred_element_type=F32))
            for hb, y in enumerate(ys):
                packed = _pack_rows(y)
                for c in range(PACK_SPLIT):
                    y_ref[c, hb * hm:(hb + 1) * hm, :] = packed[:, c * LANES:(c + 1) * LANES]
            if n_chains < EXPERT_CHAINS:
                y_ref[:, n_chains * hm:, :] = jnp.zeros((PACK_SPLIT, bm - n_chains * hm, LANES), I32)

        for n_chains in range(1, EXPERT_CHAINS + 1):
            lo = (n_chains - 1) * hm
            cond = valid > lo if n_chains == EXPERT_CHAINS else (valid > lo) & (valid <= lo + hm)
            pl.when(cond)(functools.partial(run, n_chains))


def _expert_matmul(xs, blk_expert, n_active, blk_valid, w_gate, w_up, w_down):
    n_slots = xs.shape[1]
    bm = EXPERT_BLOCK
    n_blocks = n_slots // bm
    d, ff = w_gate.shape[1], w_gate.shape[2]

    blk = jnp.arange(n_blocks, dtype=I32)
    change = jnp.concatenate([jnp.zeros((1,), I32), (blk_expert[1:] != blk_expert[:-1]).astype(I32)])
    ordinal = jnp.sum(jnp.where(blk[None, :] <= blk[:, None], change[None, :], 0), axis=1).astype(I32)
    later = (ordinal[None, :] == ordinal[:, None] + 1) & (blk[None, :] < n_active[0])
    nxt = jnp.max(jnp.where(later, blk_expert[None, :], -1), axis=1).astype(I32)

    def row_map(i, be, na, bv, od, nx):
        return (0, jnp.minimum(i, na[0] - 1), 0)

    grid_spec = pltpu.PrefetchScalarGridSpec(
        num_scalar_prefetch=5,
        grid=(n_blocks,),
        in_specs=[pl.BlockSpec((PACK_SPLIT, bm, LANES), row_map),
                  pl.BlockSpec(memory_space=pl.ANY),
                  pl.BlockSpec(memory_space=pl.ANY),
                  pl.BlockSpec(memory_space=pl.ANY)],
        out_specs=pl.BlockSpec((PACK_SPLIT, bm, LANES), row_map),
        scratch_shapes=[pltpu.VMEM((d, 2 * ff), BF16), pltpu.VMEM((ff, d), BF16),
                        pltpu.VMEM((2, d, ff), F32), pltpu.VMEM((2, d, ff), F32), pltpu.VMEM((2, ff, d), F32),
                        pltpu.SemaphoreType.DMA((2,))],
    )
    return pl.pallas_call(
        _expert_body,
        grid_spec=grid_spec,
        out_shape=jax.ShapeDtypeStruct((PACK_SPLIT, n_slots, LANES), I32),
        compiler_params=_cparams(("arbitrary",)),
        name="expert_mlp",
    )(blk_expert, n_active, blk_valid, ordinal, nxt, xs, w_gate, w_up, w_down)


def _slot_body(idx_ref, pos_ref, ps_ref, slot_ref, *, n_slots):
    tm = idx_ref.shape[1]
    iota_e = lax.broadcasted_iota(I32, (N_EXPERTS, tm), 0)
    ps = ps_ref[...]
    rows = []
    for k in range(TOP_K):
        start = jnp.sum(jnp.where(iota_e == idx_ref[k:k + 1, :], ps, 0.0), axis=0, keepdims=True)
        rows.append(start.astype(I32) + pos_ref[k:k + 1, :])
    slot = jnp.concatenate(rows, axis=0)
    for c in range(PACK_SPLIT):
        slot_ref[c * TOP_K:(c + 1) * TOP_K, :] = slot + c * n_slots


def _slots(idx, pos, pad_start, n_slots, tm=512):
    t = idx.shape[1]
    tm = min(tm, t)
    return pl.pallas_call(
        functools.partial(_slot_body, n_slots=n_slots),
        grid=(t // tm,),
        in_specs=[pl.BlockSpec((TOP_K, tm), lambda i: (0, i)),
                  pl.BlockSpec((TOP_K, tm), lambda i: (0, i)),
                  pl.BlockSpec((N_EXPERTS, 1), lambda i: (0, 0))],
        out_specs=pl.BlockSpec((PACK_SPLIT * TOP_K, tm), lambda i: (0, i)),
        out_shape=jax.ShapeDtypeStruct((PACK_SPLIT * TOP_K, t), I32),
        compiler_params=_cparams(("parallel",)),
        name="slots",
    )(idx, pos, pad_start.astype(F32).reshape(N_EXPERTS, 1))


def _sc_mesh():
    return plsc.VectorSubcoreMesh(core_axis_name="core", subcore_axis_name="subcore",
                                  num_cores=SC_CORES, num_subcores=SC_SUBCORES)


def _sc_scatter_rows(rows, idx, n_out):
    t = idx.shape[1]
    nj = t // SC_WINDOW

    @functools.partial(pl.kernel, out_type=jax.ShapeDtypeStruct((n_out, LANES), I32),
                       mesh=_sc_mesh(), scratch_types=[], name="sc_dispatch")
    def run(rows_hbm, idx_hbm, out_hbm):
        def body(rows_vmem, idx_vmem):
            for k in range(TOP_K):
                pltpu.sync_copy(rows_vmem, out_hbm.at[idx_vmem.at[k]])

        pltpu.emit_pipeline(
            body,
            grid=(rows.shape[0] // SC_WINDOW,),
            in_specs=[pl.BlockSpec((SC_WINDOW, LANES), lambda s: (s, 0)),
                      pl.BlockSpec((TOP_K, SC_WINDOW), lambda s: (s // nj, s % nj))],
            out_specs=[],
            core_axis_name=("core", "subcore"),
            dimension_semantics=(pltpu.PARALLEL,),
        )(rows_hbm, idx_hbm)

    return run(rows, idx)


def _sc_gather_rows(src, idx):
    nr, t = idx.shape
    nj = t // SC_WINDOW

    @functools.partial(pl.kernel, out_type=jax.ShapeDtypeStruct((nr * t, LANES), I32),
                       mesh=_sc_mesh(), scratch_types=[], name="sc_combine")
    def run(src_hbm, idx_hbm, out_hbm):
        def body(idx_vmem, out_vmem):
            pltpu.sync_copy(src_hbm.at[idx_vmem.at[0]], out_vmem)

        pltpu.emit_pipeline(
            body,
            grid=(nr * nj,),
            in_specs=[pl.BlockSpec((1, SC_WINDOW), lambda s: (s // nj, s % nj))],
            out_specs=[pl.BlockSpec((SC_WINDOW, LANES), lambda s: (s, 0))],
            core_axis_name=("core", "subcore"),
            dimension_semantics=(pltpu.PARALLEL,),
        )(idx_hbm, out_hbm)

    return run(src, idx)


def _combine_body(h_ref, yg_ref, gt_ref, sg_ref, su_ref, sd_ref, g_ref, b_ref, o_ref):
    h = h_ref[...]
    hb = h.astype(BF16)
    a = jnp.dot(hb, sg_ref[...], preferred_element_type=F32)
    u = jnp.dot(hb, su_ref[...], preferred_element_type=F32)
    moe = jnp.dot((_silu(a) * u).astype(BF16), sd_ref[...], preferred_element_type=F32)
    gt = gt_ref[...]
    for k in range(TOP_K):
        moe = moe + gt[:, k:k + 1] * _unpack_rows(lambda c, k=k: yg_ref[c, k])
    o_ref[...] = _layer_norm(DN_ALPHA * h + moe, g_ref[...], b_ref[...])


def _combine_ln(h, yg, gates_t, sh_gate, sh_up, sh_down, g, b, tm=256):
    t, d = h.shape
    tm = min(tm, t)
    ff = sh_gate.shape[1]
    return pl.pallas_call(
        _combine_body,
        grid=(t // tm,),
        in_specs=[pl.BlockSpec((tm, d), lambda i: (i, 0)),
                  pl.BlockSpec((PACK_SPLIT, TOP_K, tm, LANES), lambda i: (0, 0, i, 0)),
                  pl.BlockSpec((tm, TOP_K), lambda i: (i, 0)),
                  pl.BlockSpec((d, ff), lambda i: (0, 0)),
                  pl.BlockSpec((d, ff), lambda i: (0, 0)),
                  pl.BlockSpec((ff, d), lambda i: (0, 0)),
                  pl.BlockSpec((1, d), lambda i: (0, 0)),
                  pl.BlockSpec((1, d), lambda i: (0, 0))],
        out_specs=pl.BlockSpec((tm, d), lambda i: (i, 0)),
        out_shape=jax.ShapeDtypeStruct((t, d), F32),
        compiler_params=_cparams(("parallel",)),
        name="combine_ln",
    )(h, yg, gates_t, sh_gate.astype(BF16), sh_up.astype(BF16), sh_down.astype(BF16),
      g.astype(F32).reshape(1, d), b.astype(F32).reshape(1, d))


def _moe_layer(h, h_pk, router_w, router_bias, w_gate, w_up, w_down, sh_gate, sh_up, sh_down, g, b):
    t, d = h.shape
    bm = EXPERT_BLOCK
    idx, gates, pos, cnt = _router(h, router_w, router_bias)
    counts = cnt[:, 0].astype(I32)
    padded = (counts + bm - 1) // bm * bm
    e_iota = jnp.arange(N_EXPERTS, dtype=I32)
    pad_end = jnp.sum(jnp.where(e_iota[None, :] <= e_iota[:, None], padded[None, :], 0), axis=1)
    pad_start = pad_end - padded
    n_blocks = t * TOP_K // bm + N_EXPERTS
    n_slots = n_blocks * bm
    blk_start = jnp.arange(n_blocks, dtype=I32) * bm
    blk_expert = jnp.minimum(
        jnp.sum((blk_start[:, None] >= pad_end[None, :]).astype(I32), axis=1), N_EXPERTS - 1)
    real_end = jnp.sum(jnp.where(e_iota[None, :] == blk_expert[:, None], (pad_start + counts)[None, :], 0), axis=1)
    blk_valid = jnp.clip(real_end - blk_start, 0, bm).astype(I32)
    n_active = (pad_end[-1:] // bm).astype(I32)

    slot4 = _slots(idx, pos, pad_start, n_slots)
    xs = _sc_scatter_rows(h_pk.reshape(PACK_SPLIT * t, LANES), slot4, PACK_SPLIT * n_slots)
    ys = _expert_matmul(xs.reshape(PACK_SPLIT, n_slots, LANES), blk_expert, n_active, blk_valid,
                        w_gate, w_up, w_down)
    yg = _sc_gather_rows(ys.reshape(PACK_SPLIT * n_slots, LANES), slot4)
    return _combine_ln(h, yg.reshape(PACK_SPLIT, TOP_K, t, LANES), gates.T, sh_gate, sh_up, sh_down, g, b)


def kernel(x, l0_w_in, l0_w_out, l0_na_rpb, l0_diff_lq1, l0_diff_lk1, l0_diff_lq2, l0_diff_lk2, l0_diff_subln_g, l0_ln1_g, l0_ln1_b, l0_router_w, l0_router_bias, l0_expert_w_gate, l0_expert_w_up, l0_expert_w_down, l0_shared_w_gate, l0_shared_w_up, l0_shared_w_down, l0_ln2_g, l0_ln2_b, l1_w_in, l1_w_out, l1_swa_sinks, l1_ln1_g, l1_ln1_b, l1_router_w, l1_router_bias, l1_expert_w_gate, l1_expert_w_up, l1_expert_w_down, l1_shared_w_gate, l1_shared_w_up, l1_shared_w_down, l1_ln2_g, l1_ln2_b):
    batch, seq, d = x.shape
    t = batch * seq
    x2 = x.reshape(t, d).astype(F32)
    tab = _rope_table(seq)
    qscale = HEAD_DIM ** -0.5 * LOG2E
    na_w = NA_HEADS * HEAD_DIM
    dq_w = DIFF_HEADS * 2 * HEAD_DIM

    col_scale = jnp.concatenate([
        jnp.full((na_w,), qscale, F32), jnp.ones((2 * na_w,), F32),
        jnp.full((dq_w,), qscale, F32), jnp.ones((2 * dq_w,), F32)])
    w_in0 = (l0_w_in.astype(F32) * col_scale).astype(BF16)
    per = LANES
    rope0 = [0] * (3 * na_w // per) + [1] * (2 * dq_w // per) + [0] * (dq_w // per)
    hcat0 = _inproj(x2, w_in0, tab, rope0, seq)
    oa = _na_attention(hcat0, l0_na_rpb, batch, seq)
    lambda_init = 0.8 - 0.6 * math.exp(-0.3 * 0)
    lam = (jnp.exp(jnp.sum(l0_diff_lq1.astype(F32) * l0_diff_lk1.astype(F32)))
           - jnp.exp(jnp.sum(l0_diff_lq2.astype(F32) * l0_diff_lk2.astype(F32))) + lambda_init)
    od = _diff_attention(hcat0, lam.reshape(1).astype(F32), l0_diff_subln_g, lambda_init, batch, seq)
    h, h_pk = _outproj_ln([oa, od], l0_w_out.astype(BF16), x2, l0_ln1_g, l0_ln1_b)
    x2 = _moe_layer(h, h_pk, l0_router_w, l0_router_bias, l0_expert_w_gate, l0_expert_w_up,
                    l0_expert_w_down, l0_shared_w_gate, l0_shared_w_up, l0_shared_w_down,
                    l0_ln2_g, l0_ln2_b)

    q_w = SWA_Q_HEADS * HEAD_DIM
    kv_w = SWA_KV_HEADS * HEAD_DIM
    w1 = l1_w_in.astype(F32)
    wq = w1[:, :q_w] * qscale
    wk = w1[:, q_w:q_w + kv_w].reshape(d, SWA_KV_HEADS, HEAD_DIM)
    wv = w1[:, q_w + kv_w:].reshape(d, SWA_KV_HEADS, HEAD_DIM)
    wkv = jnp.concatenate([wk, wv], axis=-1).reshape(d, 2 * kv_w)
    w_in1 = jnp.concatenate([wq, wkv], axis=1).astype(BF16)
    rope1 = [1] * (q_w // per) + [2] * (2 * kv_w // per)
    hcat1 = _inproj(x2, w_in1, tab, rope1, seq)
    o1 = _swa_attention(hcat1, l1_swa_sinks, batch, seq)
    h, h_pk = _outproj_ln([o1], l1_w_out.astype(BF16), x2, l1_ln1_g, l1_ln1_b)
    x2 = _moe_layer(h, h_pk, l1_router_w, l1_router_bias, l1_expert_w_gate, l1_expert_w_up,
                    l1_expert_w_down, l1_shared_w_gate, l1_shared_w_up, l1_shared_w_down,
                    l1_ln2_g, l1_ln2_b)
    return x2.reshape(batch, seq, d).astype(x.dtype)
```

```python
import functools
import math

import numpy as np
import jax
import jax.numpy as jnp
from jax import lax
from jax.experimental import pallas as pl
from jax.experimental.pallas import tpu as pltpu
from jax.experimental.pallas import tpu_sc as plsc

F32 = jnp.float32
BF16 = jnp.bfloat16
I32 = jnp.int32

HEAD_DIM = 64
GRID_W = 64
NA_HEADS = 8
NA_WIN_ROWS = 8
NA_WIN_COLS = 16
DIFF_HEADS = 4
SWA_Q_HEADS = 16
SWA_KV_HEADS = 4
SWA_WINDOW = 128
SWA_BLOCK = 128
ROPE_THETA = 10000.0
N_EXPERTS = 256
TOP_K = 8
N_GROUPS = 8
TOPK_GROUPS = 4
ROUTED_SCALE = 2.5
LN_EPS = 1e-5
DEPTH = 2
DN_ALPHA = (2 * DEPTH) ** 0.25

LOG2E = 1.4426950408889634
NEG_BIG = -1e30
LANES = 128
NA_KEY_ROWS = NA_WIN_ROWS + 1
PROJ_CHUNK = 256
SWA_SUB = 8
NA_SUB = 8
EXPERT_BLOCK = 512
EXPERT_CHAINS = 2
DIFF_ONES_ROWS = 16
PACK_SPLIT = 4
SC_CORES = 2
SC_SUBCORES = 16
SC_WINDOW = 128
VMEM_LIMIT = 56 * 1024 * 1024


def _cparams(sem):
    return pltpu.CompilerParams(dimension_semantics=sem, vmem_limit_bytes=VMEM_LIMIT)


def _silu(x):
    return x / (1.0 + jnp.exp(-x))


def _pack_rows(y):
    half = y.shape[1] // 2
    bits = pltpu.bitcast(y.astype(BF16).astype(F32), I32)
    lo = lax.shift_right_logical(bits[:, :half], 16)
    hi = bits[:, half:] & jnp.int32(-65536)
    return hi | lo


def _unpack_words(w):
    lo = pltpu.bitcast(lax.shift_left(w, 16), F32)
    hi = pltpu.bitcast(w & jnp.int32(-65536), F32)
    return lo, hi


def _unpack_rows(ref_at):
    los, his = [], []
    for c in range(PACK_SPLIT):
        lo, hi = _unpack_words(ref_at(c))
        los.append(lo)
        his.append(hi)
    return jnp.concatenate(los + his, axis=1)


def _layer_norm(y, g, b):
    mu = jnp.mean(y, axis=-1, keepdims=True)
    yc = y - mu
    var = jnp.mean(yc * yc, axis=-1, keepdims=True)
    return yc * lax.rsqrt(var + LN_EPS) * g + b


def _inproj_body(x_ref, w_ref, tab_ref, o_ref, *, chunk, rope_kind):
    _project_rope(x_ref[...].astype(BF16), w_ref, tab_ref, o_ref, chunk, rope_kind)


def _project_rope(x, w_ref, tab_ref, o_ref, chunk, rope_kind):
    tm = x.shape[0]
    n_out = w_ref.shape[1]
    lane = lax.broadcasted_iota(I32, (tm, LANES), 1)
    first_half = (lane % HEAD_DIM) < (HEAD_DIM // 2)
    per = chunk // LANES
    for c in range(n_out // chunk):
        h = jnp.dot(x, w_ref[:, c * chunk:(c + 1) * chunk], preferred_element_type=F32)
        kinds = rope_kind[c * per:(c + 1) * per]
        if any(kinds):
            pieces = []
            for j, kind in enumerate(kinds):
                t = h[:, j * LANES:(j + 1) * LANES]
                if kind:
                    base = (kind - 1) * 2 * LANES
                    cos = tab_ref[:, base:base + LANES]
                    sin = tab_ref[:, base + LANES:base + 2 * LANES]
                    rot = jnp.where(first_half, pltpu.roll(t, LANES - HEAD_DIM // 2, 1),
                                    pltpu.roll(t, HEAD_DIM // 2, 1))
                    t = t * cos + rot * sin
                pieces.append(t)
            h = jnp.concatenate(pieces, axis=1)
        o_ref[:, c * chunk:(c + 1) * chunk] = h.astype(o_ref.dtype)


def _inproj(x2, w_bf, tab, rope_kind, seq, tm=512):
    t, d = x2.shape
    n_out = w_bf.shape[1]
    tm = min(tm, seq)
    sb = seq // tm
    return pl.pallas_call(
        functools.partial(_inproj_body, chunk=PROJ_CHUNK, rope_kind=tuple(rope_kind)),
        grid=(t // tm,),
        in_specs=[pl.BlockSpec((tm, d), lambda i: (i, 0)),
                  pl.BlockSpec((d, n_out), lambda i: (0, 0)),
                  pl.BlockSpec((tm, tab.shape[1]), lambda i: (i % sb, 0))],
        out_specs=pl.BlockSpec((tm, n_out), lambda i: (i, 0)),
        out_shape=jax.ShapeDtypeStruct((t, n_out), BF16),
        compiler_params=_cparams(("parallel",)),
        name="inproj",
    )(x2, w_bf, tab)


def _rope_table(seq):
    half = HEAD_DIM // 2
    inv = 1.0 / (ROPE_THETA ** (jnp.arange(0, HEAD_DIM, 2, dtype=F32) / HEAD_DIM))
    ang = jnp.arange(seq, dtype=F32)[:, None] * inv[None, :]
    cos = jnp.cos(ang)
    sin = jnp.sin(ang)
    cos64 = jnp.concatenate([cos, cos], axis=-1)
    sin64 = jnp.concatenate([-sin, sin], axis=-1)
    one = jnp.ones((seq, HEAD_DIM), F32)
    zero = jnp.zeros((seq, HEAD_DIM), F32)
    del half
    return jnp.concatenate([cos64, cos64, sin64, sin64, cos64, one, sin64, zero], axis=-1)


def _na_body(pat_ref, q_ref, k_ref, v_ref, *rest, rows_n):
    del pat_ref
    bias_refs, o_ref = rest[:NA_SUB], rest[NA_SUB]
    nq = q_ref.shape[0] // NA_SUB
    nk = NA_KEY_ROWS * GRID_W
    low = lax.broadcasted_iota(I32, (nq, LANES), 1) < HEAD_DIM
    scores, vwins = [], []
    for sub in range(NA_SUB):
        r = NA_SUB * pl.program_id(2) + sub
        ks = jnp.clip(2 * r - NA_WIN_ROWS // 2, 0, rows_n - NA_KEY_ROWS)
        start = pl.multiple_of(ks * GRID_W, GRID_W)
        q = q_ref[sub * nq:(sub + 1) * nq, :].astype(F32)
        qm = jnp.concatenate([jnp.where(low, q, 0.0), jnp.where(low, 0.0, q)], axis=0).astype(BF16)
        s = lax.dot_general(qm, k_ref[pl.ds(start, nk), :], (((1,), (1,)), ((), ())),
                            preferred_element_type=F32)
        scores.append(s + bias_refs[sub][0].reshape(2 * nq, nk))
        vwins.append(v_ref[pl.ds(start, nk), :])
    for sub in range(NA_SUB):
        s = scores[sub]
        m = jnp.max(s, axis=-1, keepdims=True)
        p = jnp.exp2(s - m)
        l = jnp.sum(p, axis=-1, keepdims=True)
        pv = jnp.dot(p.astype(BF16), vwins[sub], preferred_element_type=F32) / l
        o_ref[sub * nq:(sub + 1) * nq, :] = jnp.where(low, pv[:nq], pv[nq:]).astype(o_ref.dtype)


def _na_bias(rpb, rows_n):
    half = NA_WIN_ROWS // 2
    pats, pat_id = [], []
    for blk in range(rows_n // 2):
        r0 = 2 * blk
        ks = min(max(r0 - half, 0), rows_n - NA_KEY_ROWS)
        starts = tuple(min(max(r0 + i - half, 0), rows_n - NA_WIN_ROWS) - ks for i in range(2))
        key = (r0 - ks, starts)
        if key not in pats:
            pats.append(key)
        pat_id.append(pats.index(key))
    wq = np.arange(GRID_W)
    wk = np.arange(GRID_W)
    col_start = np.clip(wq - NA_WIN_COLS // 2, 0, GRID_W - NA_WIN_COLS)
    col_off = wk[None, :] - col_start[:, None]
    col_valid = (col_off >= 0) & (col_off < NA_WIN_COLS)
    dc = np.clip(wk[None, :] - wq[:, None], -(NA_WIN_COLS - 1), NA_WIN_COLS - 1) + (NA_WIN_COLS - 1)
    n_dc = 2 * NA_WIN_COLS - 1
    onehot = (dc.reshape(-1)[None, :] == np.arange(n_dc)[:, None]).astype(np.float32)
    heads, n_dr = rpb.shape[0], rpb.shape[1]
    col = jnp.dot(rpb.astype(F32).reshape(heads * n_dr, n_dc), jnp.asarray(onehot),
                  precision=lax.Precision.HIGHEST).reshape(heads, n_dr, GRID_W, GRID_W)
    col = jnp.where(col_valid[None, None], col * LOG2E, NEG_BIG)
    masked_blk = jnp.full((heads, GRID_W, GRID_W), NEG_BIG, F32)
    tables = []
    for r0rel, starts in pats:
        qrows = []
        for qi in range(2):
            blks = []
            for kr in range(NA_KEY_ROWS):
                row_ok = starts[qi] <= kr < starts[qi] + NA_WIN_ROWS
                dr = kr - (r0rel + qi) + (NA_WIN_ROWS - 1)
                blks.append(col[:, dr] if row_ok else masked_blk)
            qrows.append(jnp.concatenate(blks, axis=-1))
        tables.append(jnp.concatenate(qrows, axis=1))
    return jnp.stack(tables, axis=0), np.asarray(pat_id, np.int32)


def _na_attention(hcat, rpb, batch, seq):
    t = hcat.shape[0]
    rows_n = seq // GRID_W
    nblk = rows_n // 2
    nq = 2 * GRID_W
    bias, pat_id = _na_bias(rpb, rows_n)
    nk = NA_KEY_ROWS * GRID_W
    hp = NA_HEADS // 2
    nstep = nblk // NA_SUB

    def bias_spec(sub):
        return pl.BlockSpec((1, 2, nq, nk), lambda b, h, r, pat: (pat[NA_SUB * r + sub], h, 0, 0))

    grid_spec = pltpu.PrefetchScalarGridSpec(
        num_scalar_prefetch=1,
        grid=(batch, hp, nstep),
        in_specs=[
            pl.BlockSpec((NA_SUB * nq, LANES), lambda b, h, r, pat: (b * nstep + r, h)),
            pl.BlockSpec((seq, LANES), lambda b, h, r, pat: (b, hp + h)),
            pl.BlockSpec((seq, LANES), lambda b, h, r, pat: (b, 2 * hp + h)),
        ] + [bias_spec(sub) for sub in range(NA_SUB)],
        out_specs=pl.BlockSpec((NA_SUB * nq, LANES), lambda b, h, r, pat: (b * nstep + r, h)),
    )
    return pl.pallas_call(
        functools.partial(_na_body, rows_n=rows_n),
        grid_spec=grid_spec,
        out_shape=jax.ShapeDtypeStruct((t, NA_HEADS * HEAD_DIM), BF16),
        compiler_params=_cparams(("parallel", "parallel", "arbitrary")),
        name="na_attn",
    )(jnp.asarray(pat_id), hcat, hcat, hcat, *([bias] * NA_SUB))


def _diff_body(lam_ref, q_ref, k_ref, v_ref, g_ref, o_ref, vt_ref, acc_ref, sta_ref, stb_ref, pa_ref, pb_ref,
               *, tk, lambda_init):
    j = pl.program_id(2)
    tq = q_ref.shape[0]
    seq = k_ref.shape[0]
    nkv = seq // tk

    @pl.when(j == 0)
    def _():
        ones = jnp.ones((DIFF_ONES_ROWS, tk), BF16)
        for c in range(nkv):
            vt_ref[c, :LANES, :] = v_ref[c * tk:(c + 1) * tk, :].astype(F32).T.astype(BF16)
            vt_ref[c, LANES:, :] = ones

    q = q_ref[...].astype(F32)
    lane = lax.broadcasted_iota(I32, (tq, LANES), 1)
    q1 = jnp.where(lane < HEAD_DIM, q, 0.0).T
    q2 = jnp.where(lane >= HEAD_DIM, q, 0.0).T
    rhs = jnp.concatenate([q1, q2], axis=1).astype(BF16)

    def scores(i, dst_ref):
        start = pl.multiple_of(i * tk, tk)
        dst_ref[...] = jnp.dot(k_ref[pl.ds(start, tk), :], rhs, preferred_element_type=F32)

    def softmax(src_ref, dst_ref, m):
        st = src_ref[...]
        m_new = jnp.maximum(m, jnp.max(st, axis=0, keepdims=True))
        dst_ref[...] = jnp.exp2((st - m_new).astype(BF16))
        return m_new, jnp.exp2(m - m_new)

    def weighted_values(i, p_ref, alpha):
        pv = jnp.dot(vt_ref[i], p_ref[...], preferred_element_type=F32)
        acc_ref[...] = alpha * acc_ref[...] + pv

    acc_ref[...] = jnp.zeros_like(acc_ref)
    scores(0, sta_ref)
    scores(1, stb_ref)
    m, a0 = softmax(sta_ref, pa_ref, jnp.full((1, 2 * tq), NEG_BIG, F32))

    def pair(ii, carry):
        m, a0 = carry
        i = 2 * ii
        scores(i + 2, sta_ref)
        m, a1 = softmax(stb_ref, pb_ref, m)
        weighted_values(i, pa_ref, a0)
        scores(i + 3, stb_ref)
        m, a0 = softmax(sta_ref, pa_ref, m)
        weighted_values(i + 1, pb_ref, a1)
        return m, a0

    m, a0 = lax.fori_loop(0, (nkv - 2) // 2, pair, (m, a0))
    m, a1 = softmax(stb_ref, pb_ref, m)
    weighted_values(nkv - 2, pa_ref, a0)
    weighted_values(nkv - 1, pb_ref, a1)
    acc = acc_ref[...]
    ot = acc[:LANES, :] / acc[LANES:LANES + 1, :]
    dt = ot[:, :tq] - lam_ref[0] * ot[:, tq:]
    ms = jnp.mean(dt * dt, axis=0, keepdims=True)
    y = dt * lax.rsqrt(ms + LN_EPS) * g_ref[...] * (1.0 - lambda_init)
    o_ref[...] = y.T.astype(o_ref.dtype)


def _diff_attention(hcat, lam, subln_g, lambda_init, batch, seq, tq=2048, tk=512):
    t = hcat.shape[0]
    tq = min(tq, seq)
    tk = min(tk, seq)
    nq = seq // tq
    qoff = 3 * NA_HEADS * HEAD_DIM // LANES
    koff = qoff + DIFF_HEADS
    voff = koff + DIFF_HEADS
    grid_spec = pltpu.PrefetchScalarGridSpec(
        num_scalar_prefetch=1,
        grid=(batch, DIFF_HEADS, nq),
        in_specs=[
            pl.BlockSpec((tq, LANES), lambda b, h, j, lam: (b * nq + j, qoff + h)),
            pl.BlockSpec((seq, LANES), lambda b, h, j, lam: (b, koff + h)),
            pl.BlockSpec((seq, LANES), lambda b, h, j, lam: (b, voff + h)),
            pl.BlockSpec((LANES, 1), lambda b, h, j, lam: (0, 0)),
        ],
        out_specs=pl.BlockSpec((tq, LANES), lambda b, h, j, lam: (b * nq + j, h)),
        scratch_shapes=[pltpu.VMEM((seq // tk, LANES + DIFF_ONES_ROWS, tk), BF16),
                        pltpu.VMEM((LANES + DIFF_ONES_ROWS, 2 * tq), F32),
                        pltpu.VMEM((tk, 2 * tq), F32),
                        pltpu.VMEM((tk, 2 * tq), F32),
                        pltpu.VMEM((tk, 2 * tq), BF16),
                        pltpu.VMEM((tk, 2 * tq), BF16)],
    )
    return pl.pallas_call(
        functools.partial(_diff_body, tk=tk, lambda_init=lambda_init),
        grid_spec=grid_spec,
        out_shape=jax.ShapeDtypeStruct((t, DIFF_HEADS * 2 * HEAD_DIM), BF16),
        compiler_params=_cparams(("parallel", "parallel", "arbitrary")),
        name="diff_attn",
    )(lam, hcat, hcat, hcat, subln_g.astype(F32).reshape(LANES, 1))


def _swa_body(sink_ref, q_ref, kv_ref, o_ref, kvt_ref):
    kvh = pl.program_id(1)
    nq = SWA_BLOCK
    nsub = q_ref.shape[0] // nq
    seq = kv_ref.shape[0]
    nblk = seq // SWA_BLOCK
    wb = min(3, nblk)
    nk = wb * SWA_BLOCK

    @pl.when(pl.program_id(2) == 0)
    def _():
        for c in range(nblk):
            kvt_ref[c] = kv_ref[c * SWA_BLOCK:(c + 1) * SWA_BLOCK, :].astype(F32).T.astype(BF16)

    group = SWA_Q_HEADS // SWA_KV_HEADS
    half = LANES // 2

    def swap_halves(a):
        return jnp.concatenate([a[half:], a[:half]], axis=0)

    top = lax.broadcasted_iota(I32, (LANES, nq), 0) < half
    sink = jnp.concatenate([jnp.full((1, nq), sink_ref[kvh * group + g], F32) for g in range(group)], axis=1)

    scores, kvts = [], []
    for sub in range(nsub):
        n = nsub * pl.program_id(2) + sub
        b0 = jnp.clip(n - 1, 0, nblk - wb)
        start = pl.multiple_of(b0 * SWA_BLOCK, SWA_BLOCK)
        cols = []
        for c in range(group // 2):
            qt = q_ref[sub * nq:(sub + 1) * nq, c * LANES:(c + 1) * LANES].astype(F32).T
            cols.append(jnp.where(top, qt, 0.0))
            cols.append(jnp.where(top, swap_halves(qt), 0.0))
        rhs = jnp.concatenate(cols, axis=1).astype(BF16)
        st = jnp.dot(kv_ref[pl.ds(start, nk), :], rhs, preferred_element_type=F32)
        kpos = start + lax.broadcasted_iota(I32, (nk, nq), 0)
        qpos = n * SWA_BLOCK + lax.broadcasted_iota(I32, (nk, nq), 1)
        mask = jnp.where(jnp.abs(kpos - qpos) <= SWA_WINDOW, 0.0, NEG_BIG)
        scores.append(st + jnp.concatenate([mask] * group, axis=1))
        kvts.append(jnp.concatenate([kvt_ref[b0 + w] for w in range(wb)], axis=1))
    for sub in range(nsub):
        st = scores[sub]
        m = jnp.maximum(jnp.max(st, axis=0, keepdims=True), sink)
        e = jnp.exp2(st - m)
        den = jnp.sum(e, axis=0, keepdims=True) + jnp.exp2(sink - m)
        ot = jnp.dot(kvts[sub], e.astype(BF16), preferred_element_type=F32) / den
        for c in range(group // 2):
            even = ot[:, 2 * c * nq:(2 * c + 1) * nq]
            odd = ot[:, (2 * c + 1) * nq:(2 * c + 2) * nq]
            blk = jnp.where(top, swap_halves(even), odd)
            o_ref[sub * nq:(sub + 1) * nq, c * LANES:(c + 1) * LANES] = blk.T.astype(o_ref.dtype)


def _swa_attention(hcat, sinks, batch, seq):
    t = hcat.shape[0]
    nb = seq // SWA_BLOCK
    group = SWA_Q_HEADS // SWA_KV_HEADS
    qw = group * HEAD_DIM
    kvoff = SWA_Q_HEADS * HEAD_DIM // LANES
    nsub = min(SWA_SUB, nb)
    nstep = nb // nsub
    grid_spec = pltpu.PrefetchScalarGridSpec(
        num_scalar_prefetch=1,
        grid=(batch, SWA_KV_HEADS, nstep),
        in_specs=[
            pl.BlockSpec((nsub * SWA_BLOCK, qw), lambda b, h, n, s: (b * nstep + n, h)),
            pl.BlockSpec((seq, LANES), lambda b, h, n, s: (b, kvoff + h)),
        ],
        out_specs=pl.BlockSpec((nsub * SWA_BLOCK, qw), lambda b, h, n, s: (b * nstep + n, h)),
        scratch_shapes=[pltpu.VMEM((nb, LANES, SWA_BLOCK), BF16)],
    )
    return pl.pallas_call(
        _swa_body,
        grid_spec=grid_spec,
        out_shape=jax.ShapeDtypeStruct((t, SWA_Q_HEADS * HEAD_DIM), BF16),
        compiler_params=_cparams(("parallel", "parallel", "arbitrary")),
        name="swa_attn",
    )(sinks.astype(F32) * LOG2E, hcat, hcat)


def _outproj_body(*refs, n_in):
    a_refs = refs[:n_in]
    w_ref, x_ref, g_ref, b_ref, whi_ref, wlo_ref, rbias_ref = refs[n_in:n_in + 7]
    o_ref, opk_ref, idx_ref, gate_ref, pos_ref, cnt_ref, base_ref = refs[n_in + 7:]
    acc = None
    off = 0
    for a_ref in a_refs:
        ka = a_ref.shape[1]
        d = jnp.dot(a_ref[...], w_ref[off:off + ka, :], preferred_element_type=F32)
        acc = d if acc is None else acc + d
        off += ka
    y = DN_ALPHA * x_ref[...] + acc
    out = _layer_norm(y, g_ref[...], b_ref[...])
    o_ref[...] = out
    packed = _pack_rows(out)
    for c in range(PACK_SPLIT):
        opk_ref[c] = packed[:, c * LANES:(c + 1) * LANES]
    _route(out, whi_ref, wlo_ref, rbias_ref, idx_ref, gate_ref, pos_ref, cnt_ref, base_ref)


def _outproj_ln_route(acts, w_bf, x2, g, b, router_w, router_bias, tm=512):
    t, d = x2.shape
    tm = min(tm, t)
    wt = router_w.astype(F32).T
    whi = wt.astype(BF16)
    wlo = (wt - whi.astype(F32)).astype(BF16)
    in_specs = [pl.BlockSpec((tm, a.shape[1]), lambda i: (i, 0)) for a in acts]
    in_specs += [pl.BlockSpec(w_bf.shape, lambda i: (0, 0)),
                 pl.BlockSpec((tm, d), lambda i: (i, 0)),
                 pl.BlockSpec((1, d), lambda i: (0, 0)),
                 pl.BlockSpec((1, d), lambda i: (0, 0)),
                 pl.BlockSpec((N_EXPERTS, d), lambda i: (0, 0)),
                 pl.BlockSpec((N_EXPERTS, d), lambda i: (0, 0)),
                 pl.BlockSpec((N_EXPERTS, 1), lambda i: (0, 0))]
    return pl.pallas_call(
        functools.partial(_outproj_body, n_in=len(acts)),
        grid=(t // tm,),
        in_specs=in_specs,
        out_specs=[pl.BlockSpec((tm, d), lambda i: (i, 0)),
                   pl.BlockSpec((PACK_SPLIT, tm, LANES), lambda i: (0, i, 0)),
                   pl.BlockSpec((TOP_K, tm), lambda i: (0, i)),
                   pl.BlockSpec((TOP_K, tm), lambda i: (0, i)),
                   pl.BlockSpec((TOP_K, tm), lambda i: (0, i)),
                   pl.BlockSpec((N_EXPERTS, 1), lambda i: (0, 0))],
        out_shape=[jax.ShapeDtypeStruct((t, d), F32),
                   jax.ShapeDtypeStruct((PACK_SPLIT, t, LANES), I32),
                   jax.ShapeDtypeStruct((TOP_K, t), I32),
                   jax.ShapeDtypeStruct((TOP_K, t), F32),
                   jax.ShapeDtypeStruct((TOP_K, t), I32),
                   jax.ShapeDtypeStruct((N_EXPERTS, 1), F32)],
        scratch_shapes=[pltpu.VMEM((N_EXPERTS, 1), F32)],
        compiler_params=_cparams(("arbitrary",)),
        name="outproj_ln_route",
    )(*acts, w_bf, x2, g.astype(F32).reshape(1, d), b.astype(F32).reshape(1, d),
      whi, wlo, router_bias.astype(F32).reshape(N_EXPERTS, 1))


def _route(h, whi_ref, wlo_ref, bias_ref, idx_ref, gate_ref, pos_ref, cnt_ref, base_ref):
    i = pl.program_id(0)
    tm = h.shape[0]
    gsz = N_EXPERTS // N_GROUPS

    @pl.when(i == 0)
    def _():
        base_ref[...] = jnp.zeros_like(base_ref)

    h_hi = h.astype(BF16)
    h_lo = (h - h_hi.astype(F32)).astype(BF16)
    dn = (((1,), (1,)), ((), ()))
    whi = whi_ref[...]
    logits = (lax.dot_general(whi, h_hi, dn, preferred_element_type=F32)
              + lax.dot_general(whi, h_lo, dn, preferred_element_type=F32)
              + lax.dot_general(wlo_ref[...], h_hi, dn, preferred_element_type=F32))
    scores = 1.0 / (1.0 + jnp.exp(-logits))
    choice = scores + bias_ref[...]

    iota_g = lax.broadcasted_iota(I32, (gsz, tm), 0)
    gscore = []
    for g in range(N_GROUPS):
        cg = choice[g * gsz:(g + 1) * gsz, :]
        m1 = jnp.max(cg, axis=0, keepdims=True)
        first = jnp.min(jnp.where(cg == m1, iota_g, gsz), axis=0, keepdims=True)
        m2 = jnp.max(jnp.where(iota_g == first, -jnp.inf, cg), axis=0, keepdims=True)
        gscore.append(m1 + m2)
    pieces = []
    for g in range(N_GROUPS):
        rank = jnp.zeros((1, tm), I32)
        for o in range(N_GROUPS):
            if o == g:
                continue
            beats = (gscore[o] > gscore[g]) if o > g else (gscore[o] >= gscore[g])
            rank = rank + beats.astype(I32)
        keep = rank < TOPK_GROUPS
        pieces.append(jnp.where(keep, choice[g * gsz:(g + 1) * gsz, :], -jnp.inf))
    masked = jnp.concatenate(pieces, axis=0)

    iota_e = lax.broadcasted_iota(I32, (N_EXPERTS, tm), 0)
    sel_all = jnp.zeros((N_EXPERTS, tm), F32)
    idxs, gates, sels = [], [], []
    for _ in range(TOP_K):
        mx = jnp.max(masked, axis=0, keepdims=True)
        idx = jnp.min(jnp.where(masked == mx, iota_e, N_EXPERTS), axis=0, keepdims=True)
        sel = iota_e == idx
        gates.append(jnp.sum(jnp.where(sel, scores, 0.0), axis=0, keepdims=True))
        masked = jnp.where(sel, -jnp.inf, masked)
        sel_all = sel_all + sel.astype(F32)
        idxs.append(idx)
        sels.append(sel)
    gsum = gates[0]
    for gk in gates[1:]:
        gsum = gsum + gk
    gate_ref[...] = jnp.concatenate(gates, axis=0) / gsum * ROUTED_SCALE
    idx_ref[...] = jnp.concatenate(idxs, axis=0)

    tri = (lax.broadcasted_iota(I32, (tm, tm), 0) < lax.broadcasted_iota(I32, (tm, tm), 1))
    cum = jnp.dot(sel_all.astype(BF16), tri.astype(F32).astype(BF16), preferred_element_type=F32)
    tot = cum + base_ref[...]
    pos = [jnp.sum(jnp.where(sel, tot, 0.0), axis=0, keepdims=True) for sel in sels]
    pos_ref[...] = jnp.concatenate(pos, axis=0).astype(I32)
    base_ref[...] = base_ref[...] + jnp.sum(sel_all, axis=1, keepdims=True)
    cnt_ref[...] = base_ref[...]


def _expert_body(be_ref, na_ref, bv_ref, ord_ref, nxt_ref, x_ref, wg_hbm, wu_hbm, wd_hbm, y_ref,
                 wgu, wdb, wg_buf, wu_buf, wd_buf, sem):
    i = pl.program_id(0)
    bm = x_ref.shape[1]
    ff = wd_hbm.shape[1]
    hm = bm // EXPERT_CHAINS

    def weight_copies(e, slot):
        return [pltpu.make_async_copy(hbm.at[e], buf.at[slot], sem.at[slot])
                for hbm, buf in ((wg_hbm, wg_buf), (wu_hbm, wu_buf), (wd_hbm, wd_buf))]

    @pl.when(i < na_ref[0])
    def _():
        prev = be_ref[jnp.maximum(i - 1, 0)]
        slot = ord_ref[i] % 2

        @pl.when(i == 0)
        def _():
            for cp in weight_copies(be_ref[0], 0):
                cp.start()

        @pl.when((i == 0) | (be_ref[i] != prev))
        def _():
            for cp in weight_copies(be_ref[i], slot):
                cp.wait()
            wgu[:, :ff] = wg_buf[slot].astype(BF16)
            wgu[:, ff:] = wu_buf[slot].astype(BF16)
            wdb[...] = wd_buf[slot].astype(BF16)

            @pl.when(nxt_ref[i] >= 0)
            def _():
                for cp in weight_copies(nxt_ref[i], 1 - slot):
                    cp.start()

        valid = bv_ref[i]
        row = lax.broadcasted_iota(I32, (hm, LANES), 0)
        def run(n_chains):
            gus = []
            for hb in range(n_chains):
                rows = slice(hb * hm, (hb + 1) * hm)
                keep = row < valid - hb * hm
                x = _unpack_rows(lambda c: jnp.where(keep, x_ref[c, rows, :], 0)).astype(BF16)
                gus.append(jnp.dot(x, wgu[...], preferred_element_type=F32))
            ys = []
            for gu in gus:
                hmid = (_silu(gu[:, :ff]) * gu[:, ff:]).astype(BF16)
                ys.append(jnp.dot(hmid, wdb[...], preferred_element_type=F32))
            for hb, y in enumerate(ys):
                packed = _pack_rows(y)
                for c in range(PACK_SPLIT):
                    y_ref[c, hb * hm:(hb + 1) * hm, :] = packed[:, c * LANES:(c + 1) * LANES]
            if n_chains < EXPERT_CHAINS:
                y_ref[:, n_chains * hm:, :] = jnp.zeros((PACK_SPLIT, bm - n_chains * hm, LANES), I32)

        for n_chains in range(1, EXPERT_CHAINS + 1):
            lo = (n_chains - 1) * hm
            cond = valid > lo if n_chains == EXPERT_CHAINS else (valid > lo) & (valid <= lo + hm)
            pl.when(cond)(functools.partial(run, n_chains))


def _expert_matmul(xs, blk_expert, n_active, blk_valid, w_gate, w_up, w_down):
    n_slots = xs.shape[1]
    bm = EXPERT_BLOCK
    n_blocks = n_slots // bm
    d, ff = w_gate.shape[1], w_gate.shape[2]

    blk = jnp.arange(n_blocks, dtype=I32)
    change = jnp.concatenate([jnp.zeros((1,), I32), (blk_expert[1:] != blk_expert[:-1]).astype(I32)])
    ordinal = jnp.sum(jnp.where(blk[None, :] <= blk[:, None], change[None, :], 0), axis=1).astype(I32)
    later = (ordinal[None, :] == ordinal[:, None] + 1) & (blk[None, :] < n_active[0])
    nxt = jnp.max(jnp.where(later, blk_expert[None, :], -1), axis=1).astype(I32)

    def row_map(i, be, na, bv, od, nx):
        return (0, jnp.minimum(i, na[0] - 1), 0)

    grid_spec = pltpu.PrefetchScalarGridSpec(
        num_scalar_prefetch=5,
        grid=(n_blocks,),
        in_specs=[pl.BlockSpec((PACK_SPLIT, bm, LANES), row_map),
                  pl.BlockSpec(memory_space=pl.ANY),
                  pl.BlockSpec(memory_space=pl.ANY),
                  pl.BlockSpec(memory_space=pl.ANY)],
        out_specs=pl.BlockSpec((PACK_SPLIT, bm, LANES), row_map),
        scratch_shapes=[pltpu.VMEM((d, 2 * ff), BF16), pltpu.VMEM((ff, d), BF16),
                        pltpu.VMEM((2, d, ff), F32), pltpu.VMEM((2, d, ff), F32), pltpu.VMEM((2, ff, d), F32),
                        pltpu.SemaphoreType.DMA((2,))],
    )
    return pl.pallas_call(
        _expert_body,
        grid_spec=grid_spec,
        out_shape=jax.ShapeDtypeStruct((PACK_SPLIT, n_slots, LANES), I32),
        compiler_params=_cparams(("arbitrary",)),
        name="expert_mlp",
    )(blk_expert, n_active, blk_valid, ordinal, nxt, xs, w_gate, w_up, w_down)


def _slot_body(idx_ref, pos_ref, ps_ref, slot_ref, *, n_slots):
    tm = idx_ref.shape[1]
    iota_e = lax.broadcasted_iota(I32, (N_EXPERTS, tm), 0)
    ps = ps_ref[...]
    rows = []
    for k in range(TOP_K):
        start = jnp.sum(jnp.where(iota_e == idx_ref[k:k + 1, :], ps, 0.0), axis=0, keepdims=True)
        rows.append(start.astype(I32) + pos_ref[k:k + 1, :])
    slot = jnp.concatenate(rows, axis=0)
    for c in range(PACK_SPLIT):
        slot_ref[c * TOP_K:(c + 1) * TOP_K, :] = slot + c * n_slots


def _slots(idx, pos, pad_start, n_slots, tm=512):
    t = idx.shape[1]
    tm = min(tm, t)
    return pl.pallas_call(
        functools.partial(_slot_body, n_slots=n_slots),
        grid=(t // tm,),
        in_specs=[pl.BlockSpec((TOP_K, tm), lambda i: (0, i)),
                  pl.BlockSpec((TOP_K, tm), lambda i: (0, i)),
                  pl.BlockSpec((N_EXPERTS, 1), lambda i: (0, 0))],
        out_specs=pl.BlockSpec((PACK_SPLIT * TOP_K, tm), lambda i: (0, i)),
        out_shape=jax.ShapeDtypeStruct((PACK_SPLIT * TOP_K, t), I32),
        compiler_params=_cparams(("parallel",)),
        name="slots",
    )(idx, pos, pad_start.astype(F32).reshape(N_EXPERTS, 1))


def _sc_mesh():
    return plsc.VectorSubcoreMesh(core_axis_name="core", subcore_axis_name="subcore",
                                  num_cores=SC_CORES, num_subcores=SC_SUBCORES)


def _sc_scatter_rows(rows, idx, n_out):
    t = idx.shape[1]
    nj = t // SC_WINDOW

    @functools.partial(pl.kernel, out_type=jax.ShapeDtypeStruct((n_out, LANES), I32),
                       mesh=_sc_mesh(), scratch_types=[], name="sc_dispatch")
    def run(rows_hbm, idx_hbm, out_hbm):
        def body(rows_vmem, idx_vmem):
            for k in range(TOP_K):
                pltpu.sync_copy(rows_vmem, out_hbm.at[idx_vmem.at[k]])

        pltpu.emit_pipeline(
            body,
            grid=(rows.shape[0] // SC_WINDOW,),
            in_specs=[pl.BlockSpec((SC_WINDOW, LANES), lambda s: (s, 0)),
                      pl.BlockSpec((TOP_K, SC_WINDOW), lambda s: (s // nj, s % nj))],
            out_specs=[],
            core_axis_name=("core", "subcore"),
            dimension_semantics=(pltpu.PARALLEL,),
        )(rows_hbm, idx_hbm)

    return run(rows, idx)


def _sc_gather_rows(src, idx):
    nr, t = idx.shape
    nj = t // SC_WINDOW

    @functools.partial(pl.kernel, out_type=jax.ShapeDtypeStruct((nr * t, LANES), I32),
                       mesh=_sc_mesh(), scratch_types=[], name="sc_combine")
    def run(src_hbm, idx_hbm, out_hbm):
        def body(idx_vmem, out_vmem):
            pltpu.sync_copy(src_hbm.at[idx_vmem.at[0]], out_vmem)

        pltpu.emit_pipeline(
            body,
            grid=(nr * nj,),
            in_specs=[pl.BlockSpec((1, SC_WINDOW), lambda s: (s // nj, s % nj))],
            out_specs=[pl.BlockSpec((SC_WINDOW, LANES), lambda s: (s, 0))],
            core_axis_name=("core", "subcore"),
            dimension_semantics=(pltpu.PARALLEL,),
        )(idx_hbm, out_hbm)

    return run(src, idx)


def _combine_body(h_ref, yg_ref, gt_ref, sg_ref, su_ref, sd_ref, g_ref, b_ref, *rest, proj):
    h = h_ref[...]
    hb = h.astype(BF16)
    a = jnp.dot(hb, sg_ref[...], preferred_element_type=F32)
    u = jnp.dot(hb, su_ref[...], preferred_element_type=F32)
    moe = jnp.dot((_silu(a) * u).astype(BF16), sd_ref[...], preferred_element_type=F32)
    gt = gt_ref[...]
    for k in range(TOP_K):
        moe = moe + gt[:, k:k + 1] * _unpack_rows(lambda c, k=k: yg_ref[c, k])
    out = _layer_norm(DN_ALPHA * h + moe, g_ref[...], b_ref[...])
    if proj is None:
        rest[0][...] = out
    else:
        w_ref, tab_ref, o_ref, hcat_ref = rest
        o_ref[...] = out
        _project_rope(out.astype(BF16), w_ref, tab_ref, hcat_ref, *proj)


def _combine_ln(h, yg, gates_t, sh_gate, sh_up, sh_down, g, b, next_proj=None, tm=256):
    t, d = h.shape
    tm = min(tm, t) if next_proj is None else min(tm, next_proj[3])
    ff = sh_gate.shape[1]
    in_specs = [pl.BlockSpec((tm, d), lambda i: (i, 0)),
                pl.BlockSpec((PACK_SPLIT, TOP_K, tm, LANES), lambda i: (0, 0, i, 0)),
                pl.BlockSpec((tm, TOP_K), lambda i: (i, 0)),
                pl.BlockSpec((d, ff), lambda i: (0, 0)),
                pl.BlockSpec((d, ff), lambda i: (0, 0)),
                pl.BlockSpec((ff, d), lambda i: (0, 0)),
                pl.BlockSpec((1, d), lambda i: (0, 0)),
                pl.BlockSpec((1, d), lambda i: (0, 0))]
    out_specs = [pl.BlockSpec((tm, d), lambda i: (i, 0))]
    out_shape = [jax.ShapeDtypeStruct((t, d), F32)]
    operands = [h, yg, gates_t, sh_gate.astype(BF16), sh_up.astype(BF16), sh_down.astype(BF16),
                g.astype(F32).reshape(1, d), b.astype(F32).reshape(1, d)]
    proj = None
    if next_proj is not None:
        w_bf, tab, rope_kind, seq = next_proj
        sb = seq // tm
        n_out = w_bf.shape[1]
        in_specs += [pl.BlockSpec((d, n_out), lambda i: (0, 0)),
                     pl.BlockSpec((tm, tab.shape[1]), lambda i: (i % sb, 0))]
        out_specs.append(pl.BlockSpec((tm, n_out), lambda i: (i, 0)))
        out_shape.append(jax.ShapeDtypeStruct((t, n_out), BF16))
        operands += [w_bf, tab]
        proj = (PROJ_CHUNK, tuple(rope_kind))
    res = pl.pallas_call(
        functools.partial(_combine_body, proj=proj),
        grid=(t // tm,),
        in_specs=in_specs,
        out_specs=out_specs,
        out_shape=out_shape,
        compiler_params=_cparams(("parallel",)),
        name="combine_ln",
    )(*operands)
    return res if next_proj is not None else res[0]


def _mixer_out_and_moe(acts, w_out, x2, ln1_g, ln1_b, router_w, router_bias, w_gate, w_up, w_down,
                       sh_gate, sh_up, sh_down, g, b, next_proj=None):
    h, h_pk, idx, gates, pos, cnt = _outproj_ln_route(acts, w_out.astype(BF16), x2, ln1_g, ln1_b,
                                                      router_w, router_bias)
    t, d = h.shape
    bm = EXPERT_BLOCK
    counts = cnt[:, 0].astype(I32)
    padded = (counts + bm - 1) // bm * bm
    e_iota = jnp.arange(N_EXPERTS, dtype=I32)
    pad_end = jnp.sum(jnp.where(e_iota[None, :] <= e_iota[:, None], padded[None, :], 0), axis=1)
    pad_start = pad_end - padded
    n_blocks = t * TOP_K // bm + N_EXPERTS
    n_slots = n_blocks * bm
    blk_start = jnp.arange(n_blocks, dtype=I32) * bm
    blk_expert = jnp.minimum(
        jnp.sum((blk_start[:, None] >= pad_end[None, :]).astype(I32), axis=1), N_EXPERTS - 1)
    real_end = jnp.sum(jnp.where(e_iota[None, :] == blk_expert[:, None], (pad_start + counts)[None, :], 0), axis=1)
    blk_valid = jnp.clip(real_end - blk_start, 0, bm).astype(I32)
    n_active = (pad_end[-1:] // bm).astype(I32)

    slot4 = _slots(idx, pos, pad_start, n_slots)
    xs = _sc_scatter_rows(h_pk.reshape(PACK_SPLIT * t, LANES), slot4, PACK_SPLIT * n_slots)
    ys = _expert_matmul(xs.reshape(PACK_SPLIT, n_slots, LANES), blk_expert, n_active, blk_valid,
                        w_gate, w_up, w_down)
    yg = _sc_gather_rows(ys.reshape(PACK_SPLIT * n_slots, LANES), slot4)
    return _combine_ln(h, yg.reshape(PACK_SPLIT, TOP_K, t, LANES), gates.T, sh_gate, sh_up, sh_down, g, b,
                       next_proj=next_proj)


def kernel(x, l0_w_in, l0_w_out, l0_na_rpb, l0_diff_lq1, l0_diff_lk1, l0_diff_lq2, l0_diff_lk2, l0_diff_subln_g, l0_ln1_g, l0_ln1_b, l0_router_w, l0_router_bias, l0_expert_w_gate, l0_expert_w_up, l0_expert_w_down, l0_shared_w_gate, l0_shared_w_up, l0_shared_w_down, l0_ln2_g, l0_ln2_b, l1_w_in, l1_w_out, l1_swa_sinks, l1_ln1_g, l1_ln1_b, l1_router_w, l1_router_bias, l1_expert_w_gate, l1_expert_w_up, l1_expert_w_down, l1_shared_w_gate, l1_shared_w_up, l1_shared_w_down, l1_ln2_g, l1_ln2_b):
    batch, seq, d = x.shape
    t = batch * seq
    x2 = x.reshape(t, d).astype(F32)
    tab = _rope_table(seq)
    qscale = HEAD_DIM ** -0.5 * LOG2E
    na_w = NA_HEADS * HEAD_DIM
    dq_w = DIFF_HEADS * 2 * HEAD_DIM

    col_scale = jnp.concatenate([
        jnp.full((na_w,), qscale, F32), jnp.ones((2 * na_w,), F32),
        jnp.full((dq_w,), qscale, F32), jnp.ones((2 * dq_w,), F32)])
    w_in0 = (l0_w_in.astype(F32) * col_scale).astype(BF16)
    per = LANES
    rope0 = [0] * (3 * na_w // per) + [1] * (2 * dq_w // per) + [0] * (dq_w // per)
    hcat0 = _inproj(x2, w_in0, tab, rope0, seq)
    oa = _na_attention(hcat0, l0_na_rpb, batch, seq)
    lambda_init = 0.8 - 0.6 * math.exp(-0.3 * 0)
    lam = (jnp.exp(jnp.sum(l0_diff_lq1.astype(F32) * l0_diff_lk1.astype(F32)))
           - jnp.exp(jnp.sum(l0_diff_lq2.astype(F32) * l0_diff_lk2.astype(F32))) + lambda_init)
    od = _diff_attention(hcat0, lam.reshape(1).astype(F32), l0_diff_subln_g, lambda_init, batch, seq)
    q_w = SWA_Q_HEADS * HEAD_DIM
    kv_w = SWA_KV_HEADS * HEAD_DIM
    w1 = l1_w_in.astype(F32)
    wq = w1[:, :q_w] * qscale
    wk = w1[:, q_w:q_w + kv_w].reshape(d, SWA_KV_HEADS, HEAD_DIM)
    wv = w1[:, q_w + kv_w:].reshape(d, SWA_KV_HEADS, HEAD_DIM)
    wkv = jnp.concatenate([wk, wv], axis=-1).reshape(d, 2 * kv_w)
    w_in1 = jnp.concatenate([wq, wkv], axis=1).astype(BF16)
    rope1 = [1] * (q_w // per) + [2] * (2 * kv_w // per)

    x2, hcat1 = _mixer_out_and_moe(
        [oa, od], l0_w_out, x2, l0_ln1_g, l0_ln1_b, l0_router_w, l0_router_bias,
        l0_expert_w_gate, l0_expert_w_up, l0_expert_w_down,
        l0_shared_w_gate, l0_shared_w_up, l0_shared_w_down, l0_ln2_g, l0_ln2_b,
        next_proj=(w_in1, tab, rope1, seq))

    o1 = _swa_attention(hcat1, l1_swa_sinks, batch, seq)
    x2 = _mixer_out_and_moe(
        [o1], l1_w_out, x2, l1_ln1_g, l1_ln1_b, l1_router_w, l1_router_bias,
        l1_expert_w_gate, l1_expert_w_up, l1_expert_w_down,
        l1_shared_w_gate, l1_shared_w_up, l1_shared_w_down, l1_ln2_g, l1_ln2_b)
    return x2.reshape(batch, seq, d).astype(x.dtype)
```

```python
import functools
import math

import numpy as np
import jax
import jax.numpy as jnp
from jax import lax
from jax.experimental import pallas as pl
from jax.experimental.pallas import tpu as pltpu
from jax.experimental.pallas import tpu_sc as plsc

F32 = jnp.float32
BF16 = jnp.bfloat16
I32 = jnp.int32

HEAD_DIM = 64
GRID_W = 64
NA_HEADS = 8
NA_WIN_ROWS = 8
NA_WIN_COLS = 16
DIFF_HEADS = 4
SWA_Q_HEADS = 16
SWA_KV_HEADS = 4
SWA_WINDOW = 128
SWA_BLOCK = 128
ROPE_THETA = 10000.0
N_EXPERTS = 256
TOP_K = 8
N_GROUPS = 8
TOPK_GROUPS = 4
ROUTED_SCALE = 2.5
LN_EPS = 1e-5
DEPTH = 2
DN_ALPHA = (2 * DEPTH) ** 0.25

LOG2E = 1.4426950408889634
NEG_BIG = -3.0e38
LANES = 128
NA_KEY_ROWS = NA_WIN_ROWS + 1
PROJ_CHUNK = 256
SWA_SUB = 8
NA_SUB = 8
EXPERT_BLOCK = 512
EXPERT_CHAINS = 2
DIFF_ONES_ROWS = 16
PACK_SPLIT = 4
SC_CORES = 2
SC_SUBCORES = 16
SC_WINDOW = 128
VMEM_LIMIT = 56 * 1024 * 1024


def _cparams(sem):
    return pltpu.CompilerParams(dimension_semantics=sem, vmem_limit_bytes=VMEM_LIMIT)


def _silu(x):
    return x / (1.0 + jnp.exp(-x))


def _pack_rows(y):
    half = y.shape[1] // 2
    bits = pltpu.bitcast(y.astype(BF16).astype(F32), I32)
    lo = lax.shift_right_logical(bits[:, :half], 16)
    hi = bits[:, half:] & jnp.int32(-65536)
    return hi | lo


def _unpack_words(w):
    lo = pltpu.bitcast(lax.shift_left(w, 16), F32)
    hi = pltpu.bitcast(w & jnp.int32(-65536), F32)
    return lo, hi


def _unpack_rows(ref_at):
    los, his = [], []
    for c in range(PACK_SPLIT):
        lo, hi = _unpack_words(ref_at(c))
        los.append(lo)
        his.append(hi)
    return jnp.concatenate(los + his, axis=1)


def _layer_norm(y, g, b):
    mu = jnp.mean(y, axis=-1, keepdims=True)
    yc = y - mu
    var = jnp.mean(yc * yc, axis=-1, keepdims=True)
    return yc * lax.rsqrt(var + LN_EPS) * g + b


def _inproj_body(x_ref, w_ref, tab_ref, o_ref, *, chunk, rope_kind):
    _project_rope(x_ref[...].astype(BF16), w_ref, tab_ref, o_ref, chunk, rope_kind)


def _project_rope(x, w_ref, tab_ref, o_ref, chunk, rope_kind):
    tm = x.shape[0]
    n_out = w_ref.shape[1]
    lane = lax.broadcasted_iota(I32, (tm, LANES), 1)
    first_half = (lane % HEAD_DIM) < (HEAD_DIM // 2)
    per = chunk // LANES
    for c in range(n_out // chunk):
        h = jnp.dot(x, w_ref[:, c * chunk:(c + 1) * chunk], preferred_element_type=F32)
        kinds = rope_kind[c * per:(c + 1) * per]
        if any(kinds):
            pieces = []
            for j, kind in enumerate(kinds):
                t = h[:, j * LANES:(j + 1) * LANES]
                if kind:
                    base = (kind - 1) * 2 * LANES
                    cos = tab_ref[:, base:base + LANES]
                    sin = tab_ref[:, base + LANES:base + 2 * LANES]
                    rot = jnp.where(first_half, pltpu.roll(t, LANES - HEAD_DIM // 2, 1),
                                    pltpu.roll(t, HEAD_DIM // 2, 1))
                    t = t * cos + rot * sin
                pieces.append(t)
            h = jnp.concatenate(pieces, axis=1)
        o_ref[:, c * chunk:(c + 1) * chunk] = h.astype(o_ref.dtype)


def _inproj(x2, w_bf, tab, rope_kind, seq, tm=512):
    t, d = x2.shape
    n_out = w_bf.shape[1]
    tm = min(tm, seq)
    sb = seq // tm
    return pl.pallas_call(
        functools.partial(_inproj_body, chunk=PROJ_CHUNK, rope_kind=tuple(rope_kind)),
        grid=(t // tm,),
        in_specs=[pl.BlockSpec((tm, d), lambda i: (i, 0)),
                  pl.BlockSpec((d, n_out), lambda i: (0, 0)),
                  pl.BlockSpec((tm, tab.shape[1]), lambda i: (i % sb, 0))],
        out_specs=pl.BlockSpec((tm, n_out), lambda i: (i, 0)),
        out_shape=jax.ShapeDtypeStruct((t, n_out), BF16),
        compiler_params=_cparams(("parallel",)),
        name="inproj",
    )(x2, w_bf, tab)


def _rope_table(seq):
    inv = 1.0 / (ROPE_THETA ** (jnp.arange(0, HEAD_DIM, 2, dtype=F32) / HEAD_DIM))
    ang = jnp.arange(seq, dtype=F32)[:, None] * inv[None, :]
    cos = jnp.cos(ang)
    sin = jnp.sin(ang)
    cos64 = jnp.concatenate([cos, cos], axis=-1)
    sin64 = jnp.concatenate([-sin, sin], axis=-1)
    one = jnp.ones((seq, HEAD_DIM), F32)
    zero = jnp.zeros((seq, HEAD_DIM), F32)
    return jnp.concatenate([cos64, cos64, sin64, sin64, cos64, one, sin64, zero], axis=-1)


def _na_body(pat_ref, q_ref, k_ref, v_ref, *rest, rows_n):
    del pat_ref
    bias_refs, o_ref = rest[:NA_SUB], rest[NA_SUB]
    nq = q_ref.shape[0] // NA_SUB
    nk = NA_KEY_ROWS * GRID_W
    low = lax.broadcasted_iota(I32, (nq, LANES), 1) < HEAD_DIM
    scores, vwins = [], []
    for sub in range(NA_SUB):
        r = NA_SUB * pl.program_id(2) + sub
        ks = jnp.clip(2 * r - NA_WIN_ROWS // 2, 0, rows_n - NA_KEY_ROWS)
        start = pl.multiple_of(ks * GRID_W, GRID_W)
        q = q_ref[sub * nq:(sub + 1) * nq, :].astype(F32)
        qm = jnp.concatenate([jnp.where(low, q, 0.0), jnp.where(low, 0.0, q)], axis=0).astype(BF16)
        s = lax.dot_general(qm, k_ref[pl.ds(start, nk), :], (((1,), (1,)), ((), ())),
                            preferred_element_type=F32)
        scores.append(s + bias_refs[sub][0].reshape(2 * nq, nk))
        vwins.append(v_ref[pl.ds(start, nk), :])
    for sub in range(NA_SUB):
        s = scores[sub]
        m = jnp.max(s, axis=-1, keepdims=True)
        p = jnp.exp2(s - m)
        l = jnp.sum(p, axis=-1, keepdims=True)
        pv = jnp.dot(p.astype(BF16), vwins[sub], preferred_element_type=F32) / l
        o_ref[sub * nq:(sub + 1) * nq, :] = jnp.where(low, pv[:nq], pv[nq:]).astype(o_ref.dtype)


def _na_bias(rpb, rows_n):
    half = NA_WIN_ROWS // 2
    pats, pat_id = [], []
    for blk in range(rows_n // 2):
        r0 = 2 * blk
        ks = min(max(r0 - half, 0), rows_n - NA_KEY_ROWS)
        starts = tuple(min(max(r0 + i - half, 0), rows_n - NA_WIN_ROWS) - ks for i in range(2))
        key = (r0 - ks, starts)
        if key not in pats:
            pats.append(key)
        pat_id.append(pats.index(key))
    wq = np.arange(GRID_W)
    wk = np.arange(GRID_W)
    col_start = np.clip(wq - NA_WIN_COLS // 2, 0, GRID_W - NA_WIN_COLS)
    col_off = wk[None, :] - col_start[:, None]
    col_valid = (col_off >= 0) & (col_off < NA_WIN_COLS)
    dc = np.clip(wk[None, :] - wq[:, None], -(NA_WIN_COLS - 1), NA_WIN_COLS - 1) + (NA_WIN_COLS - 1)
    n_dc = 2 * NA_WIN_COLS - 1
    onehot = (dc.reshape(-1)[None, :] == np.arange(n_dc)[:, None]).astype(np.float32)
    heads, n_dr = rpb.shape[0], rpb.shape[1]
    col = jnp.dot(rpb.astype(F32).reshape(heads * n_dr, n_dc), jnp.asarray(onehot),
                  precision=lax.Precision.HIGHEST).reshape(heads, n_dr, GRID_W, GRID_W)
    col = jnp.where(col_valid[None, None], col * LOG2E, NEG_BIG)
    masked_blk = jnp.full((heads, GRID_W, GRID_W), NEG_BIG, F32)
    tables = []
    for r0rel, starts in pats:
        qrows = []
        for qi in range(2):
            blks = []
            for kr in range(NA_KEY_ROWS):
                row_ok = starts[qi] <= kr < starts[qi] + NA_WIN_ROWS
                dr = kr - (r0rel + qi) + (NA_WIN_ROWS - 1)
                blks.append(col[:, dr] if row_ok else masked_blk)
            qrows.append(jnp.concatenate(blks, axis=-1))
        tables.append(jnp.concatenate(qrows, axis=1))
    return jnp.stack(tables, axis=0), np.asarray(pat_id, np.int32)


def _na_attention(hcat, rpb, batch, seq):
    t = hcat.shape[0]
    rows_n = seq // GRID_W
    nblk = rows_n // 2
    nq = 2 * GRID_W
    bias, pat_id = _na_bias(rpb, rows_n)
    nk = NA_KEY_ROWS * GRID_W
    hp = NA_HEADS // 2
    nstep = nblk // NA_SUB

    def bias_spec(sub):
        return pl.BlockSpec((1, 2, nq, nk), lambda b, h, r, pat: (pat[NA_SUB * r + sub], h, 0, 0))

    grid_spec = pltpu.PrefetchScalarGridSpec(
        num_scalar_prefetch=1,
        grid=(batch, hp, nstep),
        in_specs=[
            pl.BlockSpec((NA_SUB * nq, LANES), lambda b, h, r, pat: (b * nstep + r, h)),
            pl.BlockSpec((seq, LANES), lambda b, h, r, pat: (b, hp + h)),
            pl.BlockSpec((seq, LANES), lambda b, h, r, pat: (b, 2 * hp + h)),
        ] + [bias_spec(sub) for sub in range(NA_SUB)],
        out_specs=pl.BlockSpec((NA_SUB * nq, LANES), lambda b, h, r, pat: (b * nstep + r, h)),
    )
    return pl.pallas_call(
        functools.partial(_na_body, rows_n=rows_n),
        grid_spec=grid_spec,
        out_shape=jax.ShapeDtypeStruct((t, NA_HEADS * HEAD_DIM), BF16),
        compiler_params=_cparams(("parallel", "parallel", "arbitrary")),
        name="na_attn",
    )(jnp.asarray(pat_id), hcat, hcat, hcat, *([bias] * NA_SUB))


def _diff_body(lam_ref, q_ref, k_ref, v_ref, g_ref, o_ref, vt_ref, acc_ref, sta_ref, stb_ref, pa_ref, pb_ref,
               *, tk, lambda_init):
    j = pl.program_id(2)
    tq = q_ref.shape[0]
    seq = k_ref.shape[0]
    nkv = seq // tk

    @pl.when(j == 0)
    def _():
        ones = jnp.ones((DIFF_ONES_ROWS, tk), BF16)
        for c in range(nkv):
            vt_ref[c, :LANES, :] = v_ref[c * tk:(c + 1) * tk, :].astype(F32).T.astype(BF16)
            vt_ref[c, LANES:, :] = ones

    q = q_ref[...].astype(F32)
    lane = lax.broadcasted_iota(I32, (tq, LANES), 1)
    q1 = jnp.where(lane < HEAD_DIM, q, 0.0).T
    q2 = jnp.where(lane >= HEAD_DIM, q, 0.0).T
    rhs = jnp.concatenate([q1, q2], axis=1).astype(BF16)

    def scores(i, dst_ref):
        start = pl.multiple_of(i * tk, tk)
        dst_ref[...] = jnp.dot(k_ref[pl.ds(start, tk), :], rhs, preferred_element_type=F32)

    def softmax(src_ref, dst_ref, m):
        st = src_ref[...]
        m_new = jnp.maximum(m, jnp.max(st, axis=0, keepdims=True))
        dst_ref[...] = jnp.exp2((st - m_new).astype(BF16))
        return m_new, jnp.exp2(m - m_new)

    def weighted_values(i, p_ref, alpha):
        pv = jnp.dot(vt_ref[i], p_ref[...], preferred_element_type=F32)
        acc_ref[...] = alpha * acc_ref[...] + pv

    acc_ref[...] = jnp.zeros_like(acc_ref)
    scores(0, sta_ref)
    scores(1, stb_ref)
    m, a0 = softmax(sta_ref, pa_ref, jnp.full((1, 2 * tq), NEG_BIG, F32))

    def pair(ii, carry):
        m, a0 = carry
        i = 2 * ii
        scores(i + 2, sta_ref)
        m, a1 = softmax(stb_ref, pb_ref, m)
        weighted_values(i, pa_ref, a0)
        scores(i + 3, stb_ref)
        m, a0 = softmax(sta_ref, pa_ref, m)
        weighted_values(i + 1, pb_ref, a1)
        return m, a0

    m, a0 = lax.fori_loop(0, (nkv - 2) // 2, pair, (m, a0))
    m, a1 = softmax(stb_ref, pb_ref, m)
    weighted_values(nkv - 2, pa_ref, a0)
    weighted_values(nkv - 1, pb_ref, a1)
    acc = acc_ref[...]
    ot = acc[:LANES, :] / acc[LANES:LANES + 1, :]
    dt = ot[:, :tq] - lam_ref[0] * ot[:, tq:]
    ms = jnp.mean(dt * dt, axis=0, keepdims=True)
    y = dt * lax.rsqrt(ms + LN_EPS) * g_ref[...] * (1.0 - lambda_init)
    o_ref[...] = y.T.astype(o_ref.dtype)


def _diff_attention(hcat, lam, subln_g, lambda_init, batch, seq, tq=2048, tk=512):
    t = hcat.shape[0]
    tq = min(tq, seq)
    tk = min(tk, seq)
    nq = seq // tq
    qoff = 3 * NA_HEADS * HEAD_DIM // LANES
    koff = qoff + DIFF_HEADS
    voff = koff + DIFF_HEADS
    grid_spec = pltpu.PrefetchScalarGridSpec(
        num_scalar_prefetch=1,
        grid=(batch, DIFF_HEADS, nq),
        in_specs=[
            pl.BlockSpec((tq, LANES), lambda b, h, j, lam: (b * nq + j, qoff + h)),
            pl.BlockSpec((seq, LANES), lambda b, h, j, lam: (b, koff + h)),
            pl.BlockSpec((seq, LANES), lambda b, h, j, lam: (b, voff + h)),
            pl.BlockSpec((LANES, 1), lambda b, h, j, lam: (0, 0)),
        ],
        out_specs=pl.BlockSpec((tq, LANES), lambda b, h, j, lam: (b * nq + j, h)),
        scratch_shapes=[pltpu.VMEM((seq // tk, LANES + DIFF_ONES_ROWS, tk), BF16),
                        pltpu.VMEM((LANES + DIFF_ONES_ROWS, 2 * tq), F32),
                        pltpu.VMEM((tk, 2 * tq), F32),
                        pltpu.VMEM((tk, 2 * tq), F32),
                        pltpu.VMEM((tk, 2 * tq), BF16),
                        pltpu.VMEM((tk, 2 * tq), BF16)],
    )
    return pl.pallas_call(
        functools.partial(_diff_body, tk=tk, lambda_init=lambda_init),
        grid_spec=grid_spec,
        out_shape=jax.ShapeDtypeStruct((t, DIFF_HEADS * 2 * HEAD_DIM), BF16),
        compiler_params=_cparams(("parallel", "parallel", "arbitrary")),
        name="diff_attn",
    )(lam, hcat, hcat, hcat, subln_g.astype(F32).reshape(LANES, 1))


def _swa_body(sink_ref, q_ref, kv_ref, o_ref, kvt_ref):
    kvh = pl.program_id(1)
    nq = SWA_BLOCK
    nsub = q_ref.shape[0] // nq
    seq = kv_ref.shape[0]
    nblk = seq // SWA_BLOCK
    wb = min(3, nblk)
    nk = wb * SWA_BLOCK

    @pl.when(pl.program_id(2) == 0)
    def _():
        for c in range(nblk):
            kvt_ref[c] = kv_ref[c * SWA_BLOCK:(c + 1) * SWA_BLOCK, :].astype(F32).T.astype(BF16)

    group = SWA_Q_HEADS // SWA_KV_HEADS
    half = LANES // 2

    def swap_halves(a):
        return jnp.concatenate([a[half:], a[:half]], axis=0)

    top = lax.broadcasted_iota(I32, (LANES, nq), 0) < half
    sink = jnp.concatenate([jnp.full((1, nq), sink_ref[kvh * group + g], F32) for g in range(group)], axis=1)

    scores, kvts = [], []
    for sub in range(nsub):
        n = nsub * pl.program_id(2) + sub
        b0 = jnp.clip(n - 1, 0, nblk - wb)
        start = pl.multiple_of(b0 * SWA_BLOCK, SWA_BLOCK)
        cols = []
        for c in range(group // 2):
            qt = q_ref[sub * nq:(sub + 1) * nq, c * LANES:(c + 1) * LANES].astype(F32).T
            cols.append(jnp.where(top, qt, 0.0))
            cols.append(jnp.where(top, swap_halves(qt), 0.0))
        rhs = jnp.concatenate(cols, axis=1).astype(BF16)
        st = jnp.dot(kv_ref[pl.ds(start, nk), :], rhs, preferred_element_type=F32)
        kpos = start + lax.broadcasted_iota(I32, (nk, nq), 0)
        qpos = n * SWA_BLOCK + lax.broadcasted_iota(I32, (nk, nq), 1)
        mask = jnp.where(jnp.abs(kpos - qpos) <= SWA_WINDOW, 0.0, NEG_BIG)
        scores.append(st + jnp.concatenate([mask] * group, axis=1))
        kvts.append(jnp.concatenate([kvt_ref[b0 + w] for w in range(wb)], axis=1))
    for sub in range(nsub):
        st = scores[sub]
        m = jnp.maximum(jnp.max(st, axis=0, keepdims=True), sink)
        e = jnp.exp2(st - m)
        den = jnp.sum(e, axis=0, keepdims=True) + jnp.exp2(sink - m)
        ot = jnp.dot(kvts[sub], e.astype(BF16), preferred_element_type=F32) / den
        for c in range(group // 2):
            even = ot[:, 2 * c * nq:(2 * c + 1) * nq]
            odd = ot[:, (2 * c + 1) * nq:(2 * c + 2) * nq]
            blk = jnp.where(top, swap_halves(even), odd)
            o_ref[sub * nq:(sub + 1) * nq, c * LANES:(c + 1) * LANES] = blk.T.astype(o_ref.dtype)


def _swa_attention(hcat, sinks, batch, seq):
    t = hcat.shape[0]
    nb = seq // SWA_BLOCK
    group = SWA_Q_HEADS // SWA_KV_HEADS
    qw = group * HEAD_DIM
    kvoff = SWA_Q_HEADS * HEAD_DIM // LANES
    nsub = min(SWA_SUB, nb)
    nstep = nb // nsub
    grid_spec = pltpu.PrefetchScalarGridSpec(
        num_scalar_prefetch=1,
        grid=(batch, SWA_KV_HEADS, nstep),
        in_specs=[
            pl.BlockSpec((nsub * SWA_BLOCK, qw), lambda b, h, n, s: (b * nstep + n, h)),
            pl.BlockSpec((seq, LANES), lambda b, h, n, s: (b, kvoff + h)),
        ],
        out_specs=pl.BlockSpec((nsub * SWA_BLOCK, qw), lambda b, h, n, s: (b * nstep + n, h)),
        scratch_shapes=[pltpu.VMEM((nb, LANES, SWA_BLOCK), BF16)],
    )
    return pl.pallas_call(
        _swa_body,
        grid_spec=grid_spec,
        out_shape=jax.ShapeDtypeStruct((t, SWA_Q_HEADS * HEAD_DIM), BF16),
        compiler_params=_cparams(("parallel", "parallel", "arbitrary")),
        name="swa_attn",
    )(sinks.astype(F32) * LOG2E, hcat, hcat)


def _outproj_body(*refs, n_in):
    a_refs = refs[:n_in]
    w_ref, x_ref, g_ref, b_ref, whi_ref, wlo_ref, rbias_ref = refs[n_in:n_in + 7]
    o_ref, opk_ref, idx_ref, gate_ref, pos_ref, cnt_ref, base_ref = refs[n_in + 7:]
    acc = None
    off = 0
    for a_ref in a_refs:
        ka = a_ref.shape[1]
        d = jnp.dot(a_ref[...], w_ref[off:off + ka, :], preferred_element_type=F32)
        acc = d if acc is None else acc + d
        off += ka
    y = DN_ALPHA * x_ref[...] + acc
    out = _layer_norm(y, g_ref[...], b_ref[...])
    o_ref[...] = out
    packed = _pack_rows(out)
    for c in range(PACK_SPLIT):
        opk_ref[c] = packed[:, c * LANES:(c + 1) * LANES]
    _route(out, whi_ref, wlo_ref, rbias_ref, idx_ref, gate_ref, pos_ref, cnt_ref, base_ref)


def _outproj_ln_route(acts, w_bf, x2, g, b, router_w, router_bias, tm=512):
    t, d = x2.shape
    tm = min(tm, t)
    wt = router_w.astype(F32).T
    whi = wt.astype(BF16)
    wlo = (wt - whi.astype(F32)).astype(BF16)
    in_specs = [pl.BlockSpec((tm, a.shape[1]), lambda i: (i, 0)) for a in acts]
    in_specs += [pl.BlockSpec(w_bf.shape, lambda i: (0, 0)),
                 pl.BlockSpec((tm, d), lambda i: (i, 0)),
                 pl.BlockSpec((1, d), lambda i: (0, 0)),
                 pl.BlockSpec((1, d), lambda i: (0, 0)),
                 pl.BlockSpec((N_EXPERTS, d), lambda i: (0, 0)),
                 pl.BlockSpec((N_EXPERTS, d), lambda i: (0, 0)),
                 pl.BlockSpec((N_EXPERTS, 1), lambda i: (0, 0))]
    return pl.pallas_call(
        functools.partial(_outproj_body, n_in=len(acts)),
        grid=(t // tm,),
        in_specs=in_specs,
        out_specs=[pl.BlockSpec((tm, d), lambda i: (i, 0)),
                   pl.BlockSpec((PACK_SPLIT, tm, LANES), lambda i: (0, i, 0)),
                   pl.BlockSpec((TOP_K, tm), lambda i: (0, i)),
                   pl.BlockSpec((TOP_K, tm), lambda i: (0, i)),
                   pl.BlockSpec((TOP_K, tm), lambda i: (0, i)),
                   pl.BlockSpec((N_EXPERTS, 1), lambda i: (0, 0))],
        out_shape=[jax.ShapeDtypeStruct((t, d), F32),
                   jax.ShapeDtypeStruct((PACK_SPLIT, t, LANES), I32),
                   jax.ShapeDtypeStruct((TOP_K, t), I32),
                   jax.ShapeDtypeStruct((TOP_K, t), F32),
                   jax.ShapeDtypeStruct((TOP_K, t), I32),
                   jax.ShapeDtypeStruct((N_EXPERTS, 1), F32)],
        scratch_shapes=[pltpu.VMEM((N_EXPERTS, 1), F32)],
        compiler_params=_cparams(("arbitrary",)),
        name="outproj_ln_route",
    )(*acts, w_bf, x2, g.astype(F32).reshape(1, d), b.astype(F32).reshape(1, d),
      whi, wlo, router_bias.astype(F32).reshape(N_EXPERTS, 1))


def _route(h, whi_ref, wlo_ref, bias_ref, idx_ref, gate_ref, pos_ref, cnt_ref, base_ref):
    i = pl.program_id(0)
    tm = h.shape[0]
    gsz = N_EXPERTS // N_GROUPS

    @pl.when(i == 0)
    def _():
        base_ref[...] = jnp.zeros_like(base_ref)

    h_hi = h.astype(BF16)
    h_lo = (h - h_hi.astype(F32)).astype(BF16)
    dn = (((1,), (1,)), ((), ()))
    whi = whi_ref[...]
    logits = (lax.dot_general(whi, h_hi, dn, preferred_element_type=F32)
              + lax.dot_general(whi, h_lo, dn, preferred_element_type=F32)
              + lax.dot_general(wlo_ref[...], h_hi, dn, preferred_element_type=F32))
    scores = 1.0 / (1.0 + jnp.exp(-logits))
    choice = scores + bias_ref[...]

    iota_g = lax.broadcasted_iota(I32, (gsz, tm), 0)
    gscore = []
    for g in range(N_GROUPS):
        cg = choice[g * gsz:(g + 1) * gsz, :]
        m1 = jnp.max(cg, axis=0, keepdims=True)
        first = jnp.min(jnp.where(cg == m1, iota_g, gsz), axis=0, keepdims=True)
        m2 = jnp.max(jnp.where(iota_g == first, -jnp.inf, cg), axis=0, keepdims=True)
        gscore.append(m1 + m2)
    pieces = []
    for g in range(N_GROUPS):
        rank = jnp.zeros((1, tm), I32)
        for o in range(N_GROUPS):
            if o == g:
                continue
            beats = (gscore[o] > gscore[g]) if o > g else (gscore[o] >= gscore[g])
            rank = rank + beats.astype(I32)
        keep = rank < TOPK_GROUPS
        pieces.append(jnp.where(keep, choice[g * gsz:(g + 1) * gsz, :], -jnp.inf))
    masked = jnp.concatenate(pieces, axis=0)

    iota_e = lax.broadcasted_iota(I32, (N_EXPERTS, tm), 0)
    sel_all = jnp.zeros((N_EXPERTS, tm), F32)
    idxs, gates, sels = [], [], []
    for _ in range(TOP_K):
        mx = jnp.max(masked, axis=0, keepdims=True)
        idx = jnp.min(jnp.where(masked == mx, iota_e, N_EXPERTS), axis=0, keepdims=True)
        sel = iota_e == idx
        gates.append(jnp.sum(jnp.where(sel, scores, 0.0), axis=0, keepdims=True))
        masked = jnp.where(sel, -jnp.inf, masked)
        sel_all = sel_all + sel.astype(F32)
        idxs.append(idx)
        sels.append(sel)
    gsum = gates[0]
    for gk in gates[1:]:
        gsum = gsum + gk
    gate_ref[...] = jnp.concatenate(gates, axis=0) / gsum * ROUTED_SCALE
    idx_ref[...] = jnp.concatenate(idxs, axis=0)

    tri = (lax.broadcasted_iota(I32, (tm, tm), 0) < lax.broadcasted_iota(I32, (tm, tm), 1))
    cum = jnp.dot(sel_all.astype(BF16), tri.astype(F32).astype(BF16), preferred_element_type=F32)
    tot = cum + base_ref[...]
    pos = [jnp.sum(jnp.where(sel, tot, 0.0), axis=0, keepdims=True) for sel in sels]
    pos_ref[...] = jnp.concatenate(pos, axis=0).astype(I32)
    base_ref[...] = base_ref[...] + jnp.sum(sel_all, axis=1, keepdims=True)
    cnt_ref[...] = base_ref[...]


def _expert_body(be_ref, na_ref, bv_ref, ord_ref, nxt_ref, x_ref, wg_hbm, wu_hbm, wd_hbm, y_ref,
                 wgu, wdb, wg_buf, wu_buf, wd_buf, sem):
    i = pl.program_id(0)
    bm = x_ref.shape[1]
    ff = wd_hbm.shape[1]
    hm = bm // EXPERT_CHAINS

    def weight_copies(e, slot):
        return [pltpu.make_async_copy(hbm.at[e], buf.at[slot], sem.at[slot])
                for hbm, buf in ((wg_hbm, wg_buf), (wu_hbm, wu_buf), (wd_hbm, wd_buf))]

    @pl.when(i < na_ref[0])
    def _():
        prev = be_ref[jnp.maximum(i - 1, 0)]
        slot = ord_ref[i] % 2

        @pl.when(i == 0)
        def _():
            for cp in weight_copies(be_ref[0], 0):
                cp.start()

        @pl.when((i == 0) | (be_ref[i] != prev))
        def _():
            for cp in weight_copies(be_ref[i], slot):
                cp.wait()
            wgu[:, :ff] = wg_buf[slot].astype(BF16)
            wgu[:, ff:] = wu_buf[slot].astype(BF16)
            wdb[...] = wd_buf[slot].astype(BF16)

            @pl.when(nxt_ref[i] >= 0)
            def _():
                for cp in weight_copies(nxt_ref[i], 1 - slot):
                    cp.start()

        valid = bv_ref[i]
        row = lax.broadcasted_iota(I32, (hm, LANES), 0)
        def run(n_chains):
            gus = []
            for hb in range(n_chains):
                rows = slice(hb * hm, (hb + 1) * hm)
                keep = row < valid - hb * hm
                x = _unpack_rows(lambda c: jnp.where(keep, x_ref[c, rows, :], 0)).astype(BF16)
                gus.append(jnp.dot(x, wgu[...], preferred_element_type=F32))
            ys = []
            for gu in gus:
                hmid = (_silu(gu[:, :ff]) * gu[:, ff:]).astype(BF16)
                ys.append(jnp.dot(hmid, wdb[...], preferred_element_type=F32))
            for hb, y in enumerate(ys):
                packed = _pack_rows(y)
                for c in range(PACK_SPLIT):
                    y_ref[c, hb * hm:(hb + 1) * hm, :] = packed[:, c * LANES:(c + 1) * LANES]
            if n_chains < EXPERT_CHAINS:
                y_ref[:, n_chains * hm:, :] = jnp.zeros((PACK_SPLIT, bm - n_chains * hm, LANES), I32)

        for n_chains in range(1, EXPERT_CHAINS + 1):
            lo = (n_chains - 1) * hm
            cond = valid > lo if n_chains == EXPERT_CHAINS else (valid > lo) & (valid <= lo + hm)
            pl.when(cond)(functools.partial(run, n_chains))


def _expert_matmul(xs, blk_expert, n_active, blk_valid, w_gate, w_up, w_down):
    n_slots = xs.shape[1]
    bm = EXPERT_BLOCK
    n_blocks = n_slots // bm
    d, ff = w_gate.shape[1], w_gate.shape[2]

    blk = jnp.arange(n_blocks, dtype=I32)
    change = jnp.concatenate([jnp.zeros((1,), I32), (blk_expert[1:] != blk_expert[:-1]).astype(I32)])
    ordinal = jnp.sum(jnp.where(blk[None, :] <= blk[:, None], change[None, :], 0), axis=1).astype(I32)
    later = (ordinal[None, :] == ordinal[:, None] + 1) & (blk[None, :] < n_active[0])
    nxt = jnp.max(jnp.where(later, blk_expert[None, :], -1), axis=1).astype(I32)

    def row_map(i, be, na, bv, od, nx):
        return (0, jnp.minimum(i, na[0] - 1), 0)

    grid_spec = pltpu.PrefetchScalarGridSpec(
        num_scalar_prefetch=5,
        grid=(n_blocks,),
        in_specs=[pl.BlockSpec((PACK_SPLIT, bm, LANES), row_map),
                  pl.BlockSpec(memory_space=pl.ANY),
                  pl.BlockSpec(memory_space=pl.ANY),
                  pl.BlockSpec(memory_space=pl.ANY)],
        out_specs=pl.BlockSpec((PACK_SPLIT, bm, LANES), row_map),
        scratch_shapes=[pltpu.VMEM((d, 2 * ff), BF16), pltpu.VMEM((ff, d), BF16),
                        pltpu.VMEM((2, d, ff), F32), pltpu.VMEM((2, d, ff), F32), pltpu.VMEM((2, ff, d), F32),
                        pltpu.SemaphoreType.DMA((2,))],
    )
    return pl.pallas_call(
        _expert_body,
        grid_spec=grid_spec,
        out_shape=jax.ShapeDtypeStruct((PACK_SPLIT, n_slots, LANES), I32),
        compiler_params=_cparams(("arbitrary",)),
        name="expert_mlp",
    )(blk_expert, n_active, blk_valid, ordinal, nxt, xs, w_gate, w_up, w_down)


def _slot_body(idx_ref, pos_ref, ps_ref, slot_ref, *, n_slots):
    tm = idx_ref.shape[1]
    iota_e = lax.broadcasted_iota(I32, (N_EXPERTS, tm), 0)
    ps = ps_ref[...]
    rows = []
    for k in range(TOP_K):
        start = jnp.sum(jnp.where(iota_e == idx_ref[k:k + 1, :], ps, 0.0), axis=0, keepdims=True)
        rows.append(start.astype(I32) + pos_ref[k:k + 1, :])
    slot = jnp.concatenate(rows, axis=0)
    for c in range(PACK_SPLIT):
        slot_ref[c * TOP_K:(c + 1) * TOP_K, :] = slot + c * n_slots


def _slots(idx, pos, pad_start, n_slots, tm=512):
    t = idx.shape[1]
    tm = min(tm, t)
    return pl.pallas_call(
        functools.partial(_slot_body, n_slots=n_slots),
        grid=(t // tm,),
        in_specs=[pl.BlockSpec((TOP_K, tm), lambda i: (0, i)),
                  pl.BlockSpec((TOP_K, tm), lambda i: (0, i)),
                  pl.BlockSpec((N_EXPERTS, 1), lambda i: (0, 0))],
        out_specs=pl.BlockSpec((PACK_SPLIT * TOP_K, tm), lambda i: (0, i)),
        out_shape=jax.ShapeDtypeStruct((PACK_SPLIT * TOP_K, t), I32),
        compiler_params=_cparams(("parallel",)),
        name="slots",
    )(idx, pos, pad_start.astype(F32).reshape(N_EXPERTS, 1))


def _sc_mesh():
    return plsc.VectorSubcoreMesh(core_axis_name="core", subcore_axis_name="subcore",
                                  num_cores=SC_CORES, num_subcores=SC_SUBCORES)


def _sc_scatter_rows(rows, idx, n_out):
    t = idx.shape[1]
    nj = t // SC_WINDOW

    @functools.partial(pl.kernel, out_type=jax.ShapeDtypeStruct((n_out, LANES), I32),
                       mesh=_sc_mesh(), scratch_types=[], name="sc_dispatch")
    def run(rows_hbm, idx_hbm, out_hbm):
        def body(rows_vmem, idx_vmem):
            for k in range(TOP_K):
                pltpu.sync_copy(rows_vmem, out_hbm.at[idx_vmem.at[k]])

        pltpu.emit_pipeline(
            body,
            grid=(rows.shape[0] // SC_WINDOW,),
            in_specs=[pl.BlockSpec((SC_WINDOW, LANES), lambda s: (s, 0)),
                      pl.BlockSpec((TOP_K, SC_WINDOW), lambda s: (s // nj, s % nj))],
            out_specs=[],
            core_axis_name=("core", "subcore"),
            dimension_semantics=(pltpu.PARALLEL,),
        )(rows_hbm, idx_hbm)

    return run(rows, idx)


def _sc_gather_rows(src, idx):
    nr, t = idx.shape
    nj = t // SC_WINDOW

    @functools.partial(pl.kernel, out_type=jax.ShapeDtypeStruct((nr * t, LANES), I32),
                       mesh=_sc_mesh(), scratch_types=[], name="sc_combine")
    def run(src_hbm, idx_hbm, out_hbm):
        def body(idx_vmem, out_vmem):
            pltpu.sync_copy(src_hbm.at[idx_vmem.at[0]], out_vmem)

        pltpu.emit_pipeline(
            body,
            grid=(nr * nj,),
            in_specs=[pl.BlockSpec((1, SC_WINDOW), lambda s: (s // nj, s % nj))],
            out_specs=[pl.BlockSpec((SC_WINDOW, LANES), lambda s: (s, 0))],
            core_axis_name=("core", "subcore"),
            dimension_semantics=(pltpu.PARALLEL,),
        )(idx_hbm, out_hbm)

    return run(src, idx)


def _combine_body(h_ref, yg_ref, gt_ref, sg_ref, su_ref, sd_ref, g_ref, b_ref, *rest, proj):
    h = h_ref[...]
    hb = h.astype(BF16)
    a = jnp.dot(hb, sg_ref[...], preferred_element_type=F32)
    u = jnp.dot(hb, su_ref[...], preferred_element_type=F32)
    moe = jnp.dot((_silu(a) * u).astype(BF16), sd_ref[...], preferred_element_type=F32)
    gt = gt_ref[...]
    for k in range(TOP_K):
        moe = moe + gt[:, k:k + 1] * _unpack_rows(lambda c, k=k: yg_ref[c, k])
    out = _layer_norm(DN_ALPHA * h + moe, g_ref[...], b_ref[...])
    if proj is None:
        rest[0][...] = out
    else:
        w_ref, tab_ref, o_ref, hcat_ref = rest
        o_ref[...] = out
        _project_rope(out.astype(BF16), w_ref, tab_ref, hcat_ref, *proj)


def _combine_ln(h, yg, gates_t, sh_gate, sh_up, sh_down, g, b, next_proj=None, tm=256):
    t, d = h.shape
    tm = min(tm, t) if next_proj is None else min(tm, next_proj[3])
    ff = sh_gate.shape[1]
    in_specs = [pl.BlockSpec((tm, d), lambda i: (i, 0)),
                pl.BlockSpec((PACK_SPLIT, TOP_K, tm, LANES), lambda i: (0, 0, i, 0)),
                pl.BlockSpec((tm, TOP_K), lambda i: (i, 0)),
                pl.BlockSpec((d, ff), lambda i: (0, 0)),
                pl.BlockSpec((d, ff), lambda i: (0, 0)),
                pl.BlockSpec((ff, d), lambda i: (0, 0)),
                pl.BlockSpec((1, d), lambda i: (0, 0)),
                pl.BlockSpec((1, d), lambda i: (0, 0))]
    out_specs = [pl.BlockSpec((tm, d), lambda i: (i, 0))]
    out_shape = [jax.ShapeDtypeStruct((t, d), F32)]
    operands = [h, yg, gates_t, sh_gate.astype(BF16), sh_up.astype(BF16), sh_down.astype(BF16),
                g.astype(F32).reshape(1, d), b.astype(F32).reshape(1, d)]
    proj = None
    if next_proj is not None:
        w_bf, tab, rope_kind, seq = next_proj
        sb = seq // tm
        n_out = w_bf.shape[1]
        in_specs += [pl.BlockSpec((d, n_out), lambda i: (0, 0)),
                     pl.BlockSpec((tm, tab.shape[1]), lambda i: (i % sb, 0))]
        out_specs.append(pl.BlockSpec((tm, n_out), lambda i: (i, 0)))
        out_shape.append(jax.ShapeDtypeStruct((t, n_out), BF16))
        operands += [w_bf, tab]
        proj = (PROJ_CHUNK, tuple(rope_kind))
    res = pl.pallas_call(
        functools.partial(_combine_body, proj=proj),
        grid=(t // tm,),
        in_specs=in_specs,
        out_specs=out_specs,
        out_shape=out_shape,
        compiler_params=_cparams(("parallel",)),
        name="combine_ln",
    )(*operands)
    return res if next_proj is not None else res[0]


def _mixer_out_and_moe(acts, w_out, x2, ln1_g, ln1_b, router_w, router_bias, w_gate, w_up, w_down,
                       sh_gate, sh_up, sh_down, g, b, next_proj=None):
    h, h_pk, idx, gates, pos, cnt = _outproj_ln_route(acts, w_out.astype(BF16), x2, ln1_g, ln1_b,
                                                      router_w, router_bias)
    t, d = h.shape
    bm = EXPERT_BLOCK
    counts = cnt[:, 0].astype(I32)
    padded = (counts + bm - 1) // bm * bm
    e_iota = jnp.arange(N_EXPERTS, dtype=I32)
    pad_end = jnp.sum(jnp.where(e_iota[None, :] <= e_iota[:, None], padded[None, :], 0), axis=1)
    pad_start = pad_end - padded
    n_blocks = t * TOP_K // bm + N_EXPERTS
    n_slots = n_blocks * bm
    blk_start = jnp.arange(n_blocks, dtype=I32) * bm
    blk_expert = jnp.minimum(
        jnp.sum((blk_start[:, None] >= pad_end[None, :]).astype(I32), axis=1), N_EXPERTS - 1)
    real_end = jnp.sum(jnp.where(e_iota[None, :] == blk_expert[:, None], (pad_start + counts)[None, :], 0), axis=1)
    blk_valid = jnp.clip(real_end - blk_start, 0, bm).astype(I32)
    n_active = (pad_end[-1:] // bm).astype(I32)

    slot4 = _slots(idx, pos, pad_start, n_slots)
    xs = _sc_scatter_rows(h_pk.reshape(PACK_SPLIT * t, LANES), slot4, PACK_SPLIT * n_slots)
    ys = _expert_matmul(xs.reshape(PACK_SPLIT, n_slots, LANES), blk_expert, n_active, blk_valid,
                        w_gate, w_up, w_down)
    yg = _sc_gather_rows(ys.reshape(PACK_SPLIT * n_slots, LANES), slot4)
    return _combine_ln(h, yg.reshape(PACK_SPLIT, TOP_K, t, LANES), gates.T, sh_gate, sh_up, sh_down, g, b,
                       next_proj=next_proj)


def kernel(x, l0_w_in, l0_w_out, l0_na_rpb, l0_diff_lq1, l0_diff_lk1, l0_diff_lq2, l0_diff_lk2, l0_diff_subln_g, l0_ln1_g, l0_ln1_b, l0_router_w, l0_router_bias, l0_expert_w_gate, l0_expert_w_up, l0_expert_w_down, l0_shared_w_gate, l0_shared_w_up, l0_shared_w_down, l0_ln2_g, l0_ln2_b, l1_w_in, l1_w_out, l1_swa_sinks, l1_ln1_g, l1_ln1_b, l1_router_w, l1_router_bias, l1_expert_w_gate, l1_expert_w_up, l1_expert_w_down, l1_shared_w_gate, l1_shared_w_up, l1_shared_w_down, l1_ln2_g, l1_ln2_b):
    batch, seq, d = x.shape
    t = batch * seq
    x2 = x.reshape(t, d).astype(F32)
    tab = _rope_table(seq)
    qscale = HEAD_DIM ** -0.5 * LOG2E
    na_w = NA_HEADS * HEAD_DIM
    dq_w = DIFF_HEADS * 2 * HEAD_DIM

    col_scale = jnp.concatenate([
        jnp.full((na_w,), qscale, F32), jnp.ones((2 * na_w,), F32),
        jnp.full((dq_w,), qscale, F32), jnp.ones((2 * dq_w,), F32)])
    w_in0 = (l0_w_in.astype(F32) * col_scale).astype(BF16)
    per = LANES
    rope0 = [0] * (3 * na_w // per) + [1] * (2 * dq_w // per) + [0] * (dq_w // per)
    hcat0 = _inproj(x2, w_in0, tab, rope0, seq)
    oa = _na_attention(hcat0, l0_na_rpb, batch, seq)
    lambda_init = 0.8 - 0.6 * math.exp(-0.3 * 0)
    lam = (jnp.exp(jnp.sum(l0_diff_lq1.astype(F32) * l0_diff_lk1.astype(F32)))
           - jnp.exp(jnp.sum(l0_diff_lq2.astype(F32) * l0_diff_lk2.astype(F32))) + lambda_init)
    od = _diff_attention(hcat0, lam.reshape(1).astype(F32), l0_diff_subln_g, lambda_init, batch, seq)
    q_w = SWA_Q_HEADS * HEAD_DIM
    kv_w = SWA_KV_HEADS * HEAD_DIM
    w1 = l1_w_in.astype(F32)
    wq = w1[:, :q_w] * qscale
    wk = w1[:, q_w:q_w + kv_w].reshape(d, SWA_KV_HEADS, HEAD_DIM)
    wv = w1[:, q_w + kv_w:].reshape(d, SWA_KV_HEADS, HEAD_DIM)
    wkv = jnp.concatenate([wk, wv], axis=-1).reshape(d, 2 * kv_w)
    w_in1 = jnp.concatenate([wq, wkv], axis=1).astype(BF16)
    rope1 = [1] * (q_w // per) + [2] * (2 * kv_w // per)

    x2, hcat1 = _mixer_out_and_moe(
        [oa, od], l0_w_out, x2, l0_ln1_g, l0_ln1_b, l0_router_w, l0_router_bias,
        l0_expert_w_gate, l0_expert_w_up, l0_expert_w_down,
        l0_shared_w_gate, l0_shared_w_up, l0_shared_w_down, l0_ln2_g, l0_ln2_b,
        next_proj=(w_in1, tab, rope1, seq))

    o1 = _swa_attention(hcat1, l1_swa_sinks, batch, seq)
    x2 = _mixer_out_and_moe(
        [o1], l1_w_out, x2, l1_ln1_g, l1_ln1_b, l1_router_w, l1_router_bias,
        l1_expert_w_gate, l1_expert_w_up, l1_expert_w_down,
        l1_shared_w_gate, l1_shared_w_up, l1_shared_w_down, l1_ln2_g, l1_ln2_b)
    return x2.reshape(batch, seq, d).astype(x.dtype)
```

```python
import functools
import math

import numpy as np
import jax
import jax.numpy as jnp
from jax import lax
from jax.experimental import pallas as pl
from jax.experimental.pallas import tpu as pltpu
from jax.experimental.pallas import tpu_sc as plsc

F32 = jnp.float32
BF16 = jnp.bfloat16
I32 = jnp.int32

HEAD_DIM = 64
GRID_W = 64
NA_HEADS = 8
NA_WIN_ROWS = 8
NA_WIN_COLS = 16
DIFF_HEADS = 4
SWA_Q_HEADS = 16
SWA_KV_HEADS = 4
SWA_WINDOW = 128
SWA_BLOCK = 128
ROPE_THETA = 10000.0
N_EXPERTS = 256
TOP_K = 8
N_GROUPS = 8
TOPK_GROUPS = 4
ROUTED_SCALE = 2.5
LN_EPS = 1e-5
DEPTH = 2
DN_ALPHA = (2 * DEPTH) ** 0.25

LOG2E = 1.4426950408889634
NEG_BIG = -3.0e38
LANES = 128
NA_KEY_ROWS = NA_WIN_ROWS + 1
PROJ_CHUNK = 256
SWA_SUB = 8
NA_SUB = 8
EXPERT_BLOCK = 512
COMBINE_CHUNKS = 4
EXPERT_CHAINS = 2
DIFF_ONES_ROWS = 16
PACK_SPLIT = 4
SC_CORES = 2
SC_SUBCORES = 16
SC_WINDOW = 128
VMEM_LIMIT = 56 * 1024 * 1024


def _cparams(sem):
    return pltpu.CompilerParams(dimension_semantics=sem, vmem_limit_bytes=VMEM_LIMIT)


def _silu(x):
    return x / (1.0 + jnp.exp(-x))


def _pack_rows(y):
    half = y.shape[1] // 2
    bits = pltpu.bitcast(y.astype(BF16).astype(F32), I32)
    lo = lax.shift_right_logical(bits[:, :half], 16)
    hi = bits[:, half:] & jnp.int32(-65536)
    return hi | lo


def _unpack_words(w):
    lo = pltpu.bitcast(lax.shift_left(w, 16), F32)
    hi = pltpu.bitcast(w & jnp.int32(-65536), F32)
    return lo, hi


def _unpack_rows(ref_at):
    los, his = [], []
    for c in range(PACK_SPLIT):
        lo, hi = _unpack_words(ref_at(c))
        los.append(lo)
        his.append(hi)
    return jnp.concatenate(los + his, axis=1)


def _layer_norm(y, g, b):
    mu = jnp.mean(y, axis=-1, keepdims=True)
    yc = y - mu
    var = jnp.mean(yc * yc, axis=-1, keepdims=True)
    return yc * lax.rsqrt(var + LN_EPS) * g + b


def _inproj_body(x_ref, w_ref, tab_ref, o_ref, *, chunk, rope_kind):
    _project_rope(x_ref[...].astype(BF16), w_ref, tab_ref, o_ref, chunk, rope_kind)


def _project_rope(x, w_ref, tab_ref, o_ref, chunk, rope_kind):
    tm = x.shape[0]
    n_out = w_ref.shape[1]
    lane = lax.broadcasted_iota(I32, (tm, LANES), 1)
    first_half = (lane % HEAD_DIM) < (HEAD_DIM // 2)
    per = chunk // LANES
    for c in range(n_out // chunk):
        h = jnp.dot(x, w_ref[:, c * chunk:(c + 1) * chunk], preferred_element_type=F32)
        kinds = rope_kind[c * per:(c + 1) * per]
        if any(kinds):
            pieces = []
            for j, kind in enumerate(kinds):
                t = h[:, j * LANES:(j + 1) * LANES]
                if kind:
                    base = (kind - 1) * 2 * LANES
                    cos = tab_ref[:, base:base + LANES]
                    sin = tab_ref[:, base + LANES:base + 2 * LANES]
                    rot = jnp.where(first_half, pltpu.roll(t, LANES - HEAD_DIM // 2, 1),
                                    pltpu.roll(t, HEAD_DIM // 2, 1))
                    t = t * cos + rot * sin
                pieces.append(t)
            h = jnp.concatenate(pieces, axis=1)
        o_ref[:, c * chunk:(c + 1) * chunk] = h.astype(o_ref.dtype)


def _inproj(x2, w_bf, tab, rope_kind, seq, tm=512):
    t, d = x2.shape
    n_out = w_bf.shape[1]
    tm = min(tm, seq)
    sb = seq // tm
    return pl.pallas_call(
        functools.partial(_inproj_body, chunk=PROJ_CHUNK, rope_kind=tuple(rope_kind)),
        grid=(t // tm,),
        in_specs=[pl.BlockSpec((tm, d), lambda i: (i, 0)),
                  pl.BlockSpec((d, n_out), lambda i: (0, 0)),
                  pl.BlockSpec((tm, tab.shape[1]), lambda i: (i % sb, 0))],
        out_specs=pl.BlockSpec((tm, n_out), lambda i: (i, 0)),
        out_shape=jax.ShapeDtypeStruct((t, n_out), BF16),
        compiler_params=_cparams(("parallel",)),
        name="inproj",
    )(x2, w_bf, tab)


def _rope_table(seq):
    inv = 1.0 / (ROPE_THETA ** (jnp.arange(0, HEAD_DIM, 2, dtype=F32) / HEAD_DIM))
    ang = jnp.arange(seq, dtype=F32)[:, None] * inv[None, :]
    cos = jnp.cos(ang)
    sin = jnp.sin(ang)
    cos64 = jnp.concatenate([cos, cos], axis=-1)
    sin64 = jnp.concatenate([-sin, sin], axis=-1)
    one = jnp.ones((seq, HEAD_DIM), F32)
    zero = jnp.zeros((seq, HEAD_DIM), F32)
    return jnp.concatenate([cos64, cos64, sin64, sin64, cos64, one, sin64, zero], axis=-1)


def _na_body(pat_ref, q_ref, k_ref, v_ref, *rest, rows_n):
    del pat_ref
    bias_refs, o_ref = rest[:NA_SUB], rest[NA_SUB]
    nq = q_ref.shape[0] // NA_SUB
    nk = NA_KEY_ROWS * GRID_W
    low = lax.broadcasted_iota(I32, (nq, LANES), 1) < HEAD_DIM
    scores, vwins = [], []
    for sub in range(NA_SUB):
        r = NA_SUB * pl.program_id(2) + sub
        ks = jnp.clip(2 * r - NA_WIN_ROWS // 2, 0, rows_n - NA_KEY_ROWS)
        start = pl.multiple_of(ks * GRID_W, GRID_W)
        q = q_ref[sub * nq:(sub + 1) * nq, :].astype(F32)
        qm = jnp.concatenate([jnp.where(low, q, 0.0), jnp.where(low, 0.0, q)], axis=0).astype(BF16)
        s = lax.dot_general(qm, k_ref[pl.ds(start, nk), :], (((1,), (1,)), ((), ())),
                            preferred_element_type=F32)
        scores.append(s + bias_refs[sub][0].reshape(2 * nq, nk))
        vwins.append(v_ref[pl.ds(start, nk), :])
    for sub in range(NA_SUB):
        s = scores[sub]
        m = jnp.max(s, axis=-1, keepdims=True)
        p = jnp.exp2(s - m)
        l = jnp.sum(p, axis=-1, keepdims=True)
        pv = jnp.dot(p.astype(BF16), vwins[sub], preferred_element_type=F32) / l
        o_ref[sub * nq:(sub + 1) * nq, :] = jnp.where(low, pv[:nq], pv[nq:]).astype(o_ref.dtype)


def _na_bias(rpb, rows_n):
    half = NA_WIN_ROWS // 2
    pats, pat_id = [], []
    for blk in range(rows_n // 2):
        r0 = 2 * blk
        ks = min(max(r0 - half, 0), rows_n - NA_KEY_ROWS)
        starts = tuple(min(max(r0 + i - half, 0), rows_n - NA_WIN_ROWS) - ks for i in range(2))
        key = (r0 - ks, starts)
        if key not in pats:
            pats.append(key)
        pat_id.append(pats.index(key))
    wq = np.arange(GRID_W)
    wk = np.arange(GRID_W)
    col_start = np.clip(wq - NA_WIN_COLS // 2, 0, GRID_W - NA_WIN_COLS)
    col_off = wk[None, :] - col_start[:, None]
    col_valid = (col_off >= 0) & (col_off < NA_WIN_COLS)
    dc = np.clip(wk[None, :] - wq[:, None], -(NA_WIN_COLS - 1), NA_WIN_COLS - 1) + (NA_WIN_COLS - 1)
    n_dc = 2 * NA_WIN_COLS - 1
    onehot = (dc.reshape(-1)[None, :] == np.arange(n_dc)[:, None]).astype(np.float32)
    heads, n_dr = rpb.shape[0], rpb.shape[1]
    col = jnp.dot(rpb.astype(F32).reshape(heads * n_dr, n_dc), jnp.asarray(onehot),
                  precision=lax.Precision.HIGHEST).reshape(heads, n_dr, GRID_W, GRID_W)
    col = jnp.where(col_valid[None, None], col * LOG2E, NEG_BIG)
    masked_blk = jnp.full((heads, GRID_W, GRID_W), NEG_BIG, F32)
    tables = []
    for r0rel, starts in pats:
        qrows = []
        for qi in range(2):
            blks = []
            for kr in range(NA_KEY_ROWS):
                row_ok = starts[qi] <= kr < starts[qi] + NA_WIN_ROWS
                dr = kr - (r0rel + qi) + (NA_WIN_ROWS - 1)
                blks.append(col[:, dr] if row_ok else masked_blk)
            qrows.append(jnp.concatenate(blks, axis=-1))
        tables.append(jnp.concatenate(qrows, axis=1))
    return jnp.stack(tables, axis=0), np.asarray(pat_id, np.int32)


def _na_attention(hcat, rpb, batch, seq):
    t = hcat.shape[0]
    rows_n = seq // GRID_W
    nblk = rows_n // 2
    nq = 2 * GRID_W
    bias, pat_id = _na_bias(rpb, rows_n)
    nk = NA_KEY_ROWS * GRID_W
    hp = NA_HEADS // 2
    nstep = nblk // NA_SUB

    def bias_spec(sub):
        return pl.BlockSpec((1, 2, nq, nk), lambda b, h, r, pat: (pat[NA_SUB * r + sub], h, 0, 0))

    grid_spec = pltpu.PrefetchScalarGridSpec(
        num_scalar_prefetch=1,
        grid=(batch, hp, nstep),
        in_specs=[
            pl.BlockSpec((NA_SUB * nq, LANES), lambda b, h, r, pat: (b * nstep + r, h)),
            pl.BlockSpec((seq, LANES), lambda b, h, r, pat: (b, hp + h)),
            pl.BlockSpec((seq, LANES), lambda b, h, r, pat: (b, 2 * hp + h)),
        ] + [bias_spec(sub) for sub in range(NA_SUB)],
        out_specs=pl.BlockSpec((NA_SUB * nq, LANES), lambda b, h, r, pat: (b * nstep + r, h)),
    )
    return pl.pallas_call(
        functools.partial(_na_body, rows_n=rows_n),
        grid_spec=grid_spec,
        out_shape=jax.ShapeDtypeStruct((t, NA_HEADS * HEAD_DIM), BF16),
        compiler_params=_cparams(("parallel", "parallel", "arbitrary")),
        name="na_attn",
    )(jnp.asarray(pat_id), hcat, hcat, hcat, *([bias] * NA_SUB))


def _diff_body(lam_ref, q_ref, k_ref, v_ref, g_ref, o_ref, vt_ref, acc_ref, sta_ref, stb_ref, pa_ref, pb_ref,
               *, tk, lambda_init):
    j = pl.program_id(2)
    tq = q_ref.shape[0]
    seq = k_ref.shape[0]
    nkv = seq // tk

    @pl.when(j == 0)
    def _():
        ones = jnp.ones((DIFF_ONES_ROWS, tk), BF16)
        for c in range(nkv):
            vt_ref[c, :LANES, :] = v_ref[c * tk:(c + 1) * tk, :].astype(F32).T.astype(BF16)
            vt_ref[c, LANES:, :] = ones

    q = q_ref[...].astype(F32)
    lane = lax.broadcasted_iota(I32, (tq, LANES), 1)
    q1 = jnp.where(lane < HEAD_DIM, q, 0.0).T
    q2 = jnp.where(lane >= HEAD_DIM, q, 0.0).T
    rhs = jnp.concatenate([q1, q2], axis=1).astype(BF16)

    def scores(i, dst_ref):
        start = pl.multiple_of(i * tk, tk)
        dst_ref[...] = jnp.dot(k_ref[pl.ds(start, tk), :], rhs, preferred_element_type=F32)

    def softmax(src_ref, dst_ref, m):
        st = src_ref[...]
        m_new = jnp.maximum(m, jnp.max(st, axis=0, keepdims=True))
        dst_ref[...] = jnp.exp2((st - m_new).astype(BF16))
        return m_new, jnp.exp2(m - m_new)

    def weighted_values(i, p_ref, alpha):
        pv = jnp.dot(vt_ref[i], p_ref[...], preferred_element_type=F32)
        acc_ref[...] = alpha * acc_ref[...] + pv

    acc_ref[...] = jnp.zeros_like(acc_ref)
    scores(0, sta_ref)
    scores(1, stb_ref)
    m, a0 = softmax(sta_ref, pa_ref, jnp.full((1, 2 * tq), NEG_BIG, F32))

    def pair(ii, carry):
        m, a0 = carry
        i = 2 * ii
        scores(i + 2, sta_ref)
        m, a1 = softmax(stb_ref, pb_ref, m)
        weighted_values(i, pa_ref, a0)
        scores(i + 3, stb_ref)
        m, a0 = softmax(sta_ref, pa_ref, m)
        weighted_values(i + 1, pb_ref, a1)
        return m, a0

    m, a0 = lax.fori_loop(0, (nkv - 2) // 2, pair, (m, a0))
    m, a1 = softmax(stb_ref, pb_ref, m)
    weighted_values(nkv - 2, pa_ref, a0)
    weighted_values(nkv - 1, pb_ref, a1)
    acc = acc_ref[...]
    ot = acc[:LANES, :] / acc[LANES:LANES + 1, :]
    dt = ot[:, :tq] - lam_ref[0] * ot[:, tq:]
    ms = jnp.mean(dt * dt, axis=0, keepdims=True)
    y = dt * lax.rsqrt(ms + LN_EPS) * g_ref[...] * (1.0 - lambda_init)
    o_ref[...] = y.T.astype(o_ref.dtype)


def _diff_attention(hcat, lam, subln_g, lambda_init, batch, seq, tq=2048, tk=512):
    t = hcat.shape[0]
    tq = min(tq, seq)
    tk = min(tk, seq)
    nq = seq // tq
    qoff = 3 * NA_HEADS * HEAD_DIM // LANES
    koff = qoff + DIFF_HEADS
    voff = koff + DIFF_HEADS
    grid_spec = pltpu.PrefetchScalarGridSpec(
        num_scalar_prefetch=1,
        grid=(batch, DIFF_HEADS, nq),
        in_specs=[
            pl.BlockSpec((tq, LANES), lambda b, h, j, lam: (b * nq + j, qoff + h)),
            pl.BlockSpec((seq, LANES), lambda b, h, j, lam: (b, koff + h)),
            pl.BlockSpec((seq, LANES), lambda b, h, j, lam: (b, voff + h)),
            pl.BlockSpec((LANES, 1), lambda b, h, j, lam: (0, 0)),
        ],
        out_specs=pl.BlockSpec((tq, LANES), lambda b, h, j, lam: (b * nq + j, h)),
        scratch_shapes=[pltpu.VMEM((seq // tk, LANES + DIFF_ONES_ROWS, tk), BF16),
                        pltpu.VMEM((LANES + DIFF_ONES_ROWS, 2 * tq), F32),
                        pltpu.VMEM((tk, 2 * tq), F32),
                        pltpu.VMEM((tk, 2 * tq), F32),
                        pltpu.VMEM((tk, 2 * tq), BF16),
                        pltpu.VMEM((tk, 2 * tq), BF16)],
    )
    return pl.pallas_call(
        functools.partial(_diff_body, tk=tk, lambda_init=lambda_init),
        grid_spec=grid_spec,
        out_shape=jax.ShapeDtypeStruct((t, DIFF_HEADS * 2 * HEAD_DIM), BF16),
        compiler_params=_cparams(("parallel", "parallel", "arbitrary")),
        name="diff_attn",
    )(lam, hcat, hcat, hcat, subln_g.astype(F32).reshape(LANES, 1))


def _swa_body(sink_ref, q_ref, kv_ref, o_ref, kvt_ref):
    kvh = pl.program_id(1)
    nq = SWA_BLOCK
    nsub = q_ref.shape[0] // nq
    seq = kv_ref.shape[0]
    nblk = seq // SWA_BLOCK
    wb = min(3, nblk)
    nk = wb * SWA_BLOCK

    @pl.when(pl.program_id(2) == 0)
    def _():
        for c in range(nblk):
            kvt_ref[c] = kv_ref[c * SWA_BLOCK:(c + 1) * SWA_BLOCK, :].astype(F32).T.astype(BF16)

    group = SWA_Q_HEADS // SWA_KV_HEADS
    half = LANES // 2

    def swap_halves(a):
        return jnp.concatenate([a[half:], a[:half]], axis=0)

    top = lax.broadcasted_iota(I32, (LANES, nq), 0) < half
    sink = jnp.concatenate([jnp.full((1, nq), sink_ref[kvh * group + g], F32) for g in range(group)], axis=1)

    scores, kvts = [], []
    for sub in range(nsub):
        n = nsub * pl.program_id(2) + sub
        b0 = jnp.clip(n - 1, 0, nblk - wb)
        start = pl.multiple_of(b0 * SWA_BLOCK, SWA_BLOCK)
        cols = []
        for c in range(group // 2):
            qt = q_ref[sub * nq:(sub + 1) * nq, c * LANES:(c + 1) * LANES].astype(F32).T
            cols.append(jnp.where(top, qt, 0.0))
            cols.append(jnp.where(top, swap_halves(qt), 0.0))
        rhs = jnp.concatenate(cols, axis=1).astype(BF16)
        st = jnp.dot(kv_ref[pl.ds(start, nk), :], rhs, preferred_element_type=F32)
        kpos = start + lax.broadcasted_iota(I32, (nk, nq), 0)
        qpos = n * SWA_BLOCK + lax.broadcasted_iota(I32, (nk, nq), 1)
        mask = jnp.where(jnp.abs(kpos - qpos) <= SWA_WINDOW, 0.0, NEG_BIG)
        scores.append(st + jnp.concatenate([mask] * group, axis=1))
        kvts.append(jnp.concatenate([kvt_ref[b0 + w] for w in range(wb)], axis=1))
    for sub in range(nsub):
        st = scores[sub]
        m = jnp.maximum(jnp.max(st, axis=0, keepdims=True), sink)
        e = jnp.exp2(st - m)
        den = jnp.sum(e, axis=0, keepdims=True) + jnp.exp2(sink - m)
        ot = jnp.dot(kvts[sub], e.astype(BF16), preferred_element_type=F32) / den
        for c in range(group // 2):
            even = ot[:, 2 * c * nq:(2 * c + 1) * nq]
            odd = ot[:, (2 * c + 1) * nq:(2 * c + 2) * nq]
            blk = jnp.where(top, swap_halves(even), odd)
            o_ref[sub * nq:(sub + 1) * nq, c * LANES:(c + 1) * LANES] = blk.T.astype(o_ref.dtype)


def _swa_attention(hcat, sinks, batch, seq):
    t = hcat.shape[0]
    nb = seq // SWA_BLOCK
    group = SWA_Q_HEADS // SWA_KV_HEADS
    qw = group * HEAD_DIM
    kvoff = SWA_Q_HEADS * HEAD_DIM // LANES
    nsub = min(SWA_SUB, nb)
    nstep = nb // nsub
    grid_spec = pltpu.PrefetchScalarGridSpec(
        num_scalar_prefetch=1,
        grid=(batch, SWA_KV_HEADS, nstep),
        in_specs=[
            pl.BlockSpec((nsub * SWA_BLOCK, qw), lambda b, h, n, s: (b * nstep + n, h)),
            pl.BlockSpec((seq, LANES), lambda b, h, n, s: (b, kvoff + h)),
        ],
        out_specs=pl.BlockSpec((nsub * SWA_BLOCK, qw), lambda b, h, n, s: (b * nstep + n, h)),
        scratch_shapes=[pltpu.VMEM((nb, LANES, SWA_BLOCK), BF16)],
    )
    return pl.pallas_call(
        _swa_body,
        grid_spec=grid_spec,
        out_shape=jax.ShapeDtypeStruct((t, SWA_Q_HEADS * HEAD_DIM), BF16),
        compiler_params=_cparams(("parallel", "parallel", "arbitrary")),
        name="swa_attn",
    )(sinks.astype(F32) * LOG2E, hcat, hcat)


def _outproj_body(*refs, n_in):
    a_refs = refs[:n_in]
    w_ref, x_ref, g_ref, b_ref, whi_ref, wlo_ref, rbias_ref = refs[n_in:n_in + 7]
    o_ref, opk_ref, idx_ref, gate_ref, pos_ref, cnt_ref, base_ref = refs[n_in + 7:]
    acc = None
    off = 0
    for a_ref in a_refs:
        ka = a_ref.shape[1]
        d = jnp.dot(a_ref[...], w_ref[off:off + ka, :], preferred_element_type=F32)
        acc = d if acc is None else acc + d
        off += ka
    y = DN_ALPHA * x_ref[...] + acc
    out = _layer_norm(y, g_ref[...], b_ref[...])
    o_ref[...] = out
    packed = _pack_rows(out)
    for c in range(PACK_SPLIT):
        opk_ref[c] = packed[:, c * LANES:(c + 1) * LANES]
    _route(out, whi_ref, wlo_ref, rbias_ref, idx_ref, gate_ref, pos_ref, cnt_ref, base_ref)


def _outproj_ln_route(acts, w_bf, x2, g, b, router_w, router_bias, tm=512):
    t, d = x2.shape
    tm = min(tm, t)
    wt = router_w.astype(F32).T
    whi = wt.astype(BF16)
    wlo = (wt - whi.astype(F32)).astype(BF16)
    in_specs = [pl.BlockSpec((tm, a.shape[1]), lambda i: (i, 0)) for a in acts]
    in_specs += [pl.BlockSpec(w_bf.shape, lambda i: (0, 0)),
                 pl.BlockSpec((tm, d), lambda i: (i, 0)),
                 pl.BlockSpec((1, d), lambda i: (0, 0)),
                 pl.BlockSpec((1, d), lambda i: (0, 0)),
                 pl.BlockSpec((N_EXPERTS, d), lambda i: (0, 0)),
                 pl.BlockSpec((N_EXPERTS, d), lambda i: (0, 0)),
                 pl.BlockSpec((N_EXPERTS, 1), lambda i: (0, 0))]
    return pl.pallas_call(
        functools.partial(_outproj_body, n_in=len(acts)),
        grid=(t // tm,),
        in_specs=in_specs,
        out_specs=[pl.BlockSpec((tm, d), lambda i: (i, 0)),
                   pl.BlockSpec((PACK_SPLIT, tm, LANES), lambda i: (0, i, 0)),
                   pl.BlockSpec((TOP_K, tm), lambda i: (0, i)),
                   pl.BlockSpec((TOP_K, tm), lambda i: (0, i)),
                   pl.BlockSpec((TOP_K, tm), lambda i: (0, i)),
                   pl.BlockSpec((N_EXPERTS, 1), lambda i: (0, 0))],
        out_shape=[jax.ShapeDtypeStruct((t, d), F32),
                   jax.ShapeDtypeStruct((PACK_SPLIT, t, LANES), I32),
                   jax.ShapeDtypeStruct((TOP_K, t), I32),
                   jax.ShapeDtypeStruct((TOP_K, t), F32),
                   jax.ShapeDtypeStruct((TOP_K, t), I32),
                   jax.ShapeDtypeStruct((N_EXPERTS, 1), F32)],
        scratch_shapes=[pltpu.VMEM((N_EXPERTS, 1), F32)],
        compiler_params=_cparams(("arbitrary",)),
        name="outproj_ln_route",
    )(*acts, w_bf, x2, g.astype(F32).reshape(1, d), b.astype(F32).reshape(1, d),
      whi, wlo, router_bias.astype(F32).reshape(N_EXPERTS, 1))


def _route(h, whi_ref, wlo_ref, bias_ref, idx_ref, gate_ref, pos_ref, cnt_ref, base_ref):
    i = pl.program_id(0)
    tm = h.shape[0]
    gsz = N_EXPERTS // N_GROUPS

    @pl.when(i == 0)
    def _():
        base_ref[...] = jnp.zeros_like(base_ref)

    h_hi = h.astype(BF16)
    h_lo = (h - h_hi.astype(F32)).astype(BF16)
    dn = (((1,), (1,)), ((), ()))
    whi = whi_ref[...]
    logits = (lax.dot_general(whi, h_hi, dn, preferred_element_type=F32)
              + lax.dot_general(whi, h_lo, dn, preferred_element_type=F32)
              + lax.dot_general(wlo_ref[...], h_hi, dn, preferred_element_type=F32))
    scores = 1.0 / (1.0 + jnp.exp(-logits))
    choice = scores + bias_ref[...]

    iota_g = lax.broadcasted_iota(I32, (gsz, tm), 0)
    gscore = []
    for g in range(N_GROUPS):
        cg = choice[g * gsz:(g + 1) * gsz, :]
        m1 = jnp.max(cg, axis=0, keepdims=True)
        first = jnp.min(jnp.where(cg == m1, iota_g, gsz), axis=0, keepdims=True)
        m2 = jnp.max(jnp.where(iota_g == first, -jnp.inf, cg), axis=0, keepdims=True)
        gscore.append(m1 + m2)
    pieces = []
    for g in range(N_GROUPS):
        rank = jnp.zeros((1, tm), I32)
        for o in range(N_GROUPS):
            if o == g:
                continue
            beats = (gscore[o] > gscore[g]) if o > g else (gscore[o] >= gscore[g])
            rank = rank + beats.astype(I32)
        keep = rank < TOPK_GROUPS
        pieces.append(jnp.where(keep, choice[g * gsz:(g + 1) * gsz, :], -jnp.inf))
    masked = jnp.concatenate(pieces, axis=0)

    iota_e = lax.broadcasted_iota(I32, (N_EXPERTS, tm), 0)
    sel_all = jnp.zeros((N_EXPERTS, tm), F32)
    idxs, gates, sels = [], [], []
    for _ in range(TOP_K):
        mx = jnp.max(masked, axis=0, keepdims=True)
        idx = jnp.min(jnp.where(masked == mx, iota_e, N_EXPERTS), axis=0, keepdims=True)
        sel = iota_e == idx
        gates.append(jnp.sum(jnp.where(sel, scores, 0.0), axis=0, keepdims=True))
        masked = jnp.where(sel, -jnp.inf, masked)
        sel_all = sel_all + sel.astype(F32)
        idxs.append(idx)
        sels.append(sel)
    gsum = gates[0]
    for gk in gates[1:]:
        gsum = gsum + gk
    gate_ref[...] = jnp.concatenate(gates, axis=0) / gsum * ROUTED_SCALE
    idx_ref[...] = jnp.concatenate(idxs, axis=0)

    tri = (lax.broadcasted_iota(I32, (tm, tm), 0) < lax.broadcasted_iota(I32, (tm, tm), 1))
    cum = jnp.dot(sel_all.astype(BF16), tri.astype(F32).astype(BF16), preferred_element_type=F32)
    tot = cum + base_ref[...]
    pos = [jnp.sum(jnp.where(sel, tot, 0.0), axis=0, keepdims=True) for sel in sels]
    pos_ref[...] = jnp.concatenate(pos, axis=0).astype(I32)
    base_ref[...] = base_ref[...] + jnp.sum(sel_all, axis=1, keepdims=True)
    cnt_ref[...] = base_ref[...]


def _expert_body(be_ref, na_ref, bv_ref, ord_ref, nxt_ref, x_ref, wg_hbm, wu_hbm, wd_hbm, y_ref,
                 wgu, wdb, wg_buf, wu_buf, wd_buf, sem):
    i = pl.program_id(0)
    bm = x_ref.shape[1]
    ff = wd_hbm.shape[1]
    hm = bm // EXPERT_CHAINS

    def weight_copies(e, slot):
        return [pltpu.make_async_copy(hbm.at[e], buf.at[slot], sem.at[slot])
                for hbm, buf in ((wg_hbm, wg_buf), (wu_hbm, wu_buf), (wd_hbm, wd_buf))]

    @pl.when(i < na_ref[0])
    def _():
        prev = be_ref[jnp.maximum(i - 1, 0)]
        slot = ord_ref[i] % 2

        @pl.when(i == 0)
        def _():
            for cp in weight_copies(be_ref[0], 0):
                cp.start()

        @pl.when((i == 0) | (be_ref[i] != prev))
        def _():
            for cp in weight_copies(be_ref[i], slot):
                cp.wait()
            wgu[:, :ff] = wg_buf[slot].astype(BF16)
            wgu[:, ff:] = wu_buf[slot].astype(BF16)
            wdb[...] = wd_buf[slot].astype(BF16)

            @pl.when(nxt_ref[i] >= 0)
            def _():
                for cp in weight_copies(nxt_ref[i], 1 - slot):
                    cp.start()

        valid = bv_ref[i]
        row = lax.broadcasted_iota(I32, (hm, LANES), 0)
        def run(n_chains):
            gus = []
            for hb in range(n_chains):
                rows = slice(hb * hm, (hb + 1) * hm)
                keep = row < valid - hb * hm
                x = _unpack_rows(lambda c: jnp.where(keep, x_ref[c, rows, :], 0)).astype(BF16)
                gus.append(jnp.dot(x, wgu[...], preferred_element_type=F32))
            ys = []
            for gu in gus:
                hmid = (_silu(gu[:, :ff]) * gu[:, ff:]).astype(BF16)
                ys.append(jnp.dot(hmid, wdb[...], preferred_element_type=F32))
            for hb, y in enumerate(ys):
                packed = _pack_rows(y)
                for c in range(PACK_SPLIT):
                    y_ref[c, hb * hm:(hb + 1) * hm, :] = packed[:, c * LANES:(c + 1) * LANES]
            if n_chains < EXPERT_CHAINS:
                y_ref[:, n_chains * hm:, :] = jnp.zeros((PACK_SPLIT, bm - n_chains * hm, LANES), I32)

        for n_chains in range(1, EXPERT_CHAINS + 1):
            lo = (n_chains - 1) * hm
            cond = valid > lo if n_chains == EXPERT_CHAINS else (valid > lo) & (valid <= lo + hm)
            pl.when(cond)(functools.partial(run, n_chains))


def _expert_matmul(xs, blk_expert, n_active, blk_valid, w_gate, w_up, w_down):
    n_slots = xs.shape[1]
    bm = EXPERT_BLOCK
    n_blocks = n_slots // bm
    d, ff = w_gate.shape[1], w_gate.shape[2]

    blk = jnp.arange(n_blocks, dtype=I32)
    change = jnp.concatenate([jnp.zeros((1,), I32), (blk_expert[1:] != blk_expert[:-1]).astype(I32)])
    ordinal = jnp.sum(jnp.where(blk[None, :] <= blk[:, None], change[None, :], 0), axis=1).astype(I32)
    later = (ordinal[None, :] == ordinal[:, None] + 1) & (blk[None, :] < n_active[0])
    nxt = jnp.max(jnp.where(later, blk_expert[None, :], -1), axis=1).astype(I32)

    def row_map(i, be, na, bv, od, nx):
        return (0, jnp.minimum(i, na[0] - 1), 0)

    grid_spec = pltpu.PrefetchScalarGridSpec(
        num_scalar_prefetch=5,
        grid=(n_blocks,),
        in_specs=[pl.BlockSpec((PACK_SPLIT, bm, LANES), row_map),
                  pl.BlockSpec(memory_space=pl.ANY),
                  pl.BlockSpec(memory_space=pl.ANY),
                  pl.BlockSpec(memory_space=pl.ANY)],
        out_specs=pl.BlockSpec((PACK_SPLIT, bm, LANES), row_map),
        scratch_shapes=[pltpu.VMEM((d, 2 * ff), BF16), pltpu.VMEM((ff, d), BF16),
                        pltpu.VMEM((2, d, ff), F32), pltpu.VMEM((2, d, ff), F32), pltpu.VMEM((2, ff, d), F32),
                        pltpu.SemaphoreType.DMA((2,))],
    )
    return pl.pallas_call(
        _expert_body,
        grid_spec=grid_spec,
        out_shape=jax.ShapeDtypeStruct((PACK_SPLIT, n_slots, LANES), I32),
        compiler_params=_cparams(("arbitrary",)),
        name="expert_mlp",
    )(blk_expert, n_active, blk_valid, ordinal, nxt, xs, w_gate, w_up, w_down)


def _slot_body(idx_ref, pos_ref, ps_ref, slot_ref, *, n_slots):
    tm = idx_ref.shape[1]
    iota_e = lax.broadcasted_iota(I32, (N_EXPERTS, tm), 0)
    ps = ps_ref[...]
    rows = []
    for k in range(TOP_K):
        start = jnp.sum(jnp.where(iota_e == idx_ref[k:k + 1, :], ps, 0.0), axis=0, keepdims=True)
        rows.append(start.astype(I32) + pos_ref[k:k + 1, :])
    slot = jnp.concatenate(rows, axis=0)
    for c in range(PACK_SPLIT):
        slot_ref[c * TOP_K:(c + 1) * TOP_K, :] = slot + c * n_slots


def _slots(idx, pos, pad_start, n_slots, tm=512):
    t = idx.shape[1]
    tm = min(tm, t)
    return pl.pallas_call(
        functools.partial(_slot_body, n_slots=n_slots),
        grid=(t // tm,),
        in_specs=[pl.BlockSpec((TOP_K, tm), lambda i: (0, i)),
                  pl.BlockSpec((TOP_K, tm), lambda i: (0, i)),
                  pl.BlockSpec((N_EXPERTS, 1), lambda i: (0, 0))],
        out_specs=pl.BlockSpec((PACK_SPLIT * TOP_K, tm), lambda i: (0, i)),
        out_shape=jax.ShapeDtypeStruct((PACK_SPLIT * TOP_K, t), I32),
        compiler_params=_cparams(("parallel",)),
        name="slots",
    )(idx, pos, pad_start.astype(F32).reshape(N_EXPERTS, 1))


def _sc_mesh():
    return plsc.VectorSubcoreMesh(core_axis_name="core", subcore_axis_name="subcore",
                                  num_cores=SC_CORES, num_subcores=SC_SUBCORES)


def _sc_scatter_rows(rows, idx, n_out):
    t = idx.shape[1]
    nj = t // SC_WINDOW

    @functools.partial(pl.kernel, out_type=jax.ShapeDtypeStruct((n_out, LANES), I32),
                       mesh=_sc_mesh(), scratch_types=[], name="sc_dispatch")
    def run(rows_hbm, idx_hbm, out_hbm):
        def body(rows_vmem, idx_vmem):
            for k in range(TOP_K):
                pltpu.sync_copy(rows_vmem, out_hbm.at[idx_vmem.at[k]])

        pltpu.emit_pipeline(
            body,
            grid=(rows.shape[0] // SC_WINDOW,),
            in_specs=[pl.BlockSpec((SC_WINDOW, LANES), lambda s: (s, 0)),
                      pl.BlockSpec((TOP_K, SC_WINDOW), lambda s: (s // nj, s % nj))],
            out_specs=[],
            core_axis_name=("core", "subcore"),
            dimension_semantics=(pltpu.PARALLEL,),
        )(rows_hbm, idx_hbm)

    return run(rows, idx)


def _sc_gather_rows(src, idx):
    nr, t = idx.shape
    nj = t // SC_WINDOW

    @functools.partial(pl.kernel, out_type=jax.ShapeDtypeStruct((nr * t, LANES), I32),
                       mesh=_sc_mesh(), scratch_types=[], name="sc_combine")
    def run(src_hbm, idx_hbm, out_hbm):
        def body(idx_vmem, out_vmem):
            pltpu.sync_copy(src_hbm.at[idx_vmem.at[0]], out_vmem)

        pltpu.emit_pipeline(
            body,
            grid=(nr * nj,),
            in_specs=[pl.BlockSpec((1, SC_WINDOW), lambda s: (s // nj, s % nj))],
            out_specs=[pl.BlockSpec((SC_WINDOW, LANES), lambda s: (s, 0))],
            core_axis_name=("core", "subcore"),
            dimension_semantics=(pltpu.PARALLEL,),
        )(idx_hbm, out_hbm)

    return run(src, idx)


def _combine_body(h_ref, yg_ref, gt_ref, sg_ref, su_ref, sd_ref, g_ref, b_ref, *rest, proj, n_alias):
    n_out = 1 if proj is None else 2
    rest = rest[:len(rest) - n_out - n_alias] + rest[len(rest) - n_out:]
    h = h_ref[...]
    hb = h.astype(BF16)
    a = jnp.dot(hb, sg_ref[...], preferred_element_type=F32)
    u = jnp.dot(hb, su_ref[...], preferred_element_type=F32)
    moe = jnp.dot((_silu(a) * u).astype(BF16), sd_ref[...], preferred_element_type=F32)
    gt = gt_ref[...]
    for k in range(TOP_K):
        moe = moe + gt[:, k:k + 1] * _unpack_rows(lambda c, k=k: yg_ref[c, k])
    out = _layer_norm(DN_ALPHA * h + moe, g_ref[...], b_ref[...])
    if proj is None:
        rest[0][...] = out
    else:
        w_ref, tab_ref, o_ref, hcat_ref = rest
        o_ref[...] = out
        _project_rope(out.astype(BF16), w_ref, tab_ref, hcat_ref, *proj)


def _combine_ln(h, yg, gates_t, sh_gate, sh_up, sh_down, g, b, row0, prev, next_proj=None, tm=256):
    t, d = h.shape
    tc = yg.shape[2]
    tm = min(tm, tc) if next_proj is None else min(tm, tc, next_proj[3])
    off = row0 // tm
    ff = sh_gate.shape[1]
    in_specs = [pl.BlockSpec((tm, d), lambda i: (i + off, 0)),
                pl.BlockSpec((PACK_SPLIT, TOP_K, tm, LANES), lambda i: (0, 0, i, 0)),
                pl.BlockSpec((tm, TOP_K), lambda i: (i, 0)),
                pl.BlockSpec((d, ff), lambda i: (0, 0)),
                pl.BlockSpec((d, ff), lambda i: (0, 0)),
                pl.BlockSpec((ff, d), lambda i: (0, 0)),
                pl.BlockSpec((1, d), lambda i: (0, 0)),
                pl.BlockSpec((1, d), lambda i: (0, 0))]
    out_specs = [pl.BlockSpec((tm, d), lambda i: (i + off, 0))]
    out_shape = [jax.ShapeDtypeStruct((t, d), F32)]
    operands = [h, yg, gates_t, sh_gate.astype(BF16), sh_up.astype(BF16), sh_down.astype(BF16),
                g.astype(F32).reshape(1, d), b.astype(F32).reshape(1, d)]
    proj = None
    if next_proj is not None:
        w_bf, tab, rope_kind, seq = next_proj
        sb = seq // tm
        n_out = w_bf.shape[1]
        in_specs += [pl.BlockSpec((d, n_out), lambda i: (0, 0)),
                     pl.BlockSpec((tm, tab.shape[1]), lambda i: ((i + off) % sb, 0))]
        out_specs.append(pl.BlockSpec((tm, n_out), lambda i: (i + off, 0)))
        out_shape.append(jax.ShapeDtypeStruct((t, n_out), BF16))
        operands += [w_bf, tab]
        proj = (PROJ_CHUNK, tuple(rope_kind))
    aliases = {}
    if prev is not None:
        for k, p in enumerate(prev):
            aliases[len(operands)] = k
            in_specs.append(pl.BlockSpec(memory_space=pl.ANY))
            operands.append(p)
    return pl.pallas_call(
        functools.partial(_combine_body, proj=proj, n_alias=len(aliases)),
        grid=(tc // tm,),
        in_specs=in_specs,
        out_specs=out_specs,
        out_shape=out_shape,
        input_output_aliases=aliases,
        compiler_params=_cparams(("parallel",)),
        name="combine_ln",
    )(*operands)


def _mixer_out_and_moe(acts, w_out, x2, ln1_g, ln1_b, router_w, router_bias, w_gate, w_up, w_down,
                       sh_gate, sh_up, sh_down, g, b, next_proj=None):
    h, h_pk, idx, gates, pos, cnt = _outproj_ln_route(acts, w_out.astype(BF16), x2, ln1_g, ln1_b,
                                                      router_w, router_bias)
    t, d = h.shape
    bm = EXPERT_BLOCK
    counts = cnt[:, 0].astype(I32)
    padded = (counts + bm - 1) // bm * bm
    e_iota = jnp.arange(N_EXPERTS, dtype=I32)
    pad_end = jnp.sum(jnp.where(e_iota[None, :] <= e_iota[:, None], padded[None, :], 0), axis=1)
    pad_start = pad_end - padded
    n_blocks = t * TOP_K // bm + N_EXPERTS
    n_slots = n_blocks * bm
    blk_start = jnp.arange(n_blocks, dtype=I32) * bm
    blk_expert = jnp.minimum(
        jnp.sum((blk_start[:, None] >= pad_end[None, :]).astype(I32), axis=1), N_EXPERTS - 1)
    real_end = jnp.sum(jnp.where(e_iota[None, :] == blk_expert[:, None], (pad_start + counts)[None, :], 0), axis=1)
    blk_valid = jnp.clip(real_end - blk_start, 0, bm).astype(I32)
    n_active = (pad_end[-1:] // bm).astype(I32)

    slot4 = _slots(idx, pos, pad_start, n_slots)
    xs = _sc_scatter_rows(h_pk.reshape(PACK_SPLIT * t, LANES), slot4, PACK_SPLIT * n_slots)
    ys = _expert_matmul(xs.reshape(PACK_SPLIT, n_slots, LANES), blk_expert, n_active, blk_valid,
                        w_gate, w_up, w_down)
    ys_flat = ys.reshape(PACK_SPLIT * n_slots, LANES)
    gates_t = gates.T
    tc = t // COMBINE_CHUNKS
    outs = None
    for c in range(COMBINE_CHUNKS):
        rows = slice(c * tc, (c + 1) * tc)
        yg = _sc_gather_rows(ys_flat, slot4[:, rows])
        outs = _combine_ln(h, yg.reshape(PACK_SPLIT, TOP_K, tc, LANES), gates_t[rows], sh_gate, sh_up, sh_down,
                           g, b, c * tc, outs, next_proj=next_proj)
    return outs if next_proj is not None else outs[0]


def kernel(x, l0_w_in, l0_w_out, l0_na_rpb, l0_diff_lq1, l0_diff_lk1, l0_diff_lq2, l0_diff_lk2, l0_diff_subln_g, l0_ln1_g, l0_ln1_b, l0_router_w, l0_router_bias, l0_expert_w_gate, l0_expert_w_up, l0_expert_w_down, l0_shared_w_gate, l0_shared_w_up, l0_shared_w_down, l0_ln2_g, l0_ln2_b, l1_w_in, l1_w_out, l1_swa_sinks, l1_ln1_g, l1_ln1_b, l1_router_w, l1_router_bias, l1_expert_w_gate, l1_expert_w_up, l1_expert_w_down, l1_shared_w_gate, l1_shared_w_up, l1_shared_w_down, l1_ln2_g, l1_ln2_b):
    batch, seq, d = x.shape
    t = batch * seq
    x2 = x.reshape(t, d).astype(F32)
    tab = _rope_table(seq)
    qscale = HEAD_DIM ** -0.5 * LOG2E
    na_w = NA_HEADS * HEAD_DIM
    dq_w = DIFF_HEADS * 2 * HEAD_DIM

    col_scale = jnp.concatenate([
        jnp.full((na_w,), qscale, F32), jnp.ones((2 * na_w,), F32),
        jnp.full((dq_w,), qscale, F32), jnp.ones((2 * dq_w,), F32)])
    w_in0 = (l0_w_in.astype(F32) * col_scale).astype(BF16)
    per = LANES
    rope0 = [0] * (3 * na_w // per) + [1] * (2 * dq_w // per) + [0] * (dq_w // per)
    hcat0 = _inproj(x2, w_in0, tab, rope0, seq)
    oa = _na_attention(hcat0, l0_na_rpb, batch, seq)
    lambda_init = 0.8 - 0.6 * math.exp(-0.3 * 0)
    lam = (jnp.exp(jnp.sum(l0_diff_lq1.astype(F32) * l0_diff_lk1.astype(F32)))
           - jnp.exp(jnp.sum(l0_diff_lq2.astype(F32) * l0_diff_lk2.astype(F32))) + lambda_init)
    od = _diff_attention(hcat0, lam.reshape(1).astype(F32), l0_diff_subln_g, lambda_init, batch, seq)
    q_w = SWA_Q_HEADS * HEAD_DIM
    kv_w = SWA_KV_HEADS * HEAD_DIM
    w1 = l1_w_in.astype(F32)
    wq = w1[:, :q_w] * qscale
    wk = w1[:, q_w:q_w + kv_w].reshape(d, SWA_KV_HEADS, HEAD_DIM)
    wv = w1[:, q_w + kv_w:].reshape(d, SWA_KV_HEADS, HEAD_DIM)
    wkv = jnp.concatenate([wk, wv], axis=-1).reshape(d, 2 * kv_w)
    w_in1 = jnp.concatenate([wq, wkv], axis=1).astype(BF16)
    rope1 = [1] * (q_w // per) + [2] * (2 * kv_w // per)

    x2, hcat1 = _mixer_out_and_moe(
        [oa, od], l0_w_out, x2, l0_ln1_g, l0_ln1_b, l0_router_w, l0_router_bias,
        l0_expert_w_gate, l0_expert_w_up, l0_expert_w_down,
        l0_shared_w_gate, l0_shared_w_up, l0_shared_w_down, l0_ln2_g, l0_ln2_b,
        next_proj=(w_in1, tab, rope1, seq))

    o1 = _swa_attention(hcat1, l1_swa_sinks, batch, seq)
    x2 = _mixer_out_and_moe(
        [o1], l1_w_out, x2, l1_ln1_g, l1_ln1_b, l1_router_w, l1_router_bias,
        l1_expert_w_gate, l1_expert_w_up, l1_expert_w_down,
        l1_shared_w_gate, l1_shared_w_up, l1_shared_w_down, l1_ln2_g, l1_ln2_b)
    return x2.reshape(batch, seq, d).astype(x.dtype)
```

```python
import functools
import math

import numpy as np
import jax
import jax.numpy as jnp
from jax import lax
from jax.experimental import pallas as pl
from jax.experimental.pallas import tpu as pltpu
from jax.experimental.pallas import tpu_sc as plsc

F32 = jnp.float32
BF16 = jnp.bfloat16
I32 = jnp.int32

HEAD_DIM = 64
GRID_W = 64
NA_HEADS = 8
NA_WIN_ROWS = 8
NA_WIN_COLS = 16
DIFF_HEADS = 4
SWA_Q_HEADS = 16
SWA_KV_HEADS = 4
SWA_WINDOW = 128
SWA_BLOCK = 128
ROPE_THETA = 10000.0
N_EXPERTS = 256
TOP_K = 8
N_GROUPS = 8
TOPK_GROUPS = 4
ROUTED_SCALE = 2.5
LN_EPS = 1e-5
DEPTH = 2
DN_ALPHA = (2 * DEPTH) ** 0.25

LOG2E = 1.4426950408889634
NEG_BIG = -3.0e38
LANES = 128
NA_KEY_ROWS = NA_WIN_ROWS + 1
PROJ_CHUNK = 256
SWA_SUB = 8
NA_SUB = 8
EXPERT_BLOCK = 512
COMBINE_CHUNKS = 8
EXPERT_CHAINS = 2
DIFF_ONES_ROWS = 16
PACK_SPLIT = 4
SC_CORES = 2
SC_SUBCORES = 16
SC_WINDOW = 128
VMEM_LIMIT = 56 * 1024 * 1024


def _cparams(sem):
    return pltpu.CompilerParams(dimension_semantics=sem, vmem_limit_bytes=VMEM_LIMIT)


def _silu(x):
    return x / (1.0 + jnp.exp(-x))


def _pack_rows(y):
    half = y.shape[1] // 2
    bits = pltpu.bitcast(y.astype(BF16).astype(F32), I32)
    lo = lax.shift_right_logical(bits[:, :half], 16)
    hi = bits[:, half:] & jnp.int32(-65536)
    return hi | lo


def _unpack_words(w):
    lo = pltpu.bitcast(lax.shift_left(w, 16), F32)
    hi = pltpu.bitcast(w & jnp.int32(-65536), F32)
    return lo, hi


def _unpack_rows(ref_at):
    los, his = [], []
    for c in range(PACK_SPLIT):
        lo, hi = _unpack_words(ref_at(c))
        los.append(lo)
        his.append(hi)
    return jnp.concatenate(los + his, axis=1)


def _layer_norm(y, g, b):
    mu = jnp.mean(y, axis=-1, keepdims=True)
    yc = y - mu
    var = jnp.mean(yc * yc, axis=-1, keepdims=True)
    return yc * lax.rsqrt(var + LN_EPS) * g + b


def _inproj_body(x_ref, w_ref, tab_ref, o_ref, *, chunk, rope_kind):
    _project_rope(x_ref[...].astype(BF16), w_ref, tab_ref, o_ref, chunk, rope_kind)


def _project_rope(x, w_ref, tab_ref, o_ref, chunk, rope_kind):
    tm = x.shape[0]
    n_out = w_ref.shape[1]
    lane = lax.broadcasted_iota(I32, (tm, LANES), 1)
    first_half = (lane % HEAD_DIM) < (HEAD_DIM // 2)
    per = chunk // LANES
    for c in range(n_out // chunk):
        h = jnp.dot(x, w_ref[:, c * chunk:(c + 1) * chunk], preferred_element_type=F32)
        kinds = rope_kind[c * per:(c + 1) * per]
        if any(kinds):
            pieces = []
            for j, kind in enumerate(kinds):
                t = h[:, j * LANES:(j + 1) * LANES]
                if kind:
                    base = (kind - 1) * 2 * LANES
                    cos = tab_ref[:, base:base + LANES]
                    sin = tab_ref[:, base + LANES:base + 2 * LANES]
                    rot = jnp.where(first_half, pltpu.roll(t, LANES - HEAD_DIM // 2, 1),
                                    pltpu.roll(t, HEAD_DIM // 2, 1))
                    t = t * cos + rot * sin
                pieces.append(t)
            h = jnp.concatenate(pieces, axis=1)
        o_ref[:, c * chunk:(c + 1) * chunk] = h.astype(o_ref.dtype)


def _inproj(x2, w_bf, tab, rope_kind, seq, tm=512):
    t, d = x2.shape
    n_out = w_bf.shape[1]
    tm = min(tm, seq)
    sb = seq // tm
    return pl.pallas_call(
        functools.partial(_inproj_body, chunk=PROJ_CHUNK, rope_kind=tuple(rope_kind)),
        grid=(t // tm,),
        in_specs=[pl.BlockSpec((tm, d), lambda i: (i, 0)),
                  pl.BlockSpec((d, n_out), lambda i: (0, 0)),
                  pl.BlockSpec((tm, tab.shape[1]), lambda i: (i % sb, 0))],
        out_specs=pl.BlockSpec((tm, n_out), lambda i: (i, 0)),
        out_shape=jax.ShapeDtypeStruct((t, n_out), BF16),
        compiler_params=_cparams(("parallel",)),
        name="inproj",
    )(x2, w_bf, tab)


def _rope_table(seq):
    inv = 1.0 / (ROPE_THETA ** (jnp.arange(0, HEAD_DIM, 2, dtype=F32) / HEAD_DIM))
    ang = jnp.arange(seq, dtype=F32)[:, None] * inv[None, :]
    cos = jnp.cos(ang)
    sin = jnp.sin(ang)
    cos64 = jnp.concatenate([cos, cos], axis=-1)
    sin64 = jnp.concatenate([-sin, sin], axis=-1)
    one = jnp.ones((seq, HEAD_DIM), F32)
    zero = jnp.zeros((seq, HEAD_DIM), F32)
    return jnp.concatenate([cos64, cos64, sin64, sin64, cos64, one, sin64, zero], axis=-1)


def _na_body(pat_ref, q_ref, k_ref, v_ref, *rest, rows_n):
    del pat_ref
    bias_refs, o_ref = rest[:NA_SUB], rest[NA_SUB]
    nq = q_ref.shape[0] // NA_SUB
    nk = NA_KEY_ROWS * GRID_W
    low = lax.broadcasted_iota(I32, (nq, LANES), 1) < HEAD_DIM
    scores, vwins = [], []
    for sub in range(NA_SUB):
        r = NA_SUB * pl.program_id(2) + sub
        ks = jnp.clip(2 * r - NA_WIN_ROWS // 2, 0, rows_n - NA_KEY_ROWS)
        start = pl.multiple_of(ks * GRID_W, GRID_W)
        q = q_ref[sub * nq:(sub + 1) * nq, :].astype(F32)
        qm = jnp.concatenate([jnp.where(low, q, 0.0), jnp.where(low, 0.0, q)], axis=0).astype(BF16)
        s = lax.dot_general(qm, k_ref[pl.ds(start, nk), :], (((1,), (1,)), ((), ())),
                            preferred_element_type=F32)
        scores.append(s + bias_refs[sub][0].reshape(2 * nq, nk))
        vwins.append(v_ref[pl.ds(start, nk), :])
    for sub in range(NA_SUB):
        s = scores[sub]
        m = jnp.max(s, axis=-1, keepdims=True)
        p = jnp.exp2(s - m)
        l = jnp.sum(p, axis=-1, keepdims=True)
        pv = jnp.dot(p.astype(BF16), vwins[sub], preferred_element_type=F32) / l
        o_ref[sub * nq:(sub + 1) * nq, :] = jnp.where(low, pv[:nq], pv[nq:]).astype(o_ref.dtype)


def _na_bias(rpb, rows_n):
    half = NA_WIN_ROWS // 2
    pats, pat_id = [], []
    for blk in range(rows_n // 2):
        r0 = 2 * blk
        ks = min(max(r0 - half, 0), rows_n - NA_KEY_ROWS)
        starts = tuple(min(max(r0 + i - half, 0), rows_n - NA_WIN_ROWS) - ks for i in range(2))
        key = (r0 - ks, starts)
        if key not in pats:
            pats.append(key)
        pat_id.append(pats.index(key))
    wq = np.arange(GRID_W)
    wk = np.arange(GRID_W)
    col_start = np.clip(wq - NA_WIN_COLS // 2, 0, GRID_W - NA_WIN_COLS)
    col_off = wk[None, :] - col_start[:, None]
    col_valid = (col_off >= 0) & (col_off < NA_WIN_COLS)
    dc = np.clip(wk[None, :] - wq[:, None], -(NA_WIN_COLS - 1), NA_WIN_COLS - 1) + (NA_WIN_COLS - 1)
    n_dc = 2 * NA_WIN_COLS - 1
    onehot = (dc.reshape(-1)[None, :] == np.arange(n_dc)[:, None]).astype(np.float32)
    heads, n_dr = rpb.shape[0], rpb.shape[1]
    col = jnp.dot(rpb.astype(F32).reshape(heads * n_dr, n_dc), jnp.asarray(onehot),
                  precision=lax.Precision.HIGHEST).reshape(heads, n_dr, GRID_W, GRID_W)
    col = jnp.where(col_valid[None, None], col * LOG2E, NEG_BIG)
    masked_blk = jnp.full((heads, GRID_W, GRID_W), NEG_BIG, F32)
    tables = []
    for r0rel, starts in pats:
        qrows = []
        for qi in range(2):
            blks = []
            for kr in range(NA_KEY_ROWS):
                row_ok = starts[qi] <= kr < starts[qi] + NA_WIN_ROWS
                dr = kr - (r0rel + qi) + (NA_WIN_ROWS - 1)
                blks.append(col[:, dr] if row_ok else masked_blk)
            qrows.append(jnp.concatenate(blks, axis=-1))
        tables.append(jnp.concatenate(qrows, axis=1))
    return jnp.stack(tables, axis=0), np.asarray(pat_id, np.int32)


def _na_attention(hcat, rpb, batch, seq):
    t = hcat.shape[0]
    rows_n = seq // GRID_W
    nblk = rows_n // 2
    nq = 2 * GRID_W
    bias, pat_id = _na_bias(rpb, rows_n)
    nk = NA_KEY_ROWS * GRID_W
    hp = NA_HEADS // 2
    nstep = nblk // NA_SUB

    def bias_spec(sub):
        return pl.BlockSpec((1, 2, nq, nk), lambda b, h, r, pat: (pat[NA_SUB * r + sub], h, 0, 0))

    grid_spec = pltpu.PrefetchScalarGridSpec(
        num_scalar_prefetch=1,
        grid=(batch, hp, nstep),
        in_specs=[
            pl.BlockSpec((NA_SUB * nq, LANES), lambda b, h, r, pat: (b * nstep + r, h)),
            pl.BlockSpec((seq, LANES), lambda b, h, r, pat: (b, hp + h)),
            pl.BlockSpec((seq, LANES), lambda b, h, r, pat: (b, 2 * hp + h)),
        ] + [bias_spec(sub) for sub in range(NA_SUB)],
        out_specs=pl.BlockSpec((NA_SUB * nq, LANES), lambda b, h, r, pat: (b * nstep + r, h)),
    )
    return pl.pallas_call(
        functools.partial(_na_body, rows_n=rows_n),
        grid_spec=grid_spec,
        out_shape=jax.ShapeDtypeStruct((t, NA_HEADS * HEAD_DIM), BF16),
        compiler_params=_cparams(("parallel", "parallel", "arbitrary")),
        name="na_attn",
    )(jnp.asarray(pat_id), hcat, hcat, hcat, *([bias] * NA_SUB))


def _diff_body(lam_ref, q_ref, k_ref, v_ref, g_ref, o_ref, vt_ref, acc_ref, sta_ref, stb_ref, pa_ref, pb_ref,
               *, tk, lambda_init):
    j = pl.program_id(2)
    tq = q_ref.shape[0]
    seq = k_ref.shape[0]
    nkv = seq // tk

    @pl.when(j == 0)
    def _():
        ones = jnp.ones((DIFF_ONES_ROWS, tk), BF16)
        for c in range(nkv):
            vt_ref[c, :LANES, :] = v_ref[c * tk:(c + 1) * tk, :].astype(F32).T.astype(BF16)
            vt_ref[c, LANES:, :] = ones

    q = q_ref[...].astype(F32)
    lane = lax.broadcasted_iota(I32, (tq, LANES), 1)
    q1 = jnp.where(lane < HEAD_DIM, q, 0.0).T
    q2 = jnp.where(lane >= HEAD_DIM, q, 0.0).T
    rhs = jnp.concatenate([q1, q2], axis=1).astype(BF16)

    def scores(i, dst_ref):
        start = pl.multiple_of(i * tk, tk)
        dst_ref[...] = jnp.dot(k_ref[pl.ds(start, tk), :], rhs, preferred_element_type=F32)

    def softmax(src_ref, dst_ref, m):
        st = src_ref[...]
        m_new = jnp.maximum(m, jnp.max(st, axis=0, keepdims=True))
        dst_ref[...] = jnp.exp2((st - m_new).astype(BF16))
        return m_new, jnp.exp2(m - m_new)

    def weighted_values(i, p_ref, alpha):
        pv = jnp.dot(vt_ref[i], p_ref[...], preferred_element_type=F32)
        acc_ref[...] = alpha * acc_ref[...] + pv

    acc_ref[...] = jnp.zeros_like(acc_ref)
    scores(0, sta_ref)
    scores(1, stb_ref)
    m, a0 = softmax(sta_ref, pa_ref, jnp.full((1, 2 * tq), NEG_BIG, F32))

    def pair(ii, carry):
        m, a0 = carry
        i = 2 * ii
        scores(i + 2, sta_ref)
        m, a1 = softmax(stb_ref, pb_ref, m)
        weighted_values(i, pa_ref, a0)
        scores(i + 3, stb_ref)
        m, a0 = softmax(sta_ref, pa_ref, m)
        weighted_values(i + 1, pb_ref, a1)
        return m, a0

    m, a0 = lax.fori_loop(0, (nkv - 2) // 2, pair, (m, a0))
    m, a1 = softmax(stb_ref, pb_ref, m)
    weighted_values(nkv - 2, pa_ref, a0)
    weighted_values(nkv - 1, pb_ref, a1)
    acc = acc_ref[...]
    ot = acc[:LANES, :] / acc[LANES:LANES + 1, :]
    dt = ot[:, :tq] - lam_ref[0] * ot[:, tq:]
    ms = jnp.mean(dt * dt, axis=0, keepdims=True)
    y = dt * lax.rsqrt(ms + LN_EPS) * g_ref[...] * (1.0 - lambda_init)
    o_ref[...] = y.T.astype(o_ref.dtype)


def _diff_attention(hcat, lam, subln_g, lambda_init, batch, seq, tq=2048, tk=512):
    t = hcat.shape[0]
    tq = min(tq, seq)
    tk = min(tk, seq)
    nq = seq // tq
    qoff = 3 * NA_HEADS * HEAD_DIM // LANES
    koff = qoff + DIFF_HEADS
    voff = koff + DIFF_HEADS
    grid_spec = pltpu.PrefetchScalarGridSpec(
        num_scalar_prefetch=1,
        grid=(batch, DIFF_HEADS, nq),
        in_specs=[
            pl.BlockSpec((tq, LANES), lambda b, h, j, lam: (b * nq + j, qoff + h)),
            pl.BlockSpec((seq, LANES), lambda b, h, j, lam: (b, koff + h)),
            pl.BlockSpec((seq, LANES), lambda b, h, j, lam: (b, voff + h)),
            pl.BlockSpec((LANES, 1), lambda b, h, j, lam: (0, 0)),
        ],
        out_specs=pl.BlockSpec((tq, LANES), lambda b, h, j, lam: (b * nq + j, h)),
        scratch_shapes=[pltpu.VMEM((seq // tk, LANES + DIFF_ONES_ROWS, tk), BF16),
                        pltpu.VMEM((LANES + DIFF_ONES_ROWS, 2 * tq), F32),
                        pltpu.VMEM((tk, 2 * tq), F32),
                        pltpu.VMEM((tk, 2 * tq), F32),
                        pltpu.VMEM((tk, 2 * tq), BF16),
                        pltpu.VMEM((tk, 2 * tq), BF16)],
    )
    return pl.pallas_call(
        functools.partial(_diff_body, tk=tk, lambda_init=lambda_init),
        grid_spec=grid_spec,
        out_shape=jax.ShapeDtypeStruct((t, DIFF_HEADS * 2 * HEAD_DIM), BF16),
        compiler_params=_cparams(("parallel", "parallel", "arbitrary")),
        name="diff_attn",
    )(lam, hcat, hcat, hcat, subln_g.astype(F32).reshape(LANES, 1))


def _swa_body(sink_ref, q_ref, kv_ref, o_ref, kvt_ref):
    kvh = pl.program_id(1)
    nq = SWA_BLOCK
    nsub = q_ref.shape[0] // nq
    seq = kv_ref.shape[0]
    nblk = seq // SWA_BLOCK
    wb = min(3, nblk)
    nk = wb * SWA_BLOCK

    @pl.when(pl.program_id(2) == 0)
    def _():
        for c in range(nblk):
            kvt_ref[c] = kv_ref[c * SWA_BLOCK:(c + 1) * SWA_BLOCK, :].astype(F32).T.astype(BF16)

    group = SWA_Q_HEADS // SWA_KV_HEADS
    half = LANES // 2

    def swap_halves(a):
        return jnp.concatenate([a[half:], a[:half]], axis=0)

    top = lax.broadcasted_iota(I32, (LANES, nq), 0) < half
    sink = jnp.concatenate([jnp.full((1, nq), sink_ref[kvh * group + g], F32) for g in range(group)], axis=1)

    scores, kvts = [], []
    for sub in range(nsub):
        n = nsub * pl.program_id(2) + sub
        b0 = jnp.clip(n - 1, 0, nblk - wb)
        start = pl.multiple_of(b0 * SWA_BLOCK, SWA_BLOCK)
        cols = []
        for c in range(group // 2):
            qt = q_ref[sub * nq:(sub + 1) * nq, c * LANES:(c + 1) * LANES].astype(F32).T
            cols.append(jnp.where(top, qt, 0.0))
            cols.append(jnp.where(top, swap_halves(qt), 0.0))
        rhs = jnp.concatenate(cols, axis=1).astype(BF16)
        st = jnp.dot(kv_ref[pl.ds(start, nk), :], rhs, preferred_element_type=F32)
        kpos = start + lax.broadcasted_iota(I32, (nk, nq), 0)
        qpos = n * SWA_BLOCK + lax.broadcasted_iota(I32, (nk, nq), 1)
        mask = jnp.where(jnp.abs(kpos - qpos) <= SWA_WINDOW, 0.0, NEG_BIG)
        scores.append(st + jnp.concatenate([mask] * group, axis=1))
        kvts.append(jnp.concatenate([kvt_ref[b0 + w] for w in range(wb)], axis=1))
    for sub in range(nsub):
        st = scores[sub]
        m = jnp.maximum(jnp.max(st, axis=0, keepdims=True), sink)
        e = jnp.exp2(st - m)
        den = jnp.sum(e, axis=0, keepdims=True) + jnp.exp2(sink - m)
        ot = jnp.dot(kvts[sub], e.astype(BF16), preferred_element_type=F32) / den
        for c in range(group // 2):
            even = ot[:, 2 * c * nq:(2 * c + 1) * nq]
            odd = ot[:, (2 * c + 1) * nq:(2 * c + 2) * nq]
            blk = jnp.where(top, swap_halves(even), odd)
            o_ref[sub * nq:(sub + 1) * nq, c * LANES:(c + 1) * LANES] = blk.T.astype(o_ref.dtype)


def _swa_attention(hcat, sinks, batch, seq):
    t = hcat.shape[0]
    nb = seq // SWA_BLOCK
    group = SWA_Q_HEADS // SWA_KV_HEADS
    qw = group * HEAD_DIM
    kvoff = SWA_Q_HEADS * HEAD_DIM // LANES
    nsub = min(SWA_SUB, nb)
    nstep = nb // nsub
    grid_spec = pltpu.PrefetchScalarGridSpec(
        num_scalar_prefetch=1,
        grid=(batch, SWA_KV_HEADS, nstep),
        in_specs=[
            pl.BlockSpec((nsub * SWA_BLOCK, qw), lambda b, h, n, s: (b * nstep + n, h)),
            pl.BlockSpec((seq, LANES), lambda b, h, n, s: (b, kvoff + h)),
        ],
        out_specs=pl.BlockSpec((nsub * SWA_BLOCK, qw), lambda b, h, n, s: (b * nstep + n, h)),
        scratch_shapes=[pltpu.VMEM((nb, LANES, SWA_BLOCK), BF16)],
    )
    return pl.pallas_call(
        _swa_body,
        grid_spec=grid_spec,
        out_shape=jax.ShapeDtypeStruct((t, SWA_Q_HEADS * HEAD_DIM), BF16),
        compiler_params=_cparams(("parallel", "parallel", "arbitrary")),
        name="swa_attn",
    )(sinks.astype(F32) * LOG2E, hcat, hcat)


def _outproj_body(*refs, n_in):
    a_refs = refs[:n_in]
    w_ref, x_ref, g_ref, b_ref, whi_ref, wlo_ref, rbias_ref = refs[n_in:n_in + 7]
    o_ref, opk_ref, idx_ref, gate_ref, pos_ref, cnt_ref, base_ref = refs[n_in + 7:]
    acc = None
    off = 0
    for a_ref in a_refs:
        ka = a_ref.shape[1]
        d = jnp.dot(a_ref[...], w_ref[off:off + ka, :], preferred_element_type=F32)
        acc = d if acc is None else acc + d
        off += ka
    y = DN_ALPHA * x_ref[...] + acc
    out = _layer_norm(y, g_ref[...], b_ref[...])
    o_ref[...] = out
    packed = _pack_rows(out)
    for c in range(PACK_SPLIT):
        opk_ref[c] = packed[:, c * LANES:(c + 1) * LANES]
    _route(out, whi_ref, wlo_ref, rbias_ref, idx_ref, gate_ref, pos_ref, cnt_ref, base_ref)


def _outproj_ln_route(acts, w_bf, x2, g, b, router_w, router_bias, tm=512):
    t, d = x2.shape
    tm = min(tm, t)
    wt = router_w.astype(F32).T
    whi = wt.astype(BF16)
    wlo = (wt - whi.astype(F32)).astype(BF16)
    in_specs = [pl.BlockSpec((tm, a.shape[1]), lambda i: (i, 0)) for a in acts]
    in_specs += [pl.BlockSpec(w_bf.shape, lambda i: (0, 0)),
                 pl.BlockSpec((tm, d), lambda i: (i, 0)),
                 pl.BlockSpec((1, d), lambda i: (0, 0)),
                 pl.BlockSpec((1, d), lambda i: (0, 0)),
                 pl.BlockSpec((N_EXPERTS, d), lambda i: (0, 0)),
                 pl.BlockSpec((N_EXPERTS, d), lambda i: (0, 0)),
                 pl.BlockSpec((N_EXPERTS, 1), lambda i: (0, 0))]
    return pl.pallas_call(
        functools.partial(_outproj_body, n_in=len(acts)),
        grid=(t // tm,),
        in_specs=in_specs,
        out_specs=[pl.BlockSpec((tm, d), lambda i: (i, 0)),
                   pl.BlockSpec((PACK_SPLIT, tm, LANES), lambda i: (0, i, 0)),
                   pl.BlockSpec((TOP_K, tm), lambda i: (0, i)),
                   pl.BlockSpec((TOP_K, tm), lambda i: (0, i)),
                   pl.BlockSpec((TOP_K, tm), lambda i: (0, i)),
                   pl.BlockSpec((N_EXPERTS, 1), lambda i: (0, 0))],
        out_shape=[jax.ShapeDtypeStruct((t, d), F32),
                   jax.ShapeDtypeStruct((PACK_SPLIT, t, LANES), I32),
                   jax.ShapeDtypeStruct((TOP_K, t), I32),
                   jax.ShapeDtypeStruct((TOP_K, t), F32),
                   jax.ShapeDtypeStruct((TOP_K, t), I32),
                   jax.ShapeDtypeStruct((N_EXPERTS, 1), F32)],
        scratch_shapes=[pltpu.VMEM((N_EXPERTS, 1), F32)],
        compiler_params=_cparams(("arbitrary",)),
        name="outproj_ln_route",
    )(*acts, w_bf, x2, g.astype(F32).reshape(1, d), b.astype(F32).reshape(1, d),
      whi, wlo, router_bias.astype(F32).reshape(N_EXPERTS, 1))


def _route(h, whi_ref, wlo_ref, bias_ref, idx_ref, gate_ref, pos_ref, cnt_ref, base_ref):
    i = pl.program_id(0)
    tm = h.shape[0]
    gsz = N_EXPERTS // N_GROUPS

    @pl.when(i == 0)
    def _():
        base_ref[...] = jnp.zeros_like(base_ref)

    h_hi = h.astype(BF16)
    h_lo = (h - h_hi.astype(F32)).astype(BF16)
    dn = (((1,), (1,)), ((), ()))
    whi = whi_ref[...]
    logits = (lax.dot_general(whi, h_hi, dn, preferred_element_type=F32)
              + lax.dot_general(whi, h_lo, dn, preferred_element_type=F32)
              + lax.dot_general(wlo_ref[...], h_hi, dn, preferred_element_type=F32))
    scores = 1.0 / (1.0 + jnp.exp(-logits))
    choice = scores + bias_ref[...]

    iota_g = lax.broadcasted_iota(I32, (gsz, tm), 0)
    gscore = []
    for g in range(N_GROUPS):
        cg = choice[g * gsz:(g + 1) * gsz, :]
        m1 = jnp.max(cg, axis=0, keepdims=True)
        first = jnp.min(jnp.where(cg == m1, iota_g, gsz), axis=0, keepdims=True)
        m2 = jnp.max(jnp.where(iota_g == first, -jnp.inf, cg), axis=0, keepdims=True)
        gscore.append(m1 + m2)
    pieces = []
    for g in range(N_GROUPS):
        rank = jnp.zeros((1, tm), I32)
        for o in range(N_GROUPS):
            if o == g:
                continue
            beats = (gscore[o] > gscore[g]) if o > g else (gscore[o] >= gscore[g])
            rank = rank + beats.astype(I32)
        keep = rank < TOPK_GROUPS
        pieces.append(jnp.where(keep, choice[g * gsz:(g + 1) * gsz, :], -jnp.inf))
    masked = jnp.concatenate(pieces, axis=0)

    iota_e = lax.broadcasted_iota(I32, (N_EXPERTS, tm), 0)
    sel_all = jnp.zeros((N_EXPERTS, tm), F32)
    idxs, gates, sels = [], [], []
    for _ in range(TOP_K):
        mx = jnp.max(masked, axis=0, keepdims=True)
        idx = jnp.min(jnp.where(masked == mx, iota_e, N_EXPERTS), axis=0, keepdims=True)
        sel = iota_e == idx
        gates.append(jnp.sum(jnp.where(sel, scores, 0.0), axis=0, keepdims=True))
        masked = jnp.where(sel, -jnp.inf, masked)
        sel_all = sel_all + sel.astype(F32)
        idxs.append(idx)
        sels.append(sel)
    gsum = gates[0]
    for gk in gates[1:]:
        gsum = gsum + gk
    gate_ref[...] = jnp.concatenate(gates, axis=0) / gsum * ROUTED_SCALE
    idx_ref[...] = jnp.concatenate(idxs, axis=0)

    tri = (lax.broadcasted_iota(I32, (tm, tm), 0) < lax.broadcasted_iota(I32, (tm, tm), 1))
    cum = jnp.dot(sel_all.astype(BF16), tri.astype(F32).astype(BF16), preferred_element_type=F32)
    tot = cum + base_ref[...]
    pos = [jnp.sum(jnp.where(sel, tot, 0.0), axis=0, keepdims=True) for sel in sels]
    pos_ref[...] = jnp.concatenate(pos, axis=0).astype(I32)
    base_ref[...] = base_ref[...] + jnp.sum(sel_all, axis=1, keepdims=True)
    cnt_ref[...] = base_ref[...]


def _expert_body(be_ref, na_ref, bv_ref, ord_ref, nxt_ref, x_ref, wg_hbm, wu_hbm, wd_hbm, y_ref,
                 wgu, wdb, wg_buf, wu_buf, wd_buf, sem):
    i = pl.program_id(0)
    bm = x_ref.shape[1]
    ff = wd_hbm.shape[1]
    hm = bm // EXPERT_CHAINS

    def weight_copies(e, slot):
        return [pltpu.make_async_copy(hbm.at[e], buf.at[slot], sem.at[slot])
                for hbm, buf in ((wg_hbm, wg_buf), (wu_hbm, wu_buf), (wd_hbm, wd_buf))]

    @pl.when(i < na_ref[0])
    def _():
        prev = be_ref[jnp.maximum(i - 1, 0)]
        slot = ord_ref[i] % 2

        @pl.when(i == 0)
        def _():
            for cp in weight_copies(be_ref[0], 0):
                cp.start()

        @pl.when((i == 0) | (be_ref[i] != prev))
        def _():
            for cp in weight_copies(be_ref[i], slot):
                cp.wait()
            wgu[:, :ff] = wg_buf[slot].astype(BF16)
            wgu[:, ff:] = wu_buf[slot].astype(BF16)
            wdb[...] = wd_buf[slot].astype(BF16)

            @pl.when(nxt_ref[i] >= 0)
            def _():
                for cp in weight_copies(nxt_ref[i], 1 - slot):
                    cp.start()

        valid = bv_ref[i]
        row = lax.broadcasted_iota(I32, (hm, LANES), 0)
        def run(n_chains):
            gus = []
            for hb in range(n_chains):
                rows = slice(hb * hm, (hb + 1) * hm)
                keep = row < valid - hb * hm
                x = _unpack_rows(lambda c: jnp.where(keep, x_ref[c, rows, :], 0)).astype(BF16)
                gus.append(jnp.dot(x, wgu[...], preferred_element_type=F32))
            ys = []
            for gu in gus:
                hmid = (_silu(gu[:, :ff]) * gu[:, ff:]).astype(BF16)
                ys.append(jnp.dot(hmid, wdb[...], preferred_element_type=F32))
            for hb, y in enumerate(ys):
                packed = _pack_rows(y)
                for c in range(PACK_SPLIT):
                    y_ref[c, hb * hm:(hb + 1) * hm, :] = packed[:, c * LANES:(c + 1) * LANES]
            if n_chains < EXPERT_CHAINS:
                y_ref[:, n_chains * hm:, :] = jnp.zeros((PACK_SPLIT, bm - n_chains * hm, LANES), I32)

        for n_chains in range(1, EXPERT_CHAINS + 1):
            lo = (n_chains - 1) * hm
            cond = valid > lo if n_chains == EXPERT_CHAINS else (valid > lo) & (valid <= lo + hm)
            pl.when(cond)(functools.partial(run, n_chains))


def _expert_matmul(xs, blk_expert, n_active, blk_valid, w_gate, w_up, w_down):
    n_slots = xs.shape[1]
    bm = EXPERT_BLOCK
    n_blocks = n_slots // bm
    d, ff = w_gate.shape[1], w_gate.shape[2]

    blk = jnp.arange(n_blocks, dtype=I32)
    change = jnp.concatenate([jnp.zeros((1,), I32), (blk_expert[1:] != blk_expert[:-1]).astype(I32)])
    ordinal = jnp.sum(jnp.where(blk[None, :] <= blk[:, None], change[None, :], 0), axis=1).astype(I32)
    later = (ordinal[None, :] == ordinal[:, None] + 1) & (blk[None, :] < n_active[0])
    nxt = jnp.max(jnp.where(later, blk_expert[None, :], -1), axis=1).astype(I32)

    def row_map(i, be, na, bv, od, nx):
        return (0, jnp.minimum(i, na[0] - 1), 0)

    grid_spec = pltpu.PrefetchScalarGridSpec(
        num_scalar_prefetch=5,
        grid=(n_blocks,),
        in_specs=[pl.BlockSpec((PACK_SPLIT, bm, LANES), row_map),
                  pl.BlockSpec(memory_space=pl.ANY),
                  pl.BlockSpec(memory_space=pl.ANY),
                  pl.BlockSpec(memory_space=pl.ANY)],
        out_specs=pl.BlockSpec((PACK_SPLIT, bm, LANES), row_map),
        scratch_shapes=[pltpu.VMEM((d, 2 * ff), BF16), pltpu.VMEM((ff, d), BF16),
                        pltpu.VMEM((2, d, ff), F32), pltpu.VMEM((2, d, ff), F32), pltpu.VMEM((2, ff, d), F32),
                        pltpu.SemaphoreType.DMA((2,))],
    )
    return pl.pallas_call(
        _expert_body,
        grid_spec=grid_spec,
        out_shape=jax.ShapeDtypeStruct((PACK_SPLIT, n_slots, LANES), I32),
        compiler_params=_cparams(("arbitrary",)),
        name="expert_mlp",
    )(blk_expert, n_active, blk_valid, ordinal, nxt, xs, w_gate, w_up, w_down)


def _slot_body(idx_ref, pos_ref, ps_ref, slot_ref, *, n_slots):
    tm = idx_ref.shape[1]
    iota_e = lax.broadcasted_iota(I32, (N_EXPERTS, tm), 0)
    ps = ps_ref[...]
    rows = []
    for k in range(TOP_K):
        start = jnp.sum(jnp.where(iota_e == idx_ref[k:k + 1, :], ps, 0.0), axis=0, keepdims=True)
        rows.append(start.astype(I32) + pos_ref[k:k + 1, :])
    slot = jnp.concatenate(rows, axis=0)
    for c in range(PACK_SPLIT):
        slot_ref[c * TOP_K:(c + 1) * TOP_K, :] = slot + c * n_slots


def _slots(idx, pos, pad_start, n_slots, tm=512):
    t = idx.shape[1]
    tm = min(tm, t)
    return pl.pallas_call(
        functools.partial(_slot_body, n_slots=n_slots),
        grid=(t // tm,),
        in_specs=[pl.BlockSpec((TOP_K, tm), lambda i: (0, i)),
                  pl.BlockSpec((TOP_K, tm), lambda i: (0, i)),
                  pl.BlockSpec((N_EXPERTS, 1), lambda i: (0, 0))],
        out_specs=pl.BlockSpec((PACK_SPLIT * TOP_K, tm), lambda i: (0, i)),
        out_shape=jax.ShapeDtypeStruct((PACK_SPLIT * TOP_K, t), I32),
        compiler_params=_cparams(("parallel",)),
        name="slots",
    )(idx, pos, pad_start.astype(F32).reshape(N_EXPERTS, 1))


def _sc_mesh():
    return plsc.VectorSubcoreMesh(core_axis_name="core", subcore_axis_name="subcore",
                                  num_cores=SC_CORES, num_subcores=SC_SUBCORES)


def _sc_scatter_rows(rows, idx, n_out):
    t = idx.shape[1]
    nj = t // SC_WINDOW

    @functools.partial(pl.kernel, out_type=jax.ShapeDtypeStruct((n_out, LANES), I32),
                       mesh=_sc_mesh(), scratch_types=[], name="sc_dispatch")
    def run(rows_hbm, idx_hbm, out_hbm):
        def body(rows_vmem, idx_vmem):
            for k in range(TOP_K):
                pltpu.sync_copy(rows_vmem, out_hbm.at[idx_vmem.at[k]])

        pltpu.emit_pipeline(
            body,
            grid=(rows.shape[0] // SC_WINDOW,),
            in_specs=[pl.BlockSpec((SC_WINDOW, LANES), lambda s: (s, 0)),
                      pl.BlockSpec((TOP_K, SC_WINDOW), lambda s: (s // nj, s % nj))],
            out_specs=[],
            core_axis_name=("core", "subcore"),
            dimension_semantics=(pltpu.PARALLEL,),
        )(rows_hbm, idx_hbm)

    return run(rows, idx)


def _sc_gather_rows(src, idx):
    nr, t = idx.shape
    nj = t // SC_WINDOW

    @functools.partial(pl.kernel, out_type=jax.ShapeDtypeStruct((nr * t, LANES), I32),
                       mesh=_sc_mesh(), scratch_types=[], name="sc_combine")
    def run(src_hbm, idx_hbm, out_hbm):
        def body(idx_vmem, out_vmem):
            pltpu.sync_copy(src_hbm.at[idx_vmem.at[0]], out_vmem)

        pltpu.emit_pipeline(
            body,
            grid=(nr * nj,),
            in_specs=[pl.BlockSpec((1, SC_WINDOW), lambda s: (s // nj, s % nj))],
            out_specs=[pl.BlockSpec((SC_WINDOW, LANES), lambda s: (s, 0))],
            core_axis_name=("core", "subcore"),
            dimension_semantics=(pltpu.PARALLEL,),
        )(idx_hbm, out_hbm)

    return run(src, idx)


def _combine_body(h_ref, yg_ref, gt_ref, sg_ref, su_ref, sd_ref, g_ref, b_ref, *rest, proj, n_alias):
    n_out = 1 if proj is None else 2
    rest = rest[:len(rest) - n_out - n_alias] + rest[len(rest) - n_out:]
    h = h_ref[...]
    hb = h.astype(BF16)
    a = jnp.dot(hb, sg_ref[...], preferred_element_type=F32)
    u = jnp.dot(hb, su_ref[...], preferred_element_type=F32)
    moe = jnp.dot((_silu(a) * u).astype(BF16), sd_ref[...], preferred_element_type=F32)
    gt = gt_ref[...]
    for k in range(TOP_K):
        moe = moe + gt[:, k:k + 1] * _unpack_rows(lambda c, k=k: yg_ref[c, k])
    out = _layer_norm(DN_ALPHA * h + moe, g_ref[...], b_ref[...])
    if proj is None:
        rest[0][...] = out
    else:
        w_ref, tab_ref, o_ref, hcat_ref = rest
        o_ref[...] = out
        _project_rope(out.astype(BF16), w_ref, tab_ref, hcat_ref, *proj)


def _combine_ln(h, yg, gates_t, sh_gate, sh_up, sh_down, g, b, row0, prev, next_proj=None, tm=256):
    t, d = h.shape
    tc = yg.shape[2]
    tm = min(tm, tc) if next_proj is None else min(tm, tc, next_proj[3])
    off = row0 // tm
    ff = sh_gate.shape[1]
    in_specs = [pl.BlockSpec((tm, d), lambda i: (i + off, 0)),
                pl.BlockSpec((PACK_SPLIT, TOP_K, tm, LANES), lambda i: (0, 0, i, 0)),
                pl.BlockSpec((tm, TOP_K), lambda i: (i, 0)),
                pl.BlockSpec((d, ff), lambda i: (0, 0)),
                pl.BlockSpec((d, ff), lambda i: (0, 0)),
                pl.BlockSpec((ff, d), lambda i: (0, 0)),
                pl.BlockSpec((1, d), lambda i: (0, 0)),
                pl.BlockSpec((1, d), lambda i: (0, 0))]
    out_specs = [pl.BlockSpec((tm, d), lambda i: (i + off, 0))]
    out_shape = [jax.ShapeDtypeStruct((t, d), F32)]
    operands = [h, yg, gates_t, sh_gate.astype(BF16), sh_up.astype(BF16), sh_down.astype(BF16),
                g.astype(F32).reshape(1, d), b.astype(F32).reshape(1, d)]
    proj = None
    if next_proj is not None:
        w_bf, tab, rope_kind, seq = next_proj
        sb = seq // tm
        n_out = w_bf.shape[1]
        in_specs += [pl.BlockSpec((d, n_out), lambda i: (0, 0)),
                     pl.BlockSpec((tm, tab.shape[1]), lambda i: ((i + off) % sb, 0))]
        out_specs.append(pl.BlockSpec((tm, n_out), lambda i: (i + off, 0)))
        out_shape.append(jax.ShapeDtypeStruct((t, n_out), BF16))
        operands += [w_bf, tab]
        proj = (PROJ_CHUNK, tuple(rope_kind))
    aliases = {}
    if prev is not None:
        for k, p in enumerate(prev):
            aliases[len(operands)] = k
            in_specs.append(pl.BlockSpec(memory_space=pl.ANY))
            operands.append(p)
    return pl.pallas_call(
        functools.partial(_combine_body, proj=proj, n_alias=len(aliases)),
        grid=(tc // tm,),
        in_specs=in_specs,
        out_specs=out_specs,
        out_shape=out_shape,
        input_output_aliases=aliases,
        compiler_params=_cparams(("parallel",)),
        name="combine_ln",
    )(*operands)


def _mixer_out_and_moe(acts, w_out, x2, ln1_g, ln1_b, router_w, router_bias, w_gate, w_up, w_down,
                       sh_gate, sh_up, sh_down, g, b, next_proj=None):
    h, h_pk, idx, gates, pos, cnt = _outproj_ln_route(acts, w_out.astype(BF16), x2, ln1_g, ln1_b,
                                                      router_w, router_bias)
    t, d = h.shape
    bm = EXPERT_BLOCK
    counts = cnt[:, 0].astype(I32)
    padded = (counts + bm - 1) // bm * bm
    e_iota = jnp.arange(N_EXPERTS, dtype=I32)
    pad_end = jnp.sum(jnp.where(e_iota[None, :] <= e_iota[:, None], padded[None, :], 0), axis=1)
    pad_start = pad_end - padded
    n_blocks = t * TOP_K // bm + N_EXPERTS
    n_slots = n_blocks * bm
    blk_start = jnp.arange(n_blocks, dtype=I32) * bm
    blk_expert = jnp.minimum(
        jnp.sum((blk_start[:, None] >= pad_end[None, :]).astype(I32), axis=1), N_EXPERTS - 1)
    real_end = jnp.sum(jnp.where(e_iota[None, :] == blk_expert[:, None], (pad_start + counts)[None, :], 0), axis=1)
    blk_valid = jnp.clip(real_end - blk_start, 0, bm).astype(I32)
    n_active = (pad_end[-1:] // bm).astype(I32)

    slot4 = _slots(idx, pos, pad_start, n_slots)
    xs = _sc_scatter_rows(h_pk.reshape(PACK_SPLIT * t, LANES), slot4, PACK_SPLIT * n_slots)
    ys = _expert_matmul(xs.reshape(PACK_SPLIT, n_slots, LANES), blk_expert, n_active, blk_valid,
                        w_gate, w_up, w_down)
    ys_flat = ys.reshape(PACK_SPLIT * n_slots, LANES)
    gates_t = gates.T
    tc = t // COMBINE_CHUNKS
    outs = None
    for c in range(COMBINE_CHUNKS):
        rows = slice(c * tc, (c + 1) * tc)
        yg = _sc_gather_rows(ys_flat, slot4[:, rows])
        outs = _combine_ln(h, yg.reshape(PACK_SPLIT, TOP_K, tc, LANES), gates_t[rows], sh_gate, sh_up, sh_down,
                           g, b, c * tc, outs, next_proj=next_proj)
    return outs if next_proj is not None else outs[0]


def kernel(x, l0_w_in, l0_w_out, l0_na_rpb, l0_diff_lq1, l0_diff_lk1, l0_diff_lq2, l0_diff_lk2, l0_diff_subln_g, l0_ln1_g, l0_ln1_b, l0_router_w, l0_router_bias, l0_expert_w_gate, l0_expert_w_up, l0_expert_w_down, l0_shared_w_gate, l0_shared_w_up, l0_shared_w_down, l0_ln2_g, l0_ln2_b, l1_w_in, l1_w_out, l1_swa_sinks, l1_ln1_g, l1_ln1_b, l1_router_w, l1_router_bias, l1_expert_w_gate, l1_expert_w_up, l1_expert_w_down, l1_shared_w_gate, l1_shared_w_up, l1_shared_w_down, l1_ln2_g, l1_ln2_b):
    batch, seq, d = x.shape
    t = batch * seq
    x2 = x.reshape(t, d).astype(F32)
    tab = _rope_table(seq)
    qscale = HEAD_DIM ** -0.5 * LOG2E
    na_w = NA_HEADS * HEAD_DIM
    dq_w = DIFF_HEADS * 2 * HEAD_DIM

    col_scale = jnp.concatenate([
        jnp.full((na_w,), qscale, F32), jnp.ones((2 * na_w,), F32),
        jnp.full((dq_w,), qscale, F32), jnp.ones((2 * dq_w,), F32)])
    w_in0 = (l0_w_in.astype(F32) * col_scale).astype(BF16)
    per = LANES
    rope0 = [0] * (3 * na_w // per) + [1] * (2 * dq_w // per) + [0] * (dq_w // per)
    hcat0 = _inproj(x2, w_in0, tab, rope0, seq)
    oa = _na_attention(hcat0, l0_na_rpb, batch, seq)
    lambda_init = 0.8 - 0.6 * math.exp(-0.3 * 0)
    lam = (jnp.exp(jnp.sum(l0_diff_lq1.astype(F32) * l0_diff_lk1.astype(F32)))
           - jnp.exp(jnp.sum(l0_diff_lq2.astype(F32) * l0_diff_lk2.astype(F32))) + lambda_init)
    od = _diff_attention(hcat0, lam.reshape(1).astype(F32), l0_diff_subln_g, lambda_init, batch, seq)
    q_w = SWA_Q_HEADS * HEAD_DIM
    kv_w = SWA_KV_HEADS * HEAD_DIM
    w1 = l1_w_in.astype(F32)
    wq = w1[:, :q_w] * qscale
    wk = w1[:, q_w:q_w + kv_w].reshape(d, SWA_KV_HEADS, HEAD_DIM)
    wv = w1[:, q_w + kv_w:].reshape(d, SWA_KV_HEADS, HEAD_DIM)
    wkv = jnp.concatenate([wk, wv], axis=-1).reshape(d, 2 * kv_w)
    w_in1 = jnp.concatenate([wq, wkv], axis=1).astype(BF16)
    rope1 = [1] * (q_w // per) + [2] * (2 * kv_w // per)

    x2, hcat1 = _mixer_out_and_moe(
        [oa, od], l0_w_out, x2, l0_ln1_g, l0_ln1_b, l0_router_w, l0_router_bias,
        l0_expert_w_gate, l0_expert_w_up, l0_expert_w_down,
        l0_shared_w_gate, l0_shared_w_up, l0_shared_w_down, l0_ln2_g, l0_ln2_b,
        next_proj=(w_in1, tab, rope1, seq))

    o1 = _swa_attention(hcat1, l1_swa_sinks, batch, seq)
    x2 = _mixer_out_and_moe(
        [o1], l1_w_out, x2, l1_ln1_g, l1_ln1_b, l1_router_w, l1_router_bias,
        l1_expert_w_gate, l1_expert_w_up, l1_expert_w_down,
        l1_shared_w_gate, l1_shared_w_up, l1_shared_w_down, l1_ln2_g, l1_ln2_b)
    return x2.reshape(batch, seq, d).astype(x.dtype)
```

```python
import functools
import math

import numpy as np
import jax
import jax.numpy as jnp
from jax import lax
from jax.experimental import pallas as pl
from jax.experimental.pallas import tpu as pltpu
from jax.experimental.pallas import tpu_sc as plsc

F32 = jnp.float32
BF16 = jnp.bfloat16
I32 = jnp.int32

HEAD_DIM = 64
GRID_W = 64
NA_HEADS = 8
NA_WIN_ROWS = 8
NA_WIN_COLS = 16
DIFF_HEADS = 4
SWA_Q_HEADS = 16
SWA_KV_HEADS = 4
SWA_WINDOW = 128
SWA_BLOCK = 128
ROPE_THETA = 10000.0
N_EXPERTS = 256
TOP_K = 8
N_GROUPS = 8
TOPK_GROUPS = 4
ROUTED_SCALE = 2.5
LN_EPS = 1e-5
DEPTH = 2
DN_ALPHA = (2 * DEPTH) ** 0.25

LOG2E = 1.4426950408889634
NEG_BIG = -3.0e38
LANES = 128
NA_KEY_ROWS = NA_WIN_ROWS + 1
PROJ_CHUNK = 256
SWA_SUB = 8
NA_SUB = 8
EXPERT_BLOCK = 512
COMBINE_CHUNKS = 4
EXPERT_CHAINS = 2
DIFF_ONES_ROWS = 16
PACK_SPLIT = 4
SC_CORES = 2
SC_SUBCORES = 16
SC_WINDOW = 128
VMEM_LIMIT = 56 * 1024 * 1024


def _cparams(sem):
    return pltpu.CompilerParams(dimension_semantics=sem, vmem_limit_bytes=VMEM_LIMIT)


def _silu(x):
    return x / (1.0 + jnp.exp(-x))


def _pack_rows(y):
    half = y.shape[1] // 2
    bits = pltpu.bitcast(y.astype(BF16).astype(F32), I32)
    lo = lax.shift_right_logical(bits[:, :half], 16)
    hi = bits[:, half:] & jnp.int32(-65536)
    return hi | lo


def _unpack_words(w):
    lo = pltpu.bitcast(lax.shift_left(w, 16), F32)
    hi = pltpu.bitcast(w & jnp.int32(-65536), F32)
    return lo, hi


def _unpack_rows(ref_at):
    los, his = [], []
    for c in range(PACK_SPLIT):
        lo, hi = _unpack_words(ref_at(c))
        los.append(lo)
        his.append(hi)
    return jnp.concatenate(los + his, axis=1)


def _layer_norm(y, g, b):
    mu = jnp.mean(y, axis=-1, keepdims=True)
    yc = y - mu
    var = jnp.mean(yc * yc, axis=-1, keepdims=True)
    return yc * lax.rsqrt(var + LN_EPS) * g + b


def _inproj_body(x_ref, w_ref, tab_ref, o_ref, *, chunk, rope_kind):
    _project_rope(x_ref[...].astype(BF16), w_ref, tab_ref, o_ref, chunk, rope_kind)


def _project_rope(x, w_ref, tab_ref, o_ref, chunk, rope_kind):
    tm = x.shape[0]
    n_out = w_ref.shape[1]
    lane = lax.broadcasted_iota(I32, (tm, LANES), 1)
    first_half = (lane % HEAD_DIM) < (HEAD_DIM // 2)
    per = chunk // LANES
    for c in range(n_out // chunk):
        h = jnp.dot(x, w_ref[:, c * chunk:(c + 1) * chunk], preferred_element_type=F32)
        kinds = rope_kind[c * per:(c + 1) * per]
        if any(kinds):
            pieces = []
            for j, kind in enumerate(kinds):
                t = h[:, j * LANES:(j + 1) * LANES]
                if kind:
                    base = (kind - 1) * 2 * LANES
                    cos = tab_ref[:, base:base + LANES]
                    sin = tab_ref[:, base + LANES:base + 2 * LANES]
                    rot = jnp.where(first_half, pltpu.roll(t, LANES - HEAD_DIM // 2, 1),
                                    pltpu.roll(t, HEAD_DIM // 2, 1))
                    t = t * cos + rot * sin
                pieces.append(t)
            h = jnp.concatenate(pieces, axis=1)
        o_ref[:, c * chunk:(c + 1) * chunk] = h.astype(o_ref.dtype)


def _inproj(x2, w_bf, tab, rope_kind, seq, tm=512):
    t, d = x2.shape
    n_out = w_bf.shape[1]
    tm = min(tm, seq)
    sb = seq // tm
    return pl.pallas_call(
        functools.partial(_inproj_body, chunk=PROJ_CHUNK, rope_kind=tuple(rope_kind)),
        grid=(t // tm,),
        in_specs=[pl.BlockSpec((tm, d), lambda i: (i, 0)),
                  pl.BlockSpec((d, n_out), lambda i: (0, 0)),
                  pl.BlockSpec((tm, tab.shape[1]), lambda i: (i % sb, 0))],
        out_specs=pl.BlockSpec((tm, n_out), lambda i: (i, 0)),
        out_shape=jax.ShapeDtypeStruct((t, n_out), BF16),
        compiler_params=_cparams(("parallel",)),
        name="inproj",
    )(x2, w_bf, tab)


def _rope_table(seq):
    inv = 1.0 / (ROPE_THETA ** (jnp.arange(0, HEAD_DIM, 2, dtype=F32) / HEAD_DIM))
    ang = jnp.arange(seq, dtype=F32)[:, None] * inv[None, :]
    cos = jnp.cos(ang)
    sin = jnp.sin(ang)
    cos64 = jnp.concatenate([cos, cos], axis=-1)
    sin64 = jnp.concatenate([-sin, sin], axis=-1)
    one = jnp.ones((seq, HEAD_DIM), F32)
    zero = jnp.zeros((seq, HEAD_DIM), F32)
    return jnp.concatenate([cos64, cos64, sin64, sin64, cos64, one, sin64, zero], axis=-1)


def _na_body(pat_ref, q_ref, k_ref, v_ref, *rest, rows_n):
    del pat_ref
    bias_refs, o_ref = rest[:NA_SUB], rest[NA_SUB]
    nq = q_ref.shape[0] // NA_SUB
    nk = NA_KEY_ROWS * GRID_W
    low = lax.broadcasted_iota(I32, (nq, LANES), 1) < HEAD_DIM
    scores, vwins = [], []
    for sub in range(NA_SUB):
        r = NA_SUB * pl.program_id(2) + sub
        ks = jnp.clip(2 * r - NA_WIN_ROWS // 2, 0, rows_n - NA_KEY_ROWS)
        start = pl.multiple_of(ks * GRID_W, GRID_W)
        q = q_ref[sub * nq:(sub + 1) * nq, :].astype(F32)
        qm = jnp.concatenate([jnp.where(low, q, 0.0), jnp.where(low, 0.0, q)], axis=0).astype(BF16)
        s = lax.dot_general(qm, k_ref[pl.ds(start, nk), :], (((1,), (1,)), ((), ())),
                            preferred_element_type=F32)
        scores.append(s + bias_refs[sub][0].reshape(2 * nq, nk))
        vwins.append(v_ref[pl.ds(start, nk), :])
    for sub in range(NA_SUB):
        s = scores[sub]
        m = jnp.max(s, axis=-1, keepdims=True)
        p = jnp.exp2(s - m)
        l = jnp.sum(p, axis=-1, keepdims=True)
        pv = jnp.dot(p.astype(BF16), vwins[sub], preferred_element_type=F32) / l
        o_ref[sub * nq:(sub + 1) * nq, :] = jnp.where(low, pv[:nq], pv[nq:]).astype(o_ref.dtype)


def _na_bias(rpb, rows_n):
    half = NA_WIN_ROWS // 2
    pats, pat_id = [], []
    for blk in range(rows_n // 2):
        r0 = 2 * blk
        ks = min(max(r0 - half, 0), rows_n - NA_KEY_ROWS)
        starts = tuple(min(max(r0 + i - half, 0), rows_n - NA_WIN_ROWS) - ks for i in range(2))
        key = (r0 - ks, starts)
        if key not in pats:
            pats.append(key)
        pat_id.append(pats.index(key))
    wq = np.arange(GRID_W)
    wk = np.arange(GRID_W)
    col_start = np.clip(wq - NA_WIN_COLS // 2, 0, GRID_W - NA_WIN_COLS)
    col_off = wk[None, :] - col_start[:, None]
    col_valid = (col_off >= 0) & (col_off < NA_WIN_COLS)
    dc = np.clip(wk[None, :] - wq[:, None], -(NA_WIN_COLS - 1), NA_WIN_COLS - 1) + (NA_WIN_COLS - 1)
    n_dc = 2 * NA_WIN_COLS - 1
    onehot = (dc.reshape(-1)[None, :] == np.arange(n_dc)[:, None]).astype(np.float32)
    heads, n_dr = rpb.shape[0], rpb.shape[1]
    col = jnp.dot(rpb.astype(F32).reshape(heads * n_dr, n_dc), jnp.asarray(onehot),
                  precision=lax.Precision.HIGHEST).reshape(heads, n_dr, GRID_W, GRID_W)
    col = jnp.where(col_valid[None, None], col * LOG2E, NEG_BIG)
    masked_blk = jnp.full((heads, GRID_W, GRID_W), NEG_BIG, F32)
    tables = []
    for r0rel, starts in pats:
        qrows = []
        for qi in range(2):
            blks = []
            for kr in range(NA_KEY_ROWS):
                row_ok = starts[qi] <= kr < starts[qi] + NA_WIN_ROWS
                dr = kr - (r0rel + qi) + (NA_WIN_ROWS - 1)
                blks.append(col[:, dr] if row_ok else masked_blk)
            qrows.append(jnp.concatenate(blks, axis=-1))
        tables.append(jnp.concatenate(qrows, axis=1))
    return jnp.stack(tables, axis=0), np.asarray(pat_id, np.int32)


def _na_attention(hcat, rpb, batch, seq):
    t = hcat.shape[0]
    rows_n = seq // GRID_W
    nblk = rows_n // 2
    nq = 2 * GRID_W
    bias, pat_id = _na_bias(rpb, rows_n)
    nk = NA_KEY_ROWS * GRID_W
    hp = NA_HEADS // 2
    nstep = nblk // NA_SUB

    def bias_spec(sub):
        return pl.BlockSpec((1, 2, nq, nk), lambda b, h, r, pat: (pat[NA_SUB * r + sub], h, 0, 0))

    grid_spec = pltpu.PrefetchScalarGridSpec(
        num_scalar_prefetch=1,
        grid=(batch, hp, nstep),
        in_specs=[
            pl.BlockSpec((NA_SUB * nq, LANES), lambda b, h, r, pat: (b * nstep + r, h)),
            pl.BlockSpec((seq, LANES), lambda b, h, r, pat: (b, hp + h)),
            pl.BlockSpec((seq, LANES), lambda b, h, r, pat: (b, 2 * hp + h)),
        ] + [bias_spec(sub) for sub in range(NA_SUB)],
        out_specs=pl.BlockSpec((NA_SUB * nq, LANES), lambda b, h, r, pat: (b * nstep + r, h)),
    )
    return pl.pallas_call(
        functools.partial(_na_body, rows_n=rows_n),
        grid_spec=grid_spec,
        out_shape=jax.ShapeDtypeStruct((t, NA_HEADS * HEAD_DIM), BF16),
        compiler_params=_cparams(("parallel", "parallel", "arbitrary")),
        name="na_attn",
    )(jnp.asarray(pat_id), hcat, hcat, hcat, *([bias] * NA_SUB))


def _diff_body(lam_ref, q_ref, k_ref, v_ref, g_ref, o_ref, vt_ref, acc_ref, sta_ref, stb_ref, pa_ref, pb_ref,
               *, tk, lambda_init):
    j = pl.program_id(2)
    tq = q_ref.shape[0]
    seq = k_ref.shape[0]
    nkv = seq // tk

    @pl.when(j == 0)
    def _():
        ones = jnp.ones((DIFF_ONES_ROWS, tk), BF16)
        for c in range(nkv):
            vt_ref[c, :LANES, :] = v_ref[c * tk:(c + 1) * tk, :].astype(F32).T.astype(BF16)
            vt_ref[c, LANES:, :] = ones

    q = q_ref[...].astype(F32)
    lane = lax.broadcasted_iota(I32, (tq, LANES), 1)
    q1 = jnp.where(lane < HEAD_DIM, q, 0.0).T
    q2 = jnp.where(lane >= HEAD_DIM, q, 0.0).T
    rhs = jnp.concatenate([q1, q2], axis=1).astype(BF16)

    def scores(i, dst_ref):
        start = pl.multiple_of(i * tk, tk)
        dst_ref[...] = jnp.dot(k_ref[pl.ds(start, tk), :], rhs, preferred_element_type=F32)

    def softmax(src_ref, dst_ref, m):
        st = src_ref[...]
        m_new = jnp.maximum(m, jnp.max(st, axis=0, keepdims=True))
        dst_ref[...] = jnp.exp2((st - m_new).astype(BF16))
        return m_new, jnp.exp2(m - m_new)

    def weighted_values(i, p_ref, alpha):
        pv = jnp.dot(vt_ref[i], p_ref[...], preferred_element_type=F32)
        acc_ref[...] = alpha * acc_ref[...] + pv

    acc_ref[...] = jnp.zeros_like(acc_ref)
    scores(0, sta_ref)
    scores(1, stb_ref)
    m, a0 = softmax(sta_ref, pa_ref, jnp.full((1, 2 * tq), NEG_BIG, F32))

    def pair(ii, carry):
        m, a0 = carry
        i = 2 * ii
        scores(i + 2, sta_ref)
        m, a1 = softmax(stb_ref, pb_ref, m)
        weighted_values(i, pa_ref, a0)
        scores(i + 3, stb_ref)
        m, a0 = softmax(sta_ref, pa_ref, m)
        weighted_values(i + 1, pb_ref, a1)
        return m, a0

    m, a0 = lax.fori_loop(0, (nkv - 2) // 2, pair, (m, a0))
    m, a1 = softmax(stb_ref, pb_ref, m)
    weighted_values(nkv - 2, pa_ref, a0)
    weighted_values(nkv - 1, pb_ref, a1)
    acc = acc_ref[...]
    ot = acc[:LANES, :] / acc[LANES:LANES + 1, :]
    dt = ot[:, :tq] - lam_ref[0] * ot[:, tq:]
    ms = jnp.mean(dt * dt, axis=0, keepdims=True)
    y = dt * lax.rsqrt(ms + LN_EPS) * g_ref[...] * (1.0 - lambda_init)
    o_ref[...] = y.T.astype(o_ref.dtype)


def _diff_attention(hcat, lam, subln_g, lambda_init, batch, seq, tq=2048, tk=512):
    t = hcat.shape[0]
    tq = min(tq, seq)
    tk = min(tk, seq)
    nq = seq // tq
    qoff = 3 * NA_HEADS * HEAD_DIM // LANES
    koff = qoff + DIFF_HEADS
    voff = koff + DIFF_HEADS
    grid_spec = pltpu.PrefetchScalarGridSpec(
        num_scalar_prefetch=1,
        grid=(batch, DIFF_HEADS, nq),
        in_specs=[
            pl.BlockSpec((tq, LANES), lambda b, h, j, lam: (b * nq + j, qoff + h)),
            pl.BlockSpec((seq, LANES), lambda b, h, j, lam: (b, koff + h)),
            pl.BlockSpec((seq, LANES), lambda b, h, j, lam: (b, voff + h)),
            pl.BlockSpec((LANES, 1), lambda b, h, j, lam: (0, 0)),
        ],
        out_specs=pl.BlockSpec((tq, LANES), lambda b, h, j, lam: (b * nq + j, h)),
        scratch_shapes=[pltpu.VMEM((seq // tk, LANES + DIFF_ONES_ROWS, tk), BF16),
                        pltpu.VMEM((LANES + DIFF_ONES_ROWS, 2 * tq), F32),
                        pltpu.VMEM((tk, 2 * tq), F32),
                        pltpu.VMEM((tk, 2 * tq), F32),
                        pltpu.VMEM((tk, 2 * tq), BF16),
                        pltpu.VMEM((tk, 2 * tq), BF16)],
    )
    return pl.pallas_call(
        functools.partial(_diff_body, tk=tk, lambda_init=lambda_init),
        grid_spec=grid_spec,
        out_shape=jax.ShapeDtypeStruct((t, DIFF_HEADS * 2 * HEAD_DIM), BF16),
        compiler_params=_cparams(("parallel", "parallel", "arbitrary")),
        name="diff_attn",
    )(lam, hcat, hcat, hcat, subln_g.astype(F32).reshape(LANES, 1))


def _swa_body(sink_ref, q_ref, kv_ref, o_ref, kvt_ref):
    kvh = pl.program_id(1)
    nq = SWA_BLOCK
    nsub = q_ref.shape[0] // nq
    seq = kv_ref.shape[0]
    nblk = seq // SWA_BLOCK
    wb = min(3, nblk)
    nk = wb * SWA_BLOCK

    @pl.when(pl.program_id(2) == 0)
    def _():
        for c in range(nblk):
            kvt_ref[c] = kv_ref[c * SWA_BLOCK:(c + 1) * SWA_BLOCK, :].astype(F32).T.astype(BF16)

    group = SWA_Q_HEADS // SWA_KV_HEADS
    half = LANES // 2

    def swap_halves(a):
        return jnp.concatenate([a[half:], a[:half]], axis=0)

    top = lax.broadcasted_iota(I32, (LANES, nq), 0) < half
    sink = jnp.concatenate([jnp.full((1, nq), sink_ref[kvh * group + g], F32) for g in range(group)], axis=1)

    scores, kvts = [], []
    for sub in range(nsub):
        n = nsub * pl.program_id(2) + sub
        b0 = jnp.clip(n - 1, 0, nblk - wb)
        start = pl.multiple_of(b0 * SWA_BLOCK, SWA_BLOCK)
        cols = []
        for c in range(group // 2):
            qt = q_ref[sub * nq:(sub + 1) * nq, c * LANES:(c + 1) * LANES].astype(F32).T
            cols.append(jnp.where(top, qt, 0.0))
            cols.append(jnp.where(top, swap_halves(qt), 0.0))
        rhs = jnp.concatenate(cols, axis=1).astype(BF16)
        st = jnp.dot(kv_ref[pl.ds(start, nk), :], rhs, preferred_element_type=F32)
        kpos = start + lax.broadcasted_iota(I32, (nk, nq), 0)
        qpos = n * SWA_BLOCK + lax.broadcasted_iota(I32, (nk, nq), 1)
        mask = jnp.where(jnp.abs(kpos - qpos) <= SWA_WINDOW, 0.0, NEG_BIG)
        scores.append(st + jnp.concatenate([mask] * group, axis=1))
        kvts.append(jnp.concatenate([kvt_ref[b0 + w] for w in range(wb)], axis=1))
    for sub in range(nsub):
        st = scores[sub]
        m = jnp.maximum(jnp.max(st, axis=0, keepdims=True), sink)
        e = jnp.exp2(st - m)
        den = jnp.sum(e, axis=0, keepdims=True) + jnp.exp2(sink - m)
        ot = jnp.dot(kvts[sub], e.astype(BF16), preferred_element_type=F32) / den
        for c in range(group // 2):
            even = ot[:, 2 * c * nq:(2 * c + 1) * nq]
            odd = ot[:, (2 * c + 1) * nq:(2 * c + 2) * nq]
            blk = jnp.where(top, swap_halves(even), odd)
            o_ref[sub * nq:(sub + 1) * nq, c * LANES:(c + 1) * LANES] = blk.T.astype(o_ref.dtype)


def _swa_attention(hcat, sinks, batch, seq):
    t = hcat.shape[0]
    nb = seq // SWA_BLOCK
    group = SWA_Q_HEADS // SWA_KV_HEADS
    qw = group * HEAD_DIM
    kvoff = SWA_Q_HEADS * HEAD_DIM // LANES
    nsub = min(SWA_SUB, nb)
    nstep = nb // nsub
    grid_spec = pltpu.PrefetchScalarGridSpec(
        num_scalar_prefetch=1,
        grid=(batch, SWA_KV_HEADS, nstep),
        in_specs=[
            pl.BlockSpec((nsub * SWA_BLOCK, qw), lambda b, h, n, s: (b * nstep + n, h)),
            pl.BlockSpec((seq, LANES), lambda b, h, n, s: (b, kvoff + h)),
        ],
        out_specs=pl.BlockSpec((nsub * SWA_BLOCK, qw), lambda b, h, n, s: (b * nstep + n, h)),
        scratch_shapes=[pltpu.VMEM((nb, LANES, SWA_BLOCK), BF16)],
    )
    return pl.pallas_call(
        _swa_body,
        grid_spec=grid_spec,
        out_shape=jax.ShapeDtypeStruct((t, SWA_Q_HEADS * HEAD_DIM), BF16),
        compiler_params=_cparams(("parallel", "parallel", "arbitrary")),
        name="swa_attn",
    )(sinks.astype(F32) * LOG2E, hcat, hcat)


def _outproj_body(*refs, n_in):
    a_refs = refs[:n_in]
    w_ref, x_ref, g_ref, b_ref, whi_ref, wlo_ref, rbias_ref = refs[n_in:n_in + 7]
    o_ref, opk_ref, idx_ref, gate_ref, pos_ref, cnt_ref, base_ref = refs[n_in + 7:]
    acc = None
    off = 0
    for a_ref in a_refs:
        ka = a_ref.shape[1]
        d = jnp.dot(a_ref[...], w_ref[off:off + ka, :], preferred_element_type=F32)
        acc = d if acc is None else acc + d
        off += ka
    y = DN_ALPHA * x_ref[...] + acc
    out = _layer_norm(y, g_ref[...], b_ref[...])
    o_ref[...] = out
    packed = _pack_rows(out)
    for c in range(PACK_SPLIT):
        opk_ref[c] = packed[:, c * LANES:(c + 1) * LANES]
    _route(out, whi_ref, wlo_ref, rbias_ref, idx_ref, gate_ref, pos_ref, cnt_ref, base_ref)


def _outproj_ln_route(acts, w_bf, x2, g, b, router_w, router_bias, tm=512):
    t, d = x2.shape
    tm = min(tm, t)
    wt = router_w.astype(F32).T
    whi = wt.astype(BF16)
    wlo = (wt - whi.astype(F32)).astype(BF16)
    in_specs = [pl.BlockSpec((tm, a.shape[1]), lambda i: (i, 0)) for a in acts]
    in_specs += [pl.BlockSpec(w_bf.shape, lambda i: (0, 0)),
                 pl.BlockSpec((tm, d), lambda i: (i, 0)),
                 pl.BlockSpec((1, d), lambda i: (0, 0)),
                 pl.BlockSpec((1, d), lambda i: (0, 0)),
                 pl.BlockSpec((N_EXPERTS, d), lambda i: (0, 0)),
                 pl.BlockSpec((N_EXPERTS, d), lambda i: (0, 0)),
                 pl.BlockSpec((N_EXPERTS, 1), lambda i: (0, 0))]
    return pl.pallas_call(
        functools.partial(_outproj_body, n_in=len(acts)),
        grid=(t // tm,),
        in_specs=in_specs,
        out_specs=[pl.BlockSpec((tm, d), lambda i: (i, 0)),
                   pl.BlockSpec((PACK_SPLIT, tm, LANES), lambda i: (0, i, 0)),
                   pl.BlockSpec((TOP_K, tm), lambda i: (0, i)),
                   pl.BlockSpec((TOP_K, tm), lambda i: (0, i)),
                   pl.BlockSpec((TOP_K, tm), lambda i: (0, i)),
                   pl.BlockSpec((N_EXPERTS, 1), lambda i: (0, 0))],
        out_shape=[jax.ShapeDtypeStruct((t, d), F32),
                   jax.ShapeDtypeStruct((PACK_SPLIT, t, LANES), I32),
                   jax.ShapeDtypeStruct((TOP_K, t), I32),
                   jax.ShapeDtypeStruct((TOP_K, t), F32),
                   jax.ShapeDtypeStruct((TOP_K, t), I32),
                   jax.ShapeDtypeStruct((N_EXPERTS, 1), F32)],
        scratch_shapes=[pltpu.VMEM((N_EXPERTS, 1), F32)],
        compiler_params=_cparams(("arbitrary",)),
        name="outproj_ln_route",
    )(*acts, w_bf, x2, g.astype(F32).reshape(1, d), b.astype(F32).reshape(1, d),
      whi, wlo, router_bias.astype(F32).reshape(N_EXPERTS, 1))


def _route(h, whi_ref, wlo_ref, bias_ref, idx_ref, gate_ref, pos_ref, cnt_ref, base_ref):
    i = pl.program_id(0)
    tm = h.shape[0]
    gsz = N_EXPERTS // N_GROUPS

    @pl.when(i == 0)
    def _():
        base_ref[...] = jnp.zeros_like(base_ref)

    h_hi = h.astype(BF16)
    h_lo = (h - h_hi.astype(F32)).astype(BF16)
    dn = (((1,), (1,)), ((), ()))
    whi = whi_ref[...]
    logits = (lax.dot_general(whi, h_hi, dn, preferred_element_type=F32)
              + lax.dot_general(whi, h_lo, dn, preferred_element_type=F32)
              + lax.dot_general(wlo_ref[...], h_hi, dn, preferred_element_type=F32))
    scores = 1.0 / (1.0 + jnp.exp(-logits))
    choice = scores + bias_ref[...]

    iota_g = lax.broadcasted_iota(I32, (gsz, tm), 0)
    gscore = []
    for g in range(N_GROUPS):
        cg = choice[g * gsz:(g + 1) * gsz, :]
        m1 = jnp.max(cg, axis=0, keepdims=True)
        first = jnp.min(jnp.where(cg == m1, iota_g, gsz), axis=0, keepdims=True)
        m2 = jnp.max(jnp.where(iota_g == first, -jnp.inf, cg), axis=0, keepdims=True)
        gscore.append(m1 + m2)
    pieces = []
    for g in range(N_GROUPS):
        rank = jnp.zeros((1, tm), I32)
        for o in range(N_GROUPS):
            if o == g:
                continue
            beats = (gscore[o] > gscore[g]) if o > g else (gscore[o] >= gscore[g])
            rank = rank + beats.astype(I32)
        keep = rank < TOPK_GROUPS
        pieces.append(jnp.where(keep, choice[g * gsz:(g + 1) * gsz, :], -jnp.inf))
    masked = jnp.concatenate(pieces, axis=0)

    iota_e = lax.broadcasted_iota(I32, (N_EXPERTS, tm), 0)
    sel_all = jnp.zeros((N_EXPERTS, tm), F32)
    idxs, gates, sels = [], [], []
    for _ in range(TOP_K):
        mx = jnp.max(masked, axis=0, keepdims=True)
        idx = jnp.min(jnp.where(masked == mx, iota_e, N_EXPERTS), axis=0, keepdims=True)
        sel = iota_e == idx
        gates.append(jnp.sum(jnp.where(sel, scores, 0.0), axis=0, keepdims=True))
        masked = jnp.where(sel, -jnp.inf, masked)
        sel_all = sel_all + sel.astype(F32)
        idxs.append(idx)
        sels.append(sel)
    gsum = gates[0]
    for gk in gates[1:]:
        gsum = gsum + gk
    gate_ref[...] = jnp.concatenate(gates, axis=0) / gsum * ROUTED_SCALE
    idx_ref[...] = jnp.concatenate(idxs, axis=0)

    tri = (lax.broadcasted_iota(I32, (tm, tm), 0) < lax.broadcasted_iota(I32, (tm, tm), 1))
    cum = jnp.dot(sel_all.astype(BF16), tri.astype(F32).astype(BF16), preferred_element_type=F32)
    tot = cum + base_ref[...]
    pos = [jnp.sum(jnp.where(sel, tot, 0.0), axis=0, keepdims=True) for sel in sels]
    pos_ref[...] = jnp.concatenate(pos, axis=0).astype(I32)
    base_ref[...] = base_ref[...] + jnp.sum(sel_all, axis=1, keepdims=True)
    cnt_ref[...] = base_ref[...]


def _expert_body(be_ref, na_ref, bv_ref, ord_ref, nxt_ref, x_ref, wg_hbm, wu_hbm, wd_hbm, y_ref,
                 wgu, wdb, wg_buf, wu_buf, wd_buf, sem):
    i = pl.program_id(0)
    bm = x_ref.shape[1]
    ff = wd_hbm.shape[1]
    hm = bm // EXPERT_CHAINS

    def weight_copies(e, slot):
        return [pltpu.make_async_copy(hbm.at[e], buf.at[slot], sem.at[slot])
                for hbm, buf in ((wg_hbm, wg_buf), (wu_hbm, wu_buf), (wd_hbm, wd_buf))]

    @pl.when(i < na_ref[0])
    def _():
        prev = be_ref[jnp.maximum(i - 1, 0)]
        slot = ord_ref[i] % 2

        @pl.when(i == 0)
        def _():
            for cp in weight_copies(be_ref[0], 0):
                cp.start()

        @pl.when((i == 0) | (be_ref[i] != prev))
        def _():
            for cp in weight_copies(be_ref[i], slot):
                cp.wait()
            wgu[:, :ff] = wg_buf[slot].astype(BF16)
            wgu[:, ff:] = wu_buf[slot].astype(BF16)
            wdb[...] = wd_buf[slot].astype(BF16)

            @pl.when(nxt_ref[i] >= 0)
            def _():
                for cp in weight_copies(nxt_ref[i], 1 - slot):
                    cp.start(priority=1)

        valid = bv_ref[i]
        row = lax.broadcasted_iota(I32, (hm, LANES), 0)
        def run(n_chains):
            gus = []
            for hb in range(n_chains):
                rows = slice(hb * hm, (hb + 1) * hm)
                keep = row < valid - hb * hm
                x = _unpack_rows(lambda c: jnp.where(keep, x_ref[c, rows, :], 0)).astype(BF16)
                gus.append(jnp.dot(x, wgu[...], preferred_element_type=F32))
            ys = []
            for gu in gus:
                hmid = (_silu(gu[:, :ff]) * gu[:, ff:]).astype(BF16)
                ys.append(jnp.dot(hmid, wdb[...], preferred_element_type=F32))
            for hb, y in enumerate(ys):
                packed = _pack_rows(y)
                for c in range(PACK_SPLIT):
                    y_ref[c, hb * hm:(hb + 1) * hm, :] = packed[:, c * LANES:(c + 1) * LANES]
            if n_chains < EXPERT_CHAINS:
                y_ref[:, n_chains * hm:, :] = jnp.zeros((PACK_SPLIT, bm - n_chains * hm, LANES), I32)

        for n_chains in range(1, EXPERT_CHAINS + 1):
            lo = (n_chains - 1) * hm
            cond = valid > lo if n_chains == EXPERT_CHAINS else (valid > lo) & (valid <= lo + hm)
            pl.when(cond)(functools.partial(run, n_chains))


def _expert_matmul(xs, blk_expert, n_active, blk_valid, w_gate, w_up, w_down):
    n_slots = xs.shape[1]
    bm = EXPERT_BLOCK
    n_blocks = n_slots // bm
    d, ff = w_gate.shape[1], w_gate.shape[2]

    blk = jnp.arange(n_blocks, dtype=I32)
    change = jnp.concatenate([jnp.zeros((1,), I32), (blk_expert[1:] != blk_expert[:-1]).astype(I32)])
    ordinal = jnp.sum(jnp.where(blk[None, :] <= blk[:, None], change[None, :], 0), axis=1).astype(I32)
    later = (ordinal[None, :] == ordinal[:, None] + 1) & (blk[None, :] < n_active[0])
    nxt = jnp.max(jnp.where(later, blk_expert[None, :], -1), axis=1).astype(I32)

    def row_map(i, be, na, bv, od, nx):
        return (0, jnp.minimum(i, na[0] - 1), 0)

    grid_spec = pltpu.PrefetchScalarGridSpec(
        num_scalar_prefetch=5,
        grid=(n_blocks,),
        in_specs=[pl.BlockSpec((PACK_SPLIT, bm, LANES), row_map),
                  pl.BlockSpec(memory_space=pl.ANY),
                  pl.BlockSpec(memory_space=pl.ANY),
                  pl.BlockSpec(memory_space=pl.ANY)],
        out_specs=pl.BlockSpec((PACK_SPLIT, bm, LANES), row_map),
        scratch_shapes=[pltpu.VMEM((d, 2 * ff), BF16), pltpu.VMEM((ff, d), BF16),
                        pltpu.VMEM((2, d, ff), F32), pltpu.VMEM((2, d, ff), F32), pltpu.VMEM((2, ff, d), F32),
                        pltpu.SemaphoreType.DMA((2,))],
    )
    return pl.pallas_call(
        _expert_body,
        grid_spec=grid_spec,
        out_shape=jax.ShapeDtypeStruct((PACK_SPLIT, n_slots, LANES), I32),
        compiler_params=_cparams(("arbitrary",)),
        name="expert_mlp",
    )(blk_expert, n_active, blk_valid, ordinal, nxt, xs, w_gate, w_up, w_down)


def _slot_body(idx_ref, pos_ref, ps_ref, slot_ref, *, n_slots):
    tm = idx_ref.shape[1]
    iota_e = lax.broadcasted_iota(I32, (N_EXPERTS, tm), 0)
    ps = ps_ref[...]
    rows = []
    for k in range(TOP_K):
        start = jnp.sum(jnp.where(iota_e == idx_ref[k:k + 1, :], ps, 0.0), axis=0, keepdims=True)
        rows.append(start.astype(I32) + pos_ref[k:k + 1, :])
    slot = jnp.concatenate(rows, axis=0)
    for c in range(PACK_SPLIT):
        slot_ref[c * TOP_K:(c + 1) * TOP_K, :] = slot + c * n_slots


def _slots(idx, pos, pad_start, n_slots, tm=512):
    t = idx.shape[1]
    tm = min(tm, t)
    return pl.pallas_call(
        functools.partial(_slot_body, n_slots=n_slots),
        grid=(t // tm,),
        in_specs=[pl.BlockSpec((TOP_K, tm), lambda i: (0, i)),
                  pl.BlockSpec((TOP_K, tm), lambda i: (0, i)),
                  pl.BlockSpec((N_EXPERTS, 1), lambda i: (0, 0))],
        out_specs=pl.BlockSpec((PACK_SPLIT * TOP_K, tm), lambda i: (0, i)),
        out_shape=jax.ShapeDtypeStruct((PACK_SPLIT * TOP_K, t), I32),
        compiler_params=_cparams(("parallel",)),
        name="slots",
    )(idx, pos, pad_start.astype(F32).reshape(N_EXPERTS, 1))


def _sc_mesh():
    return plsc.VectorSubcoreMesh(core_axis_name="core", subcore_axis_name="subcore",
                                  num_cores=SC_CORES, num_subcores=SC_SUBCORES)


def _sc_scatter_rows(rows, idx, n_out):
    t = idx.shape[1]
    nj = t // SC_WINDOW

    @functools.partial(pl.kernel, out_type=jax.ShapeDtypeStruct((n_out, LANES), I32),
                       mesh=_sc_mesh(), scratch_types=[], name="sc_dispatch")
    def run(rows_hbm, idx_hbm, out_hbm):
        def body(rows_vmem, idx_vmem):
            for k in range(TOP_K):
                pltpu.sync_copy(rows_vmem, out_hbm.at[idx_vmem.at[k]])

        pltpu.emit_pipeline(
            body,
            grid=(rows.shape[0] // SC_WINDOW,),
            in_specs=[pl.BlockSpec((SC_WINDOW, LANES), lambda s: (s, 0)),
                      pl.BlockSpec((TOP_K, SC_WINDOW), lambda s: (s // nj, s % nj))],
            out_specs=[],
            core_axis_name=("core", "subcore"),
            dimension_semantics=(pltpu.PARALLEL,),
        )(rows_hbm, idx_hbm)

    return run(rows, idx)


def _sc_gather_rows(src, idx):
    nr, t = idx.shape
    nj = t // SC_WINDOW

    @functools.partial(pl.kernel, out_type=jax.ShapeDtypeStruct((nr * t, LANES), I32),
                       mesh=_sc_mesh(), scratch_types=[], name="sc_combine")
    def run(src_hbm, idx_hbm, out_hbm):
        def body(idx_vmem, out_vmem):
            pltpu.sync_copy(src_hbm.at[idx_vmem.at[0]], out_vmem)

        pltpu.emit_pipeline(
            body,
            grid=(nr * nj,),
            in_specs=[pl.BlockSpec((1, SC_WINDOW), lambda s: (s // nj, s % nj))],
            out_specs=[pl.BlockSpec((SC_WINDOW, LANES), lambda s: (s, 0))],
            core_axis_name=("core", "subcore"),
            dimension_semantics=(pltpu.PARALLEL,),
        )(idx_hbm, out_hbm)

    return run(src, idx)


def _combine_body(h_ref, yg_ref, gt_ref, sg_ref, su_ref, sd_ref, g_ref, b_ref, *rest, proj, n_alias):
    n_out = 1 if proj is None else 2
    rest = rest[:len(rest) - n_out - n_alias] + rest[len(rest) - n_out:]
    h = h_ref[...]
    hb = h.astype(BF16)
    a = jnp.dot(hb, sg_ref[...], preferred_element_type=F32)
    u = jnp.dot(hb, su_ref[...], preferred_element_type=F32)
    moe = jnp.dot((_silu(a) * u).astype(BF16), sd_ref[...], preferred_element_type=F32)
    gt = gt_ref[...]
    for k in range(TOP_K):
        moe = moe + gt[:, k:k + 1] * _unpack_rows(lambda c, k=k: yg_ref[c, k])
    out = _layer_norm(DN_ALPHA * h + moe, g_ref[...], b_ref[...])
    if proj is None:
        rest[0][...] = out
    else:
        w_ref, tab_ref, o_ref, hcat_ref = rest
        o_ref[...] = out
        _project_rope(out.astype(BF16), w_ref, tab_ref, hcat_ref, *proj)


def _combine_ln(h, yg, gates_t, sh_gate, sh_up, sh_down, g, b, row0, prev, next_proj=None, tm=256):
    t, d = h.shape
    tc = yg.shape[2]
    tm = min(tm, tc) if next_proj is None else min(tm, tc, next_proj[3])
    off = row0 // tm
    ff = sh_gate.shape[1]
    in_specs = [pl.BlockSpec((tm, d), lambda i: (i + off, 0)),
                pl.BlockSpec((PACK_SPLIT, TOP_K, tm, LANES), lambda i: (0, 0, i, 0)),
                pl.BlockSpec((tm, TOP_K), lambda i: (i, 0)),
                pl.BlockSpec((d, ff), lambda i: (0, 0)),
                pl.BlockSpec((d, ff), lambda i: (0, 0)),
                pl.BlockSpec((ff, d), lambda i: (0, 0)),
                pl.BlockSpec((1, d), lambda i: (0, 0)),
                pl.BlockSpec((1, d), lambda i: (0, 0))]
    out_specs = [pl.BlockSpec((tm, d), lambda i: (i + off, 0))]
    out_shape = [jax.ShapeDtypeStruct((t, d), F32)]
    operands = [h, yg, gates_t, sh_gate.astype(BF16), sh_up.astype(BF16), sh_down.astype(BF16),
                g.astype(F32).reshape(1, d), b.astype(F32).reshape(1, d)]
    proj = None
    if next_proj is not None:
        w_bf, tab, rope_kind, seq = next_proj
        sb = seq // tm
        n_out = w_bf.shape[1]
        in_specs += [pl.BlockSpec((d, n_out), lambda i: (0, 0)),
                     pl.BlockSpec((tm, tab.shape[1]), lambda i: ((i + off) % sb, 0))]
        out_specs.append(pl.BlockSpec((tm, n_out), lambda i: (i + off, 0)))
        out_shape.append(jax.ShapeDtypeStruct((t, n_out), BF16))
        operands += [w_bf, tab]
        proj = (PROJ_CHUNK, tuple(rope_kind))
    aliases = {}
    if prev is not None:
        for k, p in enumerate(prev):
            aliases[len(operands)] = k
            in_specs.append(pl.BlockSpec(memory_space=pl.ANY))
            operands.append(p)
    return pl.pallas_call(
        functools.partial(_combine_body, proj=proj, n_alias=len(aliases)),
        grid=(tc // tm,),
        in_specs=in_specs,
        out_specs=out_specs,
        out_shape=out_shape,
        input_output_aliases=aliases,
        compiler_params=_cparams(("parallel",)),
        name="combine_ln",
    )(*operands)


def _mixer_out_and_moe(acts, w_out, x2, ln1_g, ln1_b, router_w, router_bias, w_gate, w_up, w_down,
                       sh_gate, sh_up, sh_down, g, b, next_proj=None):
    h, h_pk, idx, gates, pos, cnt = _outproj_ln_route(acts, w_out.astype(BF16), x2, ln1_g, ln1_b,
                                                      router_w, router_bias)
    t, d = h.shape
    bm = EXPERT_BLOCK
    counts = cnt[:, 0].astype(I32)
    padded = (counts + bm - 1) // bm * bm
    e_iota = jnp.arange(N_EXPERTS, dtype=I32)
    pad_end = jnp.sum(jnp.where(e_iota[None, :] <= e_iota[:, None], padded[None, :], 0), axis=1)
    pad_start = pad_end - padded
    n_blocks = t * TOP_K // bm + N_EXPERTS
    n_slots = n_blocks * bm
    blk_start = jnp.arange(n_blocks, dtype=I32) * bm
    blk_expert = jnp.minimum(
        jnp.sum((blk_start[:, None] >= pad_end[None, :]).astype(I32), axis=1), N_EXPERTS - 1)
    real_end = jnp.sum(jnp.where(e_iota[None, :] == blk_expert[:, None], (pad_start + counts)[None, :], 0), axis=1)
    blk_valid = jnp.clip(real_end - blk_start, 0, bm).astype(I32)
    n_active = (pad_end[-1:] // bm).astype(I32)

    slot4 = _slots(idx, pos, pad_start, n_slots)
    xs = _sc_scatter_rows(h_pk.reshape(PACK_SPLIT * t, LANES), slot4, PACK_SPLIT * n_slots)
    ys = _expert_matmul(xs.reshape(PACK_SPLIT, n_slots, LANES), blk_expert, n_active, blk_valid,
                        w_gate, w_up, w_down)
    ys_flat = ys.reshape(PACK_SPLIT * n_slots, LANES)
    gates_t = gates.T
    tc = t // COMBINE_CHUNKS
    outs = None
    for c in range(COMBINE_CHUNKS):
        rows = slice(c * tc, (c + 1) * tc)
        yg = _sc_gather_rows(ys_flat, slot4[:, rows])
        outs = _combine_ln(h, yg.reshape(PACK_SPLIT, TOP_K, tc, LANES), gates_t[rows], sh_gate, sh_up, sh_down,
                           g, b, c * tc, outs, next_proj=next_proj)
    return outs if next_proj is not None else outs[0]


def kernel(x, l0_w_in, l0_w_out, l0_na_rpb, l0_diff_lq1, l0_diff_lk1, l0_diff_lq2, l0_diff_lk2, l0_diff_subln_g, l0_ln1_g, l0_ln1_b, l0_router_w, l0_router_bias, l0_expert_w_gate, l0_expert_w_up, l0_expert_w_down, l0_shared_w_gate, l0_shared_w_up, l0_shared_w_down, l0_ln2_g, l0_ln2_b, l1_w_in, l1_w_out, l1_swa_sinks, l1_ln1_g, l1_ln1_b, l1_router_w, l1_router_bias, l1_expert_w_gate, l1_expert_w_up, l1_expert_w_down, l1_shared_w_gate, l1_shared_w_up, l1_shared_w_down, l1_ln2_g, l1_ln2_b):
    batch, seq, d = x.shape
    t = batch * seq
    x2 = x.reshape(t, d).astype(F32)
    tab = _rope_table(seq)
    qscale = HEAD_DIM ** -0.5 * LOG2E
    na_w = NA_HEADS * HEAD_DIM
    dq_w = DIFF_HEADS * 2 * HEAD_DIM

    col_scale = jnp.concatenate([
        jnp.full((na_w,), qscale, F32), jnp.ones((2 * na_w,), F32),
        jnp.full((dq_w,), qscale, F32), jnp.ones((2 * dq_w,), F32)])
    w_in0 = (l0_w_in.astype(F32) * col_scale).astype(BF16)
    per = LANES
    rope0 = [0] * (3 * na_w // per) + [1] * (2 * dq_w // per) + [0] * (dq_w // per)
    hcat0 = _inproj(x2, w_in0, tab, rope0, seq)
    oa = _na_attention(hcat0, l0_na_rpb, batch, seq)
    lambda_init = 0.8 - 0.6 * math.exp(-0.3 * 0)
    lam = (jnp.exp(jnp.sum(l0_diff_lq1.astype(F32) * l0_diff_lk1.astype(F32)))
           - jnp.exp(jnp.sum(l0_diff_lq2.astype(F32) * l0_diff_lk2.astype(F32))) + lambda_init)
    od = _diff_attention(hcat0, lam.reshape(1).astype(F32), l0_diff_subln_g, lambda_init, batch, seq)
    q_w = SWA_Q_HEADS * HEAD_DIM
    kv_w = SWA_KV_HEADS * HEAD_DIM
    w1 = l1_w_in.astype(F32)
    wq = w1[:, :q_w] * qscale
    wk = w1[:, q_w:q_w + kv_w].reshape(d, SWA_KV_HEADS, HEAD_DIM)
    wv = w1[:, q_w + kv_w:].reshape(d, SWA_KV_HEADS, HEAD_DIM)
    wkv = jnp.concatenate([wk, wv], axis=-1).reshape(d, 2 * kv_w)
    w_in1 = jnp.concatenate([wq, wkv], axis=1).astype(BF16)
    rope1 = [1] * (q_w // per) + [2] * (2 * kv_w // per)

    x2, hcat1 = _mixer_out_and_moe(
        [oa, od], l0_w_out, x2, l0_ln1_g, l0_ln1_b, l0_router_w, l0_router_bias,
        l0_expert_w_gate, l0_expert_w_up, l0_expert_w_down,
        l0_shared_w_gate, l0_shared_w_up, l0_shared_w_down, l0_ln2_g, l0_ln2_b,
        next_proj=(w_in1, tab, rope1, seq))

    o1 = _swa_attention(hcat1, l1_swa_sinks, batch, seq)
    x2 = _mixer_out_and_moe(
        [o1], l1_w_out, x2, l1_ln1_g, l1_ln1_b, l1_router_w, l1_router_bias,
        l1_expert_w_gate, l1_expert_w_up, l1_expert_w_down,
        l1_shared_w_gate, l1_shared_w_up, l1_shared_w_down, l1_ln2_g, l1_ln2_b)
    return x2.reshape(batch, seq, d).astype(x.dtype)
```
